```python
import math
import jax, jax.numpy as jnp
from jax import lax
import numpy as np

D_MODEL = 1024
BATCH = 4
SEQ = 4096
DEPTH = 1

ALPHA = (2.0 * DEPTH) ** 0.25
BETA = (8.0 * DEPTH) ** -0.25
LN_EPS = 1e-5

POOL_WINDOWS = (2, 4, 8, 16)
POOL_WIDTH = D_MODEL
POOL_GROUP = POOL_WIDTH // len(POOL_WINDOWS)

MLSTM_HEADS = 4
MLSTM_INNER = 2 * D_MODEL
DV = MLSTM_INNER // MLSTM_HEADS
DQK = DV // 2
CONV_WIDTH = 5
CHUNK = 64

QK_COLS = 2 * MLSTM_HEADS * DQK
V_COLS = MLSTM_HEADS * DV
O_COLS = MLSTM_HEADS * DV
IF_COLS = 4 * MLSTM_HEADS
GATE_COLS = 2 * D_MODEL
IN_COLS = POOL_WIDTH + QK_COLS + V_COLS + O_COLS + IF_COLS + GATE_COLS

N_EXPERTS = 256
TOP_K = 8
N_GROUPS = 8
TOPK_GROUPS = 4
D_EXPERT = 256
D_SHARED = 256
ROUTED_SCALE = 2.5
BLK = 128

kernel_name = "hybrid_pool_mlstm_moe_encoder"


def layer_norm(x, g, b):
    xf = x.astype(jnp.float32)
    mu = jnp.mean(xf, axis=-1, keepdims=True)
    var = jnp.mean(jnp.square(xf - mu), axis=-1, keepdims=True)
    y = (xf - mu) * lax.rsqrt(var + LN_EPS) * g.astype(jnp.float32) + b.astype(jnp.float32)
    return y.astype(x.dtype)


def multiscale_pool(u, pool_w, pool_scale):
    bsz, s, p = u.shape
    uf = u.astype(jnp.float32)
    cs = jnp.concatenate([jnp.zeros((bsz, 1, p), jnp.float32), jnp.cumsum(uf, axis=1)], axis=1)
    pos = jnp.arange(s)
    outs = []
    for gi, w in enumerate(POOL_WINDOWS):
        lo = jnp.clip(pos - w // 2, 0, s)
        hi = jnp.clip(pos + w // 2, 0, s)
        csg = cs[..., gi * POOL_GROUP:(gi + 1) * POOL_GROUP]
        cnt = (hi - lo).astype(jnp.float32)[None, :, None]
        outs.append((jnp.take(csg, hi, axis=1) - jnp.take(csg, lo, axis=1)) / cnt)
    pooled = (jnp.concatenate(outs, axis=-1) - uf).astype(u.dtype)
    pooled = pooled.reshape(bsz, s, len(POOL_WINDOWS), POOL_GROUP)
    mixed = jnp.einsum('bsgc,gcd->bsgd', pooled, pool_w).reshape(bsz, s, p)
    return mixed * pool_scale


def centred_depthwise_conv(u, w):
    c = u.shape[-1]
    pad = CONV_WIDTH // 2
    return lax.conv_general_dilated(
        u, w.astype(u.dtype)[:, None, :], window_strides=(1,), padding=[(pad, pad)],
        dimension_numbers=('NWC', 'WIO', 'NWC'), feature_group_count=c)


def _mlstm_chunk_step(carry, xs):
    c_prev, n_prev, m_prev = carry
    q, k, v, li, lf = xs
    length = q.shape[2]
    b = jnp.cumsum(lf, axis=-1)
    tri = jnp.tril(jnp.ones((length, length), bool))
    d_intra = jnp.where(tri, b[..., :, None] - b[..., None, :] + li[..., None, :], -jnp.inf)
    m_inter = b + m_prev[..., None]
    m_t = jnp.maximum(m_inter, jnp.max(d_intra, axis=-1))
    p = jnp.exp(d_intra - m_t[..., None]) * jnp.einsum('nhtk,nhsk->nhts', q, k)
    w_inter = jnp.exp(m_inter - m_t)
    num = jnp.einsum('nhts,nhsv->nhtv', p, v) + w_inter[..., None] * jnp.einsum('nhtk,nhkv->nhtv', q, c_prev)
    den = jnp.sum(p, axis=-1) + w_inter * jnp.einsum('nhtk,nhk->nht', q, n_prev)
    h = num / jnp.maximum(jnp.abs(den), jnp.exp(-m_t))[..., None]
    b_last = b[..., -1]
    g = b_last[..., None] - b + li
    m_new = jnp.maximum(b_last + m_prev, jnp.max(g, axis=-1))
    decay = jnp.exp(b_last + m_prev - m_new)
    wk = jnp.exp(g - m_new[..., None])[..., None] * k
    c_new = decay[..., None, None] * c_prev + jnp.einsum('nhsk,nhsv->nhkv', wk, v)
    n_new = decay[..., None] * n_prev + jnp.sum(wk, axis=2)
    return (c_new, n_new, m_new), h


def bidirectional_mlstm(q, k, v, li_f, lf_f, li_b, lf_b):
    bsz, nh, s, _ = q.shape
    flip = lambda a: jnp.flip(a, axis=2)
    qq = jnp.concatenate([q, flip(q)], axis=0)
    kk = jnp.concatenate([k, flip(k)], axis=0)
    vv = jnp.concatenate([v, flip(v)], axis=0)
    li = jnp.concatenate([li_f, flip(li_b)], axis=0)
    lf = jnp.concatenate([lf_f, flip(lf_b)], axis=0)
    nb = 2 * bsz
    nc = s // CHUNK

    def to_chunks(a):
        return jnp.moveaxis(a.reshape(a.shape[:2] + (nc, CHUNK) + a.shape[3:]), 2, 0)

    init = (jnp.zeros((nb, nh, DQK, DV), jnp.float32),
            jnp.zeros((nb, nh, DQK), jnp.float32),
            jnp.zeros((nb, nh), jnp.float32))
    _, h = lax.scan(_mlstm_chunk_step, init, tuple(to_chunks(a) for a in (qq, kk, vv, li, lf)))
    h = jnp.moveaxis(h, 0, 2).reshape(nb, nh, s, DV)
    return h[:bsz] + flip(h[bsz:])


def hybrid_mixer(x, w_in, b_if, b_gate, conv_qk, pool_w, pool_scale, mh_norm_w, w_b_down, w_out):
    bsz, s, _ = x.shape
    proj = x @ w_in
    sizes = (POOL_WIDTH, QK_COLS, V_COLS, O_COLS, IF_COLS, GATE_COLS)
    offs = np.cumsum((0,) + sizes)
    u_pool, u_qk, u_v, u_o, u_if, u_gate = (proj[..., offs[i]:offs[i + 1]] for i in range(len(sizes)))

    branch_a = multiscale_pool(u_pool, pool_w, pool_scale)

    qk = jax.nn.silu(centred_depthwise_conv(u_qk, conv_qk))
    heads = lambda a, d: a.reshape(bsz, s, MLSTM_HEADS, d).transpose(0, 2, 1, 3).astype(jnp.float32)
    q = heads(qk[..., :QK_COLS // 2], DQK) * (DQK ** -0.5)
    k = heads(qk[..., QK_COLS // 2:], DQK)
    v = heads(u_v, DV)
    gates = (u_if + b_if).astype(jnp.float32).reshape(bsz, s, 4, MLSTM_HEADS).transpose(2, 0, 3, 1)
    li_f, lf_f, li_b, lf_b = gates[0], jax.nn.log_sigmoid(gates[1]), gates[2], jax.nn.log_sigmoid(gates[3])
    h = bidirectional_mlstm(q, k, v, li_f, lf_f, li_b, lf_b)
    mu = jnp.mean(h, axis=-1, keepdims=True)
    var = jnp.mean(jnp.square(h - mu), axis=-1, keepdims=True)
    h = (h - mu) * lax.rsqrt(var + LN_EPS) * mh_norm_w.astype(jnp.float32)[None, :, None, :]
    h = h.transpose(0, 2, 1, 3).reshape(bsz, s, MLSTM_INNER).astype(x.dtype)
    h = jax.nn.sigmoid(u_o) * h
    branch_b = h @ w_b_down

    g = jax.nn.sigmoid(u_gate + b_gate)
    merged = g[..., :D_MODEL] * branch_a + g[..., D_MODEL:] * branch_b
    return merged @ w_out


def moe_ffn(xf, w_router, b_router, w_gate_e, w_up_e, w_down_e, w_gate_s, w_up_s, w_down_s):
    t = xf.shape[0]
    scores = jax.nn.sigmoid((xf @ w_router).astype(jnp.float32))
    sel = scores + b_router.astype(jnp.float32)
    per_group = N_EXPERTS // N_GROUPS
    grp_score = jnp.sum(lax.top_k(sel.reshape(t, N_GROUPS, per_group), 2)[0], axis=-1)
    _, gidx = lax.top_k(grp_score, TOPK_GROUPS)
    gmask = jnp.zeros((t, N_GROUPS), bool).at[jnp.arange(t)[:, None], gidx].set(True)
    sel = jnp.where(jnp.repeat(gmask, per_group, axis=1), sel, -jnp.inf)
    _, eidx = lax.top_k(sel, TOP_K)
    w = jnp.take_along_axis(scores, eidx, axis=1)
    w = w / jnp.sum(w, axis=-1, keepdims=True) * ROUTED_SCALE

    n_assign = t * TOP_K
    flat_e = eidx.reshape(-1)
    flat_tok = jnp.repeat(jnp.arange(t, dtype=jnp.int32), TOP_K)
    flat_w = w.reshape(-1)
    order = jnp.argsort(flat_e, stable=True)
    se, stok, sw = flat_e[order], flat_tok[order], flat_w[order]
    counts = jnp.bincount(flat_e, length=N_EXPERTS)
    starts = jnp.cumsum(counts) - counts
    pcounts = ((counts + BLK - 1) // BLK) * BLK
    pends = jnp.cumsum(pcounts)
    pstarts = pends - pcounts
    dest = pstarts[se] + (jnp.arange(n_assign) - starts[se])
    n_blocks = -(-n_assign // BLK) + N_EXPERTS
    n_slots = n_blocks * BLK
    slot_tok = jnp.zeros((n_slots,), jnp.int32).at[dest].set(stok)
    slot_w = jnp.zeros((n_slots,), jnp.float32).at[dest].set(sw)
    block_e = jnp.clip(jnp.searchsorted(pends, jnp.arange(n_blocks) * BLK, side='right'), 0, N_EXPERTS - 1)

    def expert_block(args):
        tok, e = args
        xb = xf[tok]
        hb = jax.nn.silu(xb @ w_gate_e[e]) * (xb @ w_up_e[e])
        return hb @ w_down_e[e]

    yb = lax.map(expert_block, (slot_tok.reshape(n_blocks, BLK), block_e))
    yb = yb.reshape(n_slots, -1).astype(jnp.float32) * slot_w[:, None]
    routed = jax.ops.segment_sum(yb, slot_tok, num_segments=t).astype(xf.dtype)
    shared = (jax.nn.silu(xf @ w_gate_s) * (xf @ w_up_s)) @ w_down_s
    return routed + shared


def setup_inputs(seed: int = 0) -> dict:
    key = jax.random.key(seed)
    ks = jax.random.split(key, 24)
    f32 = jnp.float32
    nrm = lambda k, shape, scale: jax.random.normal(k, shape, f32) * scale
    L, D, H = DEPTH, D_MODEL, MLSTM_HEADS
    f_bias = jnp.broadcast_to(jnp.linspace(3.0, 6.0, H, dtype=f32), (L, H))
    b_if = jnp.concatenate([
        nrm(ks[2], (L, H), 0.1),
        f_bias + nrm(ks[3], (L, H), 0.1),
        nrm(ks[4], (L, H), 0.1),
        f_bias + nrm(ks[5], (L, H), 0.1)], axis=1)
    return {
        "x": nrm(ks[0], (BATCH, SEQ, D), 1.0),
        "w_in": nrm(ks[1], (L, D, IN_COLS), D ** -0.5),
        "b_if": b_if,
        "b_gate": nrm(ks[6], (L, GATE_COLS), 0.02),
        "conv_qk": nrm(ks[7], (L, CONV_WIDTH, QK_COLS), CONV_WIDTH ** -0.5),
        "pool_w": nrm(ks[8], (L, len(POOL_WINDOWS), POOL_GROUP, POOL_GROUP), POOL_GROUP ** -0.5),
        "pool_scale": 1.0 + nrm(ks[9], (L, POOL_WIDTH), 0.02),
        "mh_norm_w": 1.0 + nrm(ks[10], (L, H, DV), 0.02),
        "w_b_down": nrm(ks[11], (L, MLSTM_INNER, D), BETA * MLSTM_INNER ** -0.5),
        "w_out": nrm(ks[12], (L, D, D), BETA * D ** -0.5),
        "ln1_g": 1.0 + nrm(ks[13], (L, D), 0.02),
        "ln1_b": nrm(ks[14], (L, D), 0.02),
        "w_router": nrm(ks[15], (L, D, N_EXPERTS), D ** -0.5),
        "b_router": nrm(ks[16], (L, N_EXPERTS), 0.01),
        "w_gate_e": nrm(ks[17], (L, N_EXPERTS, D, D_EXPERT), D ** -0.5),
        "w_up_e": nrm(ks[18], (L, N_EXPERTS, D, D_EXPERT), D ** -0.5),
        "w_down_e": nrm(ks[19], (L, N_EXPERTS, D_EXPERT, D), BETA * D_EXPERT ** -0.5),
        "w_gate_s": nrm(ks[20], (L, D, D_SHARED), D ** -0.5),
        "w_up_s": nrm(ks[21], (L, D, D_SHARED), D ** -0.5),
        "w_down_s": nrm(ks[22], (L, D_SHARED, D), BETA * D_SHARED ** -0.5),
        "ln2_g": 1.0 + nrm(ks[23], (L, D), 0.02),
        "ln2_b": nrm(jax.random.fold_in(ks[23], 1), (L, D), 0.02),
    }


def reference(x, w_in, b_if, b_gate, conv_qk, pool_w, pool_scale, mh_norm_w, w_b_down, w_out,
              ln1_g, ln1_b, w_router, b_router, w_gate_e, w_up_e, w_down_e,
              w_gate_s, w_up_s, w_down_s, ln2_g, ln2_b):
    bsz, s, d = x.shape
    for l in range(DEPTH):
        mix = hybrid_mixer(x, w_in[l], b_if[l], b_gate[l], conv_qk[l], pool_w[l], pool_scale[l],
                           mh_norm_w[l], w_b_down[l], w_out[l])
        x = layer_norm(ALPHA * x + mix, ln1_g[l], ln1_b[l])
        ffn = moe_ffn(x.reshape(bsz * s, d), w_router[l], b_router[l], w_gate_e[l], w_up_e[l],
                      w_down_e[l], w_gate_s[l], w_up_s[l], w_down_s[l]).reshape(bsz, s, d)
        x = layer_norm(ALPHA * x + ffn, ln2_g[l], ln2_b[l])
    return x
```

```python
import functools

import jax
import jax.numpy as jnp
from jax import lax
from jax.experimental import pallas as pl
from jax.experimental.pallas import tpu as pltpu

F32 = jnp.float32
BF16 = jnp.bfloat16
I32 = jnp.int32
U32 = jnp.uint32

N_HEADS = 4
POOL_GROUPS = 4
CONV_WIDTH = 5
LN_EPS = 1e-5
N_ROUTE_GROUPS = 8
TOPK_GROUPS = 4
TOP_K = 8
ROUTED_SCALE = 2.5

LANES = 128
SUBLANES = 8
BF16_ROWS = 16
VMEM_LIMIT = 56 * 1024 * 1024

MLSTM_CHUNK = 256
SEQ_TILE = 256
ROW_TILE = 512
ROUTER_TILE = 256
EXPERT_TILE = 256
MOVE_TILE = 128

HI_MASK = 0xFFFF0000


def _cparams(sem):
    return pltpu.CompilerParams(dimension_semantics=sem, vmem_limit_bytes=VMEM_LIMIT)


def _sigmoid(x):
    return 1.0 / (1.0 + jnp.exp(-x))


def _layer_norm(y, g, b):
    mu = jnp.mean(y, axis=-1, keepdims=True)
    yc = y - mu
    var = jnp.mean(yc * yc, axis=-1, keepdims=True)
    return yc * lax.rsqrt(var + LN_EPS) * g + b


def _pack_bf16_pairs(y):
    c = y.shape[1] // 2
    bits = lax.bitcast_convert_type(y.astype(BF16).astype(F32), U32)
    return (bits[:, :c] >> 16) | (bits[:, c:] & jnp.uint32(HI_MASK))


def _unpack_bf16_pairs(p):
    lo = lax.bitcast_convert_type(p << 16, F32)
    hi = lax.bitcast_convert_type(p & jnp.uint32(HI_MASK), F32)
    return jnp.concatenate([lo, hi], axis=1)


def _matmul_kernel(x_ref, w_ref, b_ref, o_ref):
    acc = jnp.dot(x_ref[...], w_ref[...], preferred_element_type=F32)
    o_ref[...] = (acc + b_ref[...]).astype(o_ref.dtype)


def _matmul_bias(x, w, bias, out_dtype, tm, tn):
    m, k = x.shape
    n = w.shape[1]
    return pl.pallas_call(
        _matmul_kernel,
        grid=(m // tm, n // tn),
        in_specs=[pl.BlockSpec((tm, k), lambda i, j: (i, 0)),
                  pl.BlockSpec((k, tn), lambda i, j: (0, j)),
                  pl.BlockSpec((1, tn), lambda i, j: (0, j))],
        out_specs=pl.BlockSpec((tm, tn), lambda i, j: (i, j)),
        out_shape=jax.ShapeDtypeStruct((m, n), out_dtype),
        compiler_params=_cparams(("parallel", "parallel")),
        name="inproj",
    )(x, w, bias)


def _qkconv_kernel(prev_ref, main_ref, next_ref, w_ref, o_ref, *, ts, scale, transpose):
    t = pl.program_id(1)
    nt = pl.num_programs(1)
    main = main_ref[0].astype(F32)
    prev = prev_ref[0, ts - BF16_ROWS:ts, :].astype(F32)[BF16_ROWS - SUBLANES:]
    nxt = next_ref[0, 0:BF16_ROWS, :].astype(F32)[:SUBLANES]
    prev = jnp.where(t > 0, prev, 0.0)
    nxt = jnp.where(t < nt - 1, nxt, 0.0)
    ext = jnp.concatenate([prev, main, nxt], axis=0)
    w = w_ref[...]
    pad = CONV_WIDTH // 2
    acc = jnp.zeros_like(main)
    for j in range(CONV_WIDTH):
        off = SUBLANES - pad + j
        acc = acc + ext[off:off + ts] * w[j:j + 1]
    y = acc * _sigmoid(acc) * scale
    if transpose:
        o_ref[0] = y.T.astype(o_ref.dtype)
    else:
        o_ref[0] = y.astype(o_ref.dtype)


def _qkconv(proj3, conv_w, col0, scale, transpose):
    bsz, seq, _ = proj3.shape
    ts = SEQ_TILE
    nt = seq // ts
    cw = 2 * LANES
    ncol = conv_w.shape[1] // cw
    kern = functools.partial(_qkconv_kernel, ts=ts, scale=scale, transpose=transpose)
    if transpose:
        out_shape = jax.ShapeDtypeStruct((bsz, ncol * cw, seq), BF16)
        out_spec = pl.BlockSpec((1, cw, ts), lambda b, t, j: (b, j, t))
    else:
        out_shape = jax.ShapeDtypeStruct((bsz, seq, ncol * cw), BF16)
        out_spec = pl.BlockSpec((1, ts, cw), lambda b, t, j: (b, t, j))
    return pl.pallas_call(
        kern,
        grid=(bsz, nt, ncol),
        in_specs=[pl.BlockSpec((1, ts, cw), lambda b, t, j: (b, jnp.maximum(t - 1, 0), col0 + j)),
                  pl.BlockSpec((1, ts, cw), lambda b, t, j: (b, t, col0 + j)),
                  pl.BlockSpec((1, ts, cw), lambda b, t, j: (b, jnp.minimum(t + 1, nt - 1), col0 + j)),
                  pl.BlockSpec((CONV_WIDTH, cw), lambda b, t, j: (0, j))],
        out_specs=out_spec,
        out_shape=out_shape,
        compiler_params=_cparams(("parallel", "parallel", "parallel")),
        name="qkconv_t" if transpose else "qkconv",
    )(proj3, proj3, proj3, conv_w)


def _pool_kernel(prev_ref, main_ref, next_ref, pw_ref, ps_ref, o_ref, *, ts, seq):
    t = pl.program_id(1)
    nt = pl.num_programs(1)
    g = pl.program_id(2)
    hw = jnp.left_shift(1, g)
    main = main_ref[0]
    prev = prev_ref[0, ts - LANES:ts, :]
    nxt = next_ref[0, 0:LANES, :]
    i_m = lax.broadcasted_iota(I32, (ts, ts), 0)
    c_m = lax.broadcasted_iota(I32, (ts, ts), 1)
    band_m = jnp.where((c_m >= i_m - hw) & (c_m < i_m + hw), 1.0, 0.0).astype(BF16)
    i_h = lax.broadcasted_iota(I32, (ts, LANES), 0)
    c_h = lax.broadcasted_iota(I32, (ts, LANES), 1)
    band_p = jnp.where((c_h - LANES >= i_h - hw) & (t > 0), 1.0, 0.0).astype(BF16)
    band_n = jnp.where((c_h + ts < i_h + hw) & (t < nt - 1), 1.0, 0.0).astype(BF16)
    s = (jnp.dot(band_m, main, preferred_element_type=F32)
         + jnp.dot(band_p, prev, preferred_element_type=F32)
         + jnp.dot(band_n, nxt, preferred_element_type=F32))
    tabs = t * ts + lax.broadcasted_iota(I32, (ts, 1), 0)
    cnt = jnp.minimum(tabs + hw, seq) - jnp.maximum(tabs - hw, 0)
    pooled = s / cnt.astype(F32) - main.astype(F32)
    mixed = jnp.dot(pooled.astype(BF16), pw_ref[0], preferred_element_type=F32) * ps_ref[...]
    o_ref[0] = mixed.astype(o_ref.dtype)


def _pool(proj3, pool_w, pool_scale):
    bsz, seq, _ = proj3.shape
    ts = SEQ_TILE
    nt = seq // ts
    cw = pool_w.shape[-1]
    kern = functools.partial(_pool_kernel, ts=ts, seq=seq)
    return pl.pallas_call(
        kern,
        grid=(bsz, nt, POOL_GROUPS),
        in_specs=[pl.BlockSpec((1, ts, cw), lambda b, t, g: (b, jnp.maximum(t - 1, 0), g)),
                  pl.BlockSpec((1, ts, cw), lambda b, t, g: (b, t, g)),
                  pl.BlockSpec((1, ts, cw), lambda b, t, g: (b, jnp.minimum(t + 1, nt - 1), g)),
                  pl.BlockSpec((1, cw, cw), lambda b, t, g: (g, 0, 0)),
                  pl.BlockSpec((1, cw), lambda b, t, g: (0, g))],
        out_specs=pl.BlockSpec((1, ts, cw), lambda b, t, g: (b, t, g)),
        out_shape=jax.ShapeDtypeStruct((bsz, seq, POOL_GROUPS * cw), BF16),
        compiler_params=_cparams(("parallel", "parallel", "parallel")),
        name="pool",
    )(proj3, proj3, proj3, pool_w, pool_scale)


def _dot_split(a, b, a_is_value):
    val = a if a_is_value else b
    hi = val.astype(BF16)
    lo = (val - hi.astype(F32)).astype(BF16)
    if a_is_value:
        return (jnp.dot(hi, b, preferred_element_type=F32) + jnp.dot(lo, b, preferred_element_type=F32))
    return (jnp.dot(a, hi, preferred_element_type=F32) + jnp.dot(a, lo, preferred_element_type=F32))


def _mlstm_kernel(q_ref, kt_ref, v_ref, uo_ref, gr_ref, gc_ref, nw_ref, o_ref,
                  cf_ref, nf_ref, mf_ref, cb_ref, nb_ref, mb_ref, cbs_ref, nbs_ref, mbs_ref,
                  *, chunk, n_chunks):
    L = chunk
    p = pl.program_id(1)
    c = pl.program_id(2)
    row = lax.broadcasted_iota(I32, (L, L), 0)
    col = lax.broadcasted_iota(I32, (L, L), 1)
    tri_le = row <= col
    tri_ge = row >= col
    m_le = jnp.where(tri_le, 1.0, 0.0).astype(BF16)
    m_ge = jnp.where(tri_ge, 1.0, 0.0).astype(BF16)
    lane_r = lax.broadcasted_iota(I32, (1, L), 1)
    neg_inf = jnp.float32(-jnp.inf)

    gates_r = gr_ref[0, 0, 0]
    lf_r = jax.nn.log_sigmoid(gates_r)
    kt = kt_ref[0]
    v = v_ref[0]

    def update_state(c_ref, n_ref, m_ref, g_r, tot):
        m_prev = m_ref[...]
        m_new = jnp.maximum(tot + m_prev, jnp.max(g_r, axis=1, keepdims=True))
        decay = jnp.exp(tot + m_prev - m_new)
        w_r = jnp.exp(g_r - m_new)
        kw = kt.astype(F32) * w_r
        c_ref[...] = decay * c_ref[...] + jnp.dot(kw.astype(BF16), v, preferred_element_type=F32)
        n_ref[...] = decay * n_ref[...] + jnp.sum(kw, axis=1, keepdims=True)
        m_ref[...] = m_new

    @pl.when(p == 0)
    def _backward_states():
        @pl.when(c == 0)
        def _():
            cb_ref[...] = jnp.zeros_like(cb_ref)
            nb_ref[...] = jnp.zeros_like(nb_ref)
            mb_ref[...] = jnp.zeros_like(mb_ref)

        cc = n_chunks - 1 - c
        cbs_ref[cc] = cb_ref[...].astype(BF16)
        nbs_ref[cc] = nb_ref[...]
        mbs_ref[cc] = mb_ref[...]
        suff_r = _dot_split(lf_r, m_ge, True)
        a_r = suff_r[3:4]
        a0 = jnp.sum(jnp.where(lane_r == 0, a_r, 0.0), axis=1, keepdims=True)
        g_r = a0 - a_r + gates_r[2:3]
        update_state(cb_ref, nb_ref, mb_ref, g_r, a0)

    @pl.when(p == 1)
    def _outputs():
        @pl.when(c == 0)
        def _():
            cf_ref[...] = jnp.zeros_like(cf_ref)
            nf_ref[...] = jnp.zeros_like(nf_ref)
            mf_ref[...] = jnp.zeros_like(mf_ref)

        q = q_ref[0]
        lf_c = jax.nn.log_sigmoid(gc_ref[0, 0])
        incl_r = _dot_split(lf_r, m_le, True)
        suff_r = _dot_split(lf_r, m_ge, True)
        incl_c = _dot_split(m_ge, lf_c, False)
        suff_c = _dot_split(m_le, lf_c, False)
        b_r = incl_r[1:2]
        a_r = suff_r[3:4]
        b_c = incl_c[:, 1:2]
        a_c = suff_c[:, 3:4]
        li_f = gates_r[0:1]
        li_b = gates_r[2:3]

        s = jnp.dot(q, kt, preferred_element_type=F32)
        nlane = lax.broadcasted_iota(I32, (q.shape[1], LANES), 1)
        nmat = jnp.where(nlane == 0, nf_ref[...], jnp.where(nlane == 1, nbs_ref[c], 0.0)).astype(BF16)
        qn = jnp.dot(q, nmat, preferred_element_type=F32)
        mf_prev = mf_ref[...]
        mb_prev = mbs_ref[c]

        def direction(d, mask, cum_c, m_prev, qn_col):
            d = jnp.where(mask, d, neg_inf)
            m_inter = cum_c + m_prev
            m_t = jnp.maximum(m_inter, jnp.max(d, axis=1, keepdims=True))
            pmat = jnp.exp(d - m_t) * s
            w_inter = jnp.exp(m_inter - m_t)
            den = jnp.sum(pmat, axis=1, keepdims=True) + w_inter * qn_col
            r = 1.0 / jnp.maximum(jnp.abs(den), jnp.exp(-m_t))
            return pmat * r, w_inter * r

        pf, sf = direction(b_c - (b_r - li_f), tri_ge, b_c, mf_prev, qn[:, 0:1])
        pb, sb = direction(a_c - (a_r - li_b), tri_le, a_c, mb_prev, qn[:, 1:2])
        qf = q.astype(F32)
        h = (jnp.dot((pf + pb).astype(BF16), v, preferred_element_type=F32)
             + jnp.dot((qf * sf).astype(BF16), cf_ref[...].astype(BF16), preferred_element_type=F32)
             + jnp.dot((qf * sb).astype(BF16), cbs_ref[c], preferred_element_type=F32))

        mu = jnp.mean(h, axis=1, keepdims=True)
        hc = h - mu
        var = jnp.mean(hc * hc, axis=1, keepdims=True)
        hn = hc * lax.rsqrt(var + LN_EPS) * nw_ref[0]
        o_ref[0] = (_sigmoid(uo_ref[0].astype(F32)) * hn).astype(o_ref.dtype)

        b_last = jnp.sum(jnp.where(lane_r == L - 1, b_r, 0.0), axis=1, keepdims=True)
        update_state(cf_ref, nf_ref, mf_ref, b_last - b_r + li_f, b_last)


def _mlstm(q, kt, proj3, gates_r, gates_c, norm_w, v_col0, o_col0):
    bsz, seq, qw = q.shape
    dqk = qw // N_HEADS
    dv = norm_w.shape[-1]
    L = MLSTM_CHUNK
    nc = seq // L
    kern = functools.partial(_mlstm_kernel, chunk=L, n_chunks=nc)

    def chunk_of(p, c):
        return jnp.where(p == 0, nc - 1 - c, c)

    def out_chunk(p, c):
        return jnp.where(p == 0, 0, c)

    return pl.pallas_call(
        kern,
        grid=(bsz * N_HEADS, 2, nc),
        in_specs=[
            pl.BlockSpec((1, L, dqk), lambda bh, p, c: (bh // N_HEADS, out_chunk(p, c), bh % N_HEADS)),
            pl.BlockSpec((1, dqk, L), lambda bh, p, c: (bh // N_HEADS, bh % N_HEADS, chunk_of(p, c))),
            pl.BlockSpec((1, L, dv), lambda bh, p, c: (bh // N_HEADS, chunk_of(p, c), v_col0 + bh % N_HEADS)),
            pl.BlockSpec((1, L, dv), lambda bh, p, c: (bh // N_HEADS, out_chunk(p, c), o_col0 + bh % N_HEADS)),
            pl.BlockSpec((1, 1, 1, SUBLANES, L),
                         lambda bh, p, c: (bh // N_HEADS, bh % N_HEADS, chunk_of(p, c), 0, 0)),
            pl.BlockSpec((1, 1, L, SUBLANES),
                         lambda bh, p, c: (bh // N_HEADS, bh % N_HEADS, out_chunk(p, c), 0)),
            pl.BlockSpec((1, 1, dv), lambda bh, p, c: (bh % N_HEADS, 0, 0)),
        ],
        out_specs=pl.BlockSpec((1, L, dv), lambda bh, p, c: (bh // N_HEADS, out_chunk(p, c), bh % N_HEADS)),
        out_shape=jax.ShapeDtypeStruct((bsz, seq, N_HEADS * dv), BF16),
        scratch_shapes=[
            pltpu.VMEM((dqk, dv), F32), pltpu.VMEM((dqk, 1), F32), pltpu.VMEM((1, 1), F32),
            pltpu.VMEM((dqk, dv), F32), pltpu.VMEM((dqk, 1), F32), pltpu.VMEM((1, 1), F32),
            pltpu.VMEM((nc, dqk, dv), BF16), pltpu.VMEM((nc, dqk, 1), F32), pltpu.VMEM((nc, 1, 1), F32),
        ],
        compiler_params=_cparams(("parallel", "arbitrary", "arbitrary")),
        name="mlstm",
    )(q, kt, proj3, proj3, gates_r, gates_c, norm_w)


def _mixout_kernel(hg_ref, a_ref, uga_ref, ugb_ref, x_ref, wbd_ref, wo_ref, bga_ref, bgb_ref,
                   g_ref, b_ref, o_ref, op_ref, *, alpha):
    branch_b = jnp.dot(hg_ref[...], wbd_ref[...], preferred_element_type=F32)
    ga = _sigmoid(uga_ref[...].astype(F32) + bga_ref[...])
    gb = _sigmoid(ugb_ref[...].astype(F32) + bgb_ref[...])
    merged = ga * a_ref[...].astype(F32) + gb * branch_b
    mix = jnp.dot(merged.astype(BF16), wo_ref[...], preferred_element_type=F32)
    y = _layer_norm(alpha * x_ref[...] + mix, g_ref[...], b_ref[...])
    o_ref[...] = y
    op_ref[...] = _pack_bf16_pairs(y)


def _mixout(hg, branch_a, proj, x, w_b_down, w_out, b_gate, ln_g, ln_b, ga_col, alpha):
    t, d = x.shape
    tm = ROW_TILE
    inner = hg.shape[1]
    row = lambda i: (i, 0)
    const = lambda i: (0, 0)
    return pl.pallas_call(
        functools.partial(_mixout_kernel, alpha=alpha),
        grid=(t // tm,),
        in_specs=[pl.BlockSpec((tm, inner), row),
                  pl.BlockSpec((tm, d), row),
                  pl.BlockSpec((tm, d), lambda i: (i, ga_col)),
                  pl.BlockSpec((tm, d), lambda i: (i, ga_col + 1)),
                  pl.BlockSpec((tm, d), row),
                  pl.BlockSpec((inner, d), const),
                  pl.BlockSpec((d, d), const),
                  pl.BlockSpec((1, d), const),
                  pl.BlockSpec((1, d), lambda i: (0, 1)),
                  pl.BlockSpec((1, d), const),
                  pl.BlockSpec((1, d), const)],
        out_specs=[pl.BlockSpec((tm, d), row), pl.BlockSpec((tm, d // 2), row)],
        out_shape=[jax.ShapeDtypeStruct((t, d), F32), jax.ShapeDtypeStruct((t, d // 2), U32)],
        compiler_params=_cparams(("parallel",)),
        name="mixout",
    )(hg, branch_a, proj, proj, x, w_b_down, w_out, b_gate, b_gate, ln_g, ln_b)


def _router_kernel(x_ref, wh_ref, wl_ref, br_ref, eidx_ref, wts_ref, rank_ref, cnt_ref, run_ref,
                   *, n_experts):
    i = pl.program_id(0)
    tm = x_ref.shape[0]
    ne = n_experts
    per_group = ne // N_ROUTE_GROUPS

    @pl.when(i == 0)
    def _():
        run_ref[...] = jnp.zeros_like(run_ref)

    x = x_ref[...]
    xh = x.astype(BF16)
    xl = (x - xh.astype(F32)).astype(BF16)
    nt_dims = (((1,), (1,)), ((), ()))
    logits = (lax.dot_general(wh_ref[...], xh, nt_dims, preferred_element_type=F32)
              + lax.dot_general(wh_ref[...], xl, nt_dims, preferred_element_type=F32)
              + lax.dot_general(wl_ref[...], xh, nt_dims, preferred_element_type=F32))
    scores = _sigmoid(logits)
    sel = scores + br_ref[...]
    neg = jnp.float32(-jnp.inf)
    big = jnp.float32(1e9)
    rowi = lax.broadcasted_iota(I32, (ne, tm), 0).astype(F32)

    gi = lax.broadcasted_iota(I32, (N_ROUTE_GROUPS, tm), 0).astype(F32)
    work = jnp.zeros((N_ROUTE_GROUPS, tm), F32)
    for g in range(N_ROUTE_GROUPS):
        blk = sel[g * per_group:(g + 1) * per_group]
        ri = lax.broadcasted_iota(I32, (per_group, tm), 0).astype(F32) + float(g * per_group)
        m1 = jnp.max(blk, axis=0, keepdims=True)
        i1 = jnp.min(jnp.where(blk == m1, ri, big), axis=0, keepdims=True)
        m2 = jnp.max(jnp.where(ri == i1, neg, blk), axis=0, keepdims=True)
        work = jnp.where(gi == float(g), m1 + m2, work)
    row_group = jnp.floor(rowi * (1.0 / per_group))
    allowed = jnp.zeros((ne, tm), F32)
    for _ in range(TOPK_GROUPS):
        m = jnp.max(work, axis=0, keepdims=True)
        idx = jnp.min(jnp.where(work == m, gi, big), axis=0, keepdims=True)
        work = jnp.where(gi == idx, neg, work)
        allowed = jnp.where(row_group == idx, 1.0, allowed)
    selm = jnp.where(allowed > 0.5, sel, neg)

    member = jnp.zeros((ne, tm), F32)
    idxs, wks = [], []
    for _ in range(TOP_K):
        m = jnp.max(selm, axis=0, keepdims=True)
        idx = jnp.min(jnp.where(selm == m, rowi, big), axis=0, keepdims=True)
        hit = rowi == idx
        wks.append(jnp.sum(jnp.where(hit, scores, 0.0), axis=0, keepdims=True))
        idxs.append(idx)
        member = jnp.where(hit, 1.0, member)
        selm = jnp.where(hit, neg, selm)
    wsum = wks[0]
    for wk in wks[1:]:
        wsum = wsum + wk

    ti = lax.broadcasted_iota(I32, (tm, tm), 0)
    tj = lax.broadcasted_iota(I32, (tm, tm), 1)
    strict = jnp.where(ti < tj, 1.0, 0.0).astype(BF16)
    prefix = jnp.dot(member.astype(BF16), strict, preferred_element_type=F32) + run_ref[...]
    ranks = [jnp.sum(jnp.where(rowi == idx, prefix, 0.0), axis=0, keepdims=True) for idx in idxs]
    run_new = run_ref[...] + jnp.sum(member, axis=1, keepdims=True)
    run_ref[...] = run_new

    eidx_ref[...] = jnp.concatenate(idxs, axis=0).astype(I32)
    wts_ref[...] = jnp.concatenate([wk / wsum * ROUTED_SCALE for wk in wks], axis=0)
    rank_ref[...] = jnp.concatenate(ranks, axis=0).astype(I32)
    cnt_ref[...] = jnp.broadcast_to(run_new, cnt_ref.shape).astype(I32)


def _router(x1, wr_hi, wr_lo, b_router):
    t, d = x1.shape
    ne = wr_hi.shape[0]
    tm = ROUTER_TILE
    kern = functools.partial(_router_kernel, n_experts=ne)
    tok = lambda i: (0, i)
    const = lambda i: (0, 0)
    return pl.pallas_call(
        kern,
        grid=(t // tm,),
        in_specs=[pl.BlockSpec((tm, d), lambda i: (i, 0)),
                  pl.BlockSpec((ne, d), const),
                  pl.BlockSpec((ne, d), const),
                  pl.BlockSpec((ne, 1), const)],
        out_specs=[pl.BlockSpec((TOP_K, tm), tok), pl.BlockSpec((TOP_K, tm), tok),
                   pl.BlockSpec((TOP_K, tm), tok), pl.BlockSpec((ne, LANES), const)],
        out_shape=[jax.ShapeDtypeStruct((TOP_K, t), I32), jax.ShapeDtypeStruct((TOP_K, t), F32),
                   jax.ShapeDtypeStruct((TOP_K, t), I32), jax.ShapeDtypeStruct((ne, LANES), I32)],
        scratch_shapes=[pltpu.VMEM((ne, 1), F32)],
        compiler_params=_cparams(("arbitrary",)),
        name="router",
    )(x1, wr_hi, wr_lo, b_router)


def _row_copy(src_ref, src_row, dst_ref, dst_row, sem):
    return pltpu.make_async_copy(src_ref.at[pl.ds(src_row, 1)], dst_ref.at[pl.ds(dst_row, 1)], sem)


def _dispatch_kernel(pstart_ref, plen_ref, nu_ref, pos_ref, x_ref, xs_ref, zero_ref, sem_ref,
                     *, n_tiles, tile, n_experts, experts_per_step, tail_per_step):
    i = pl.program_id(0)
    tb = x_ref.shape[0]

    @pl.when(i == 0)
    def _():
        zero_ref[...] = jnp.zeros_like(zero_ref)

    def token_rows(act):
        def body(r, carry):
            for k in range(TOP_K):
                act(_row_copy(x_ref, r, xs_ref, pos_ref[0, k, r], sem_ref.at[0]))
            return carry
        lax.fori_loop(0, tb, body, 0)

    def zero_fill(act):
        for j in range(experts_per_step):
            e = i * experts_per_step + j

            @pl.when(e < n_experts)
            def _():
                start = pstart_ref[e]
                length = plen_ref[e]
                to_align = lax.rem(SUBLANES - lax.rem(start, SUBLANES), SUBLANES)
                n_single = jnp.minimum(to_align, length)

                def single(r, carry):
                    act(_row_copy(zero_ref, 0, xs_ref, start + r, sem_ref.at[1]))
                    return carry
                lax.fori_loop(0, n_single, single, 0)

                def chunk(r, carry):
                    dst = pl.multiple_of(start + to_align + r * SUBLANES, SUBLANES)
                    act(pltpu.make_async_copy(zero_ref.at[pl.ds(0, SUBLANES)],
                                              xs_ref.at[pl.ds(dst, SUBLANES)], sem_ref.at[1]))
                    return carry
                lax.fori_loop(0, (length - n_single) // SUBLANES, chunk, 0)

        for j in range(tail_per_step):
            tl = nu_ref[0] + i * tail_per_step + j

            @pl.when(tl < n_tiles)
            def _():
                dst = pl.multiple_of(tl * tile, tile)
                act(pltpu.make_async_copy(zero_ref, xs_ref.at[pl.ds(dst, tile)], sem_ref.at[1]))

    token_rows(lambda cp: cp.start())
    zero_fill(lambda cp: cp.start())
    token_rows(lambda cp: cp.wait())
    zero_fill(lambda cp: cp.wait())


def _dispatch(x1p, pos3, pad_start, pad_len, n_used, n_tiles, n_experts):
    t, w = x1p.shape
    tb = MOVE_TILE
    tile = EXPERT_TILE
    steps = t // tb
    min_used = (t * TOP_K) // tile
    kern = functools.partial(
        _dispatch_kernel, n_tiles=n_tiles, tile=tile, n_experts=n_experts,
        experts_per_step=pl.cdiv(n_experts, steps), tail_per_step=pl.cdiv(n_tiles - min_used, steps))
    grid_spec = pltpu.PrefetchScalarGridSpec(
        num_scalar_prefetch=3,
        grid=(steps,),
        in_specs=[pl.BlockSpec((1, TOP_K, tb), lambda i, *_: (i, 0, 0), memory_space=pltpu.SMEM),
                  pl.BlockSpec((tb, w), lambda i, *_: (i, 0))],
        out_specs=pl.BlockSpec(memory_space=pl.ANY),
        scratch_shapes=[pltpu.VMEM((tile, w), U32), pltpu.SemaphoreType.DMA((2,))],
    )
    return pl.pallas_call(
        kern,
        grid_spec=grid_spec,
        out_shape=jax.ShapeDtypeStruct((n_tiles * tile, w), U32),
        compiler_params=_cparams(("arbitrary",)),
        name="dispatch",
    )(pad_start, pad_len, n_used, pos3, x1p)


def _experts_kernel(te_ref, nu_ref, xs_ref, wg_ref, wu_ref, wd_ref, ys_ref, wgb_ref, wub_ref, wdb_ref):
    i = pl.program_id(0)

    @pl.when(i < nu_ref[0])
    def _():
        changed = jnp.logical_or(i == 0, te_ref[i] != te_ref[jnp.maximum(i - 1, 0)])

        @pl.when(changed)
        def _():
            wgb_ref[...] = wg_ref[0].astype(BF16)
            wub_ref[...] = wu_ref[0].astype(BF16)
            wdb_ref[...] = wd_ref[0].astype(BF16)

        x = _unpack_bf16_pairs(xs_ref[...]).astype(BF16)
        gate = jnp.dot(x, wgb_ref[...], preferred_element_type=F32)
        up = jnp.dot(x, wub_ref[...], preferred_element_type=F32)
        hid = (gate * _sigmoid(gate) * up).astype(BF16)
        y = jnp.dot(hid, wdb_ref[...], preferred_element_type=F32)
        ys_ref[...] = _pack_bf16_pairs(y)

    @pl.when(i >= nu_ref[0])
    def _():
        ys_ref[...] = jnp.zeros_like(ys_ref)


def _experts(tile_expert, n_used, xs, w_gate_e, w_up_e, w_down_e):
    n_slots, w = xs.shape
    tile = EXPERT_TILE
    n_tiles = n_slots // tile
    _, d, de = w_gate_e.shape
    used = lambda i, te, nu: (jnp.minimum(i, nu[0] - 1), 0)
    wsel = lambda i, te, nu: (te[i], 0, 0)
    grid_spec = pltpu.PrefetchScalarGridSpec(
        num_scalar_prefetch=2,
        grid=(n_tiles,),
        in_specs=[pl.BlockSpec((tile, w), used),
                  pl.BlockSpec((1, d, de), wsel),
                  pl.BlockSpec((1, d, de), wsel),
                  pl.BlockSpec((1, de, d), wsel)],
        out_specs=pl.BlockSpec((tile, w), lambda i, te, nu: (i, 0)),
        scratch_shapes=[pltpu.VMEM((d, de), BF16), pltpu.VMEM((d, de), BF16), pltpu.VMEM((de, d), BF16)],
    )
    return pl.pallas_call(
        _experts_kernel,
        grid_spec=grid_spec,
        out_shape=jax.ShapeDtypeStruct((n_slots, w), U32),
        compiler_params=_cparams(("arbitrary",)),
        name="experts",
    )(tile_expert, n_used, xs, w_gate_e, w_up_e, w_down_e)


def _combine_kernel(pos_ref, posn_ref, wts_ref, x_ref, xp_ref, ys_ref, wgs_ref, wus_ref, wds_ref,
                    g_ref, b_ref, o_ref, buf_ref, sem_ref, *, alpha):
    i = pl.program_id(0)
    n = pl.num_programs(0)
    tb = x_ref.shape[0]
    slot = lax.rem(i, 2)

    def issue(p_ref, s):
        def body(r, carry):
            for k in range(TOP_K):
                _row_copy(ys_ref, p_ref[0, k, r], buf_ref.at[s, k], r, sem_ref.at[s]).start()
            return carry
        lax.fori_loop(0, tb, body, 0)

    @pl.when(i == 0)
    def _():
        issue(pos_ref, 0)

    @pl.when(i + 1 < n)
    def _():
        issue(posn_ref, 1 - slot)

    def drain(r, carry):
        for k in range(TOP_K):
            _row_copy(ys_ref, pos_ref[0, k, r], buf_ref.at[slot, k], r, sem_ref.at[slot]).wait()
        return carry

    lax.fori_loop(0, tb, drain, 0)

    wts = wts_ref[...]
    routed = jnp.zeros(x_ref.shape, F32)
    for k in range(TOP_K):
        routed = routed + wts[:, k:k + 1] * _unpack_bf16_pairs(buf_ref[slot, k])
    xb = _unpack_bf16_pairs(xp_ref[...]).astype(BF16)
    gate = jnp.dot(xb, wgs_ref[...], preferred_element_type=F32)
    up = jnp.dot(xb, wus_ref[...], preferred_element_type=F32)
    hid = (gate * _sigmoid(gate) * up).astype(BF16)
    shared = jnp.dot(hid, wds_ref[...], preferred_element_type=F32)
    o_ref[...] = _layer_norm(alpha * x_ref[...] + (routed + shared), g_ref[...], b_ref[...])


def _combine(pos3, wts_c, x1, x1p, ys, w_gate_s, w_up_s, w_down_s, ln_g, ln_b, alpha):
    t, d = x1.shape
    w = x1p.shape[1]
    tb = MOVE_TILE
    nt = t // tb
    ds = w_gate_s.shape[1]
    row = lambda i: (i, 0)
    const = lambda i: (0, 0)
    return pl.pallas_call(
        functools.partial(_combine_kernel, alpha=alpha),
        grid=(nt,),
        in_specs=[pl.BlockSpec((1, TOP_K, tb), lambda i: (i, 0, 0), memory_space=pltpu.SMEM),
                  pl.BlockSpec((1, TOP_K, tb), lambda i: (jnp.minimum(i + 1, nt - 1), 0, 0),
                               memory_space=pltpu.SMEM),
                  pl.BlockSpec((tb, TOP_K), row),
                  pl.BlockSpec((tb, d), row),
                  pl.BlockSpec((tb, w), row),
                  pl.BlockSpec(memory_space=pl.ANY),
                  pl.BlockSpec((d, ds), const),
                  pl.BlockSpec((d, ds), const),
                  pl.BlockSpec((ds, d), const),
                  pl.BlockSpec((1, d), const),
                  pl.BlockSpec((1, d), const)],
        out_specs=pl.BlockSpec((tb, d), row),
        out_shape=jax.ShapeDtypeStruct((t, d), F32),
        scratch_shapes=[pltpu.VMEM((2, TOP_K, tb, w), U32), pltpu.SemaphoreType.DMA((2,))],
        compiler_params=_cparams(("arbitrary",)),
        name="combine",
    )(pos3, pos3, wts_c, x1, x1p, ys, w_gate_s, w_up_s, w_down_s, ln_g, ln_b)


def _layer(alpha, x, w_in, b_if, b_gate, conv_qk, pool_w, pool_scale, mh_norm_w, w_b_down, w_out,
           ln1_g, ln1_b, w_router, b_router, w_gate_e, w_up_e, w_down_e,
           w_gate_s, w_up_s, w_down_s, ln2_g, ln2_b):
    bsz, seq, d = x.shape
    t = bsz * seq
    heads = N_HEADS
    pool_width = pool_w.shape[0] * pool_w.shape[1]
    qk_cols = conv_qk.shape[1]
    v_cols = mh_norm_w.shape[0] * mh_norm_w.shape[1]
    o_cols = v_cols
    if_cols = b_if.shape[0]
    dv = mh_norm_w.shape[1]
    dqk = qk_cols // (2 * heads)
    off_if = pool_width + qk_cols + v_cols + o_cols
    off_gate = off_if + if_cols

    xf = x.reshape(t, d)
    xb = xf.astype(BF16)
    w_main = jnp.concatenate([w_in[:, :off_if], w_in[:, off_gate:]], axis=1).astype(BF16)
    w_if = jnp.pad(w_in[:, off_if:off_gate], ((0, 0), (0, LANES - if_cols))).astype(BF16)
    bias_if = jnp.pad(b_if, (0, LANES - if_cols)).reshape(1, LANES)
    n_main = w_main.shape[1]

    proj = _matmul_bias(xb, w_main, jnp.zeros((1, n_main), F32), BF16, 1024, 1024)
    u_if = _matmul_bias(xb, w_if, bias_if, F32, 1024, LANES)
    proj3 = proj.reshape(bsz, seq, n_main)

    branch_a = _pool(proj3, pool_w.astype(BF16), pool_scale.reshape(1, pool_width))

    cw = 2 * LANES
    q_col0 = pool_width // cw
    half = qk_cols // 2
    q = _qkconv(proj3, conv_qk[:, :half], q_col0, float(dqk) ** -0.5, False)
    kt = _qkconv(proj3, conv_qk[:, half:], q_col0 + half // cw, 1.0, True)

    nc = seq // MLSTM_CHUNK
    gates = u_if[:, :if_cols].reshape(bsz, seq, 4, heads).transpose(0, 3, 2, 1)
    gates = jnp.pad(gates, ((0, 0), (0, 0), (0, SUBLANES - 4), (0, 0)))
    gates_c = gates.transpose(0, 1, 3, 2)
    gates_r = gates.reshape(bsz, heads, SUBLANES, nc, MLSTM_CHUNK).transpose(0, 1, 3, 2, 4)
    v_col0 = (pool_width + qk_cols) // dv
    o_col0 = (pool_width + qk_cols + v_cols) // dv
    hg = _mlstm(q, kt, proj3, gates_r, gates_c, mh_norm_w.reshape(heads, 1, dv), v_col0, o_col0)

    ga_col = (pool_width + qk_cols + v_cols + o_cols) // d
    x1, x1p = _mixout(hg.reshape(t, heads * dv), branch_a.reshape(t, pool_width), proj, xf,
                      w_b_down.astype(BF16), w_out.astype(BF16), b_gate.reshape(1, 2 * d),
                      ln1_g.reshape(1, d), ln1_b.reshape(1, d), ga_col, alpha)

    ne = w_router.shape[1]
    wr_t = w_router.T
    wr_hi = wr_t.astype(BF16)
    wr_lo = (wr_t - wr_hi.astype(F32)).astype(BF16)
    eidx, wts, rank, cnt = _router(x1, wr_hi, wr_lo, b_router.reshape(ne, 1))

    tile = EXPERT_TILE
    n_tiles = (t * TOP_K) // tile + ne
    counts = cnt[:, 0]
    pcounts = ((counts + tile - 1) // tile) * tile
    pends = jnp.cumsum(pcounts)
    pstarts = pends - pcounts
    pos = pstarts[eidx] + rank
    n_used = (pends[-1] // tile).astype(I32)
    tile_ids = jnp.minimum(jnp.arange(n_tiles, dtype=I32), n_used - 1)
    tile_expert = jnp.clip(jnp.searchsorted(pends, tile_ids * tile, side='right'), 0, ne - 1).astype(I32)

    tb = MOVE_TILE
    pos3 = pos.reshape(TOP_K, t // tb, tb).transpose(1, 0, 2)
    n_used = n_used.reshape(1)
    xs = _dispatch(x1p, pos3, (pstarts + counts).astype(I32), (pcounts - counts).astype(I32), n_used, n_tiles, ne)
    ys = _experts(tile_expert, n_used, xs, w_gate_e, w_up_e, w_down_e)
    out = _combine(pos3, wts.T, x1, x1p, ys, w_gate_s.astype(BF16), w_up_s.astype(BF16),
                   w_down_s.astype(BF16), ln2_g.reshape(1, d), ln2_b.reshape(1, d), alpha)
    return out.reshape(bsz, seq, d)


def kernel(x, w_in, b_if, b_gate, conv_qk, pool_w, pool_scale, mh_norm_w, w_b_down, w_out, ln1_g, ln1_b,
           w_router, b_router, w_gate_e, w_up_e, w_down_e, w_gate_s, w_up_s, w_down_s, ln2_g, ln2_b):
    depth = w_in.shape[0]
    alpha = (2.0 * depth) ** 0.25
    for l in range(depth):
        x = _layer(alpha, x, w_in[l], b_if[l], b_gate[l], conv_qk[l], pool_w[l], pool_scale[l], mh_norm_w[l],
                   w_b_down[l], w_out[l], ln1_g[l], ln1_b[l], w_router[l], b_router[l], w_gate_e[l],
                   w_up_e[l], w_down_e[l], w_gate_s[l], w_up_s[l], w_down_s[l], ln2_g[l], ln2_b[l])
    return x
```

```python
import functools

import jax
import jax.numpy as jnp
import numpy as np
from jax import lax
from jax.experimental import pallas as pl
from jax.experimental.pallas import tpu as pltpu

F32 = jnp.float32
BF16 = jnp.bfloat16
I32 = jnp.int32
U32 = jnp.uint32

N_HEADS = 4
POOL_GROUPS = 4
CONV_WIDTH = 5
LN_EPS = 1e-5
N_ROUTE_GROUPS = 8
TOPK_GROUPS = 4
TOP_K = 8
ROUTED_SCALE = 2.5

LANES = 128
SUBLANES = 8
BF16_ROWS = 16
VMEM_LIMIT = 56 * 1024 * 1024

MLSTM_CHUNK = 256
MLSTM_HEADS_PER_STEP = 2
SEQ_TILE = 512
ROW_TILE = 512
ROUTER_TILE = 256
EXPERT_TILE = 256
MOVE_TILE = 128
SLOT_TILE = 512
DMA_PRIORITIES = 2

HI_MASK = 0xFFFF0000


def _cparams(sem):
    return pltpu.CompilerParams(dimension_semantics=sem, vmem_limit_bytes=VMEM_LIMIT)


def _sigmoid(x):
    return 1.0 / (1.0 + jnp.exp(-x))


def _layer_norm(y, g, b):
    mu = jnp.mean(y, axis=-1, keepdims=True)
    yc = y - mu
    var = jnp.mean(yc * yc, axis=-1, keepdims=True)
    return yc * lax.rsqrt(var + LN_EPS) * g + b


def _pack_bf16_pairs(y):
    c = y.shape[1] // 2
    bits = lax.bitcast_convert_type(y.astype(BF16).astype(F32), U32)
    return (bits[:, :c] >> 16) | (bits[:, c:] & jnp.uint32(HI_MASK))


def _unpack_bf16_pairs(p):
    lo = lax.bitcast_convert_type(p << 16, F32)
    hi = lax.bitcast_convert_type(p & jnp.uint32(HI_MASK), F32)
    return jnp.concatenate([lo, hi], axis=1)


def _matmul_kernel(x_ref, w_ref, b_ref, o_ref):
    acc = jnp.dot(x_ref[...], w_ref[...], preferred_element_type=F32)
    o_ref[...] = (acc + b_ref[...]).astype(o_ref.dtype)


def _matmul_bias(x, w, bias, out_dtype, tm, tn):
    m, k = x.shape
    n = w.shape[1]
    return pl.pallas_call(
        _matmul_kernel,
        grid=(m // tm, n // tn),
        in_specs=[pl.BlockSpec((tm, k), lambda i, j: (i, 0)),
                  pl.BlockSpec((k, tn), lambda i, j: (0, j)),
                  pl.BlockSpec((1, tn), lambda i, j: (0, j))],
        out_specs=pl.BlockSpec((tm, tn), lambda i, j: (i, j)),
        out_shape=jax.ShapeDtypeStruct((m, n), out_dtype),
        compiler_params=_cparams(("parallel", "parallel")),
        name="inproj",
    )(x, w, bias)


def _qkconv_kernel(prev_ref, main_ref, next_ref, w_ref, o_ref, *, ts, scale, transpose):
    t = pl.program_id(1)
    nt = pl.num_programs(1)
    main = main_ref[0].astype(F32)
    prev = prev_ref[0].astype(F32)[BF16_ROWS - SUBLANES:]
    nxt = next_ref[0].astype(F32)[:SUBLANES]
    prev = jnp.where(t > 0, prev, 0.0)
    nxt = jnp.where(t < nt - 1, nxt, 0.0)
    ext = jnp.concatenate([prev, main, nxt], axis=0)
    w = w_ref[...]
    pad = CONV_WIDTH // 2
    acc = jnp.zeros_like(main)
    for j in range(CONV_WIDTH):
        off = SUBLANES - pad + j
        acc = acc + ext[off:off + ts] * w[j:j + 1]
    y = acc * _sigmoid(acc) * scale
    if transpose:
        o_ref[0] = y.T.astype(o_ref.dtype)
    else:
        o_ref[0] = y.astype(o_ref.dtype)


def _qkconv(proj3, conv_w, col0, scale, transpose):
    bsz, seq, _ = proj3.shape
    ts = SEQ_TILE
    nt = seq // ts
    cw = 2 * LANES
    ncol = conv_w.shape[1] // cw
    hb = ts // BF16_ROWS
    n_hb = seq // BF16_ROWS
    kern = functools.partial(_qkconv_kernel, ts=ts, scale=scale, transpose=transpose)
    if transpose:
        out_shape = jax.ShapeDtypeStruct((bsz, ncol * cw, seq), BF16)
        out_spec = pl.BlockSpec((1, cw, ts), lambda b, t, j: (b, j, t))
    else:
        out_shape = jax.ShapeDtypeStruct((bsz, seq, ncol * cw), BF16)
        out_spec = pl.BlockSpec((1, ts, cw), lambda b, t, j: (b, t, j))
    return pl.pallas_call(
        kern,
        grid=(bsz, nt, ncol),
        in_specs=[pl.BlockSpec((1, BF16_ROWS, cw), lambda b, t, j: (b, jnp.maximum(t * hb - 1, 0), col0 + j)),
                  pl.BlockSpec((1, ts, cw), lambda b, t, j: (b, t, col0 + j)),
                  pl.BlockSpec((1, BF16_ROWS, cw),
                               lambda b, t, j: (b, jnp.minimum((t + 1) * hb, n_hb - 1), col0 + j)),
                  pl.BlockSpec((CONV_WIDTH, cw), lambda b, t, j: (0, j))],
        out_specs=out_spec,
        out_shape=out_shape,
        compiler_params=_cparams(("parallel", "parallel", "parallel")),
        name="qkconv_t" if transpose else "qkconv",
    )(proj3, proj3, proj3, conv_w)


def _pool_kernel(prev_ref, main_ref, next_ref, bm_ref, bp_ref, bn_ref, pw_ref, ps_ref, o_ref, *, ts, seq):
    g = pl.program_id(1)
    t = pl.program_id(2)
    nt = pl.num_programs(2)
    hw = jnp.left_shift(1, g)
    main = main_ref[0]
    prev = jnp.where(t > 0, prev_ref[0], jnp.zeros_like(prev_ref[0]))
    nxt = jnp.where(t < nt - 1, next_ref[0], jnp.zeros_like(next_ref[0]))
    s = (jnp.dot(bm_ref[0], main, preferred_element_type=F32)
         + jnp.dot(bp_ref[0], prev, preferred_element_type=F32)
         + jnp.dot(bn_ref[0], nxt, preferred_element_type=F32))
    tabs = t * ts + lax.broadcasted_iota(I32, (ts, 1), 0)
    cnt = jnp.minimum(tabs + hw, seq) - jnp.maximum(tabs - hw, 0)
    pooled = s / cnt.astype(F32) - main.astype(F32)
    mixed = jnp.dot(pooled.astype(BF16), pw_ref[0], preferred_element_type=F32) * ps_ref[...]
    o_ref[0] = mixed.astype(o_ref.dtype)


def _pool(proj3, pool_w, pool_scale):
    bsz, seq, _ = proj3.shape
    ts = SEQ_TILE
    nt = seq // ts
    cw = pool_w.shape[-1]
    hb = ts // LANES
    n_hb = seq // LANES
    i = np.arange(ts)[:, None]
    hws = [1 << g for g in range(POOL_GROUPS)]
    band_m = np.stack([(np.arange(ts)[None, :] >= i - hw) & (np.arange(ts)[None, :] < i + hw) for hw in hws])
    band_p = np.stack([(np.arange(LANES)[None, :] - LANES >= i - hw) for hw in hws])
    band_n = np.stack([(np.arange(LANES)[None, :] + ts < i + hw) for hw in hws])
    band_m, band_p, band_n = (jnp.asarray(b.astype(np.float32), BF16) for b in (band_m, band_p, band_n))
    kern = functools.partial(_pool_kernel, ts=ts, seq=seq)
    per_group = lambda b, g, t: (g, 0, 0)
    return pl.pallas_call(
        kern,
        grid=(bsz, POOL_GROUPS, nt),
        in_specs=[pl.BlockSpec((1, LANES, cw), lambda b, g, t: (b, jnp.maximum(t * hb - 1, 0), g)),
                  pl.BlockSpec((1, ts, cw), lambda b, g, t: (b, t, g)),
                  pl.BlockSpec((1, LANES, cw), lambda b, g, t: (b, jnp.minimum((t + 1) * hb, n_hb - 1), g)),
                  pl.BlockSpec((1, ts, ts), per_group),
                  pl.BlockSpec((1, ts, LANES), per_group),
                  pl.BlockSpec((1, ts, LANES), per_group),
                  pl.BlockSpec((1, cw, cw), per_group),
                  pl.BlockSpec((1, cw), lambda b, g, t: (0, g))],
        out_specs=pl.BlockSpec((1, ts, cw), lambda b, g, t: (b, t, g)),
        out_shape=jax.ShapeDtypeStruct((bsz, seq, POOL_GROUPS * cw), BF16),
        compiler_params=_cparams(("parallel", "parallel", "parallel")),
        name="pool",
    )(proj3, proj3, proj3, band_m, band_p, band_n, pool_w, pool_scale)


def _dot_split(a, b, a_is_value):
    val = a if a_is_value else b
    hi = val.astype(BF16)
    lo = (val - hi.astype(F32)).astype(BF16)
    if a_is_value:
        return (jnp.dot(hi, b, preferred_element_type=F32) + jnp.dot(lo, b, preferred_element_type=F32))
    return (jnp.dot(a, hi, preferred_element_type=F32) + jnp.dot(a, lo, preferred_element_type=F32))


def _mlstm_kernel(q_ref, kt_ref, v_ref, uo_ref, gr_ref, gc_ref, nw_ref, o_ref,
                  cf_ref, nf_ref, mf_ref, cb_ref, nb_ref, mb_ref, cbs_ref, nbs_ref, mbs_ref,
                  *, chunk, n_chunks, heads):
    L = chunk
    dqk = kt_ref.shape[1] // heads
    dv = v_ref.shape[2] // heads
    p = pl.program_id(1)
    c = pl.program_id(2)
    row = lax.broadcasted_iota(I32, (L, L), 0)
    col = lax.broadcasted_iota(I32, (L, L), 1)
    tri_le = row <= col
    tri_ge = row >= col
    m_le = jnp.where(tri_le, 1.0, 0.0).astype(BF16)
    m_ge = jnp.where(tri_ge, 1.0, 0.0).astype(BF16)
    lane_r = lax.broadcasted_iota(I32, (1, L), 1)
    neg_inf = jnp.float32(-jnp.inf)

    hs = range(heads)
    kts = [kt_ref[0, hh * dqk:(hh + 1) * dqk, :] for hh in hs]
    vs = [v_ref[0, :, hh * dv:(hh + 1) * dv] for hh in hs]
    gates = [gr_ref[0, hh, 0] for hh in hs]
    lf_r = [jax.nn.log_sigmoid(g) for g in gates]

    def update_state(c_ref, n_ref, m_ref, g_r, tot):
        m_prev = [m_ref[hh] for hh in hs]
        m_new = [jnp.maximum(tot[hh] + m_prev[hh], jnp.max(g_r[hh], axis=1, keepdims=True)) for hh in hs]
        decay = [jnp.exp(tot[hh] + m_prev[hh] - m_new[hh]) for hh in hs]
        kw = [kts[hh].astype(F32) * jnp.exp(g_r[hh] - m_new[hh]) for hh in hs]
        upd = [jnp.dot(kw[hh].astype(BF16), vs[hh], preferred_element_type=F32) for hh in hs]
        for hh in hs:
            c_ref[hh] = decay[hh] * c_ref[hh] + upd[hh]
            n_ref[hh] = decay[hh] * n_ref[hh] + jnp.sum(kw[hh], axis=1, keepdims=True)
            m_ref[hh] = m_new[hh]

    @pl.when(p == 0)
    def _backward_states():
        @pl.when(c == 0)
        def _():
            cb_ref[...] = jnp.zeros_like(cb_ref)
            nb_ref[...] = jnp.zeros_like(nb_ref)
            mb_ref[...] = jnp.zeros_like(mb_ref)

        cc = n_chunks - 1 - c
        cbs_ref[cc] = cb_ref[...].astype(BF16)
        nbs_ref[cc] = nb_ref[...]
        mbs_ref[cc] = mb_ref[...]
        a_r = [_dot_split(lf_r[hh], m_ge, True)[3:4] for hh in hs]
        a0 = [jnp.sum(jnp.where(lane_r == 0, a_r[hh], 0.0), axis=1, keepdims=True) for hh in hs]
        g_r = [a0[hh] - a_r[hh] + gates[hh][2:3] for hh in hs]
        update_state(cb_ref, nb_ref, mb_ref, g_r, a0)

    @pl.when(p == 1)
    def _outputs():
        @pl.when(c == 0)
        def _():
            cf_ref[...] = jnp.zeros_like(cf_ref)
            nf_ref[...] = jnp.zeros_like(nf_ref)
            mf_ref[...] = jnp.zeros_like(mf_ref)

        nb_in = nbs_ref[c]
        mb_in = mbs_ref[c]
        cb_in = cbs_ref[c]
        qs = [q_ref[0, :, hh * dqk:(hh + 1) * dqk] for hh in hs]
        lf_c = [jax.nn.log_sigmoid(gc_ref[0, hh]) for hh in hs]
        b_r = [_dot_split(lf_r[hh], m_le, True)[1:2] for hh in hs]
        a_r = [_dot_split(lf_r[hh], m_ge, True)[3:4] for hh in hs]
        b_c = [_dot_split(m_ge, lf_c[hh], False)[:, 1:2] for hh in hs]
        a_c = [_dot_split(m_le, lf_c[hh], False)[:, 3:4] for hh in hs]
        li_f = [gates[hh][0:1] for hh in hs]
        li_b = [gates[hh][2:3] for hh in hs]

        s = [jnp.dot(qs[hh], kts[hh], preferred_element_type=F32) for hh in hs]
        nlane = lax.broadcasted_iota(I32, (dqk, LANES), 1)
        nmat = [jnp.where(nlane == 0, nf_ref[hh], jnp.where(nlane == 1, nb_in[hh], 0.0)).astype(BF16)
                for hh in hs]
        qn = [jnp.dot(qs[hh], nmat[hh], preferred_element_type=F32) for hh in hs]

        def direction(d, mask, cum_c, m_prev, qn_col):
            d = [jnp.where(mask, d[hh], neg_inf) for hh in hs]
            m_inter = [cum_c[hh] + m_prev[hh] for hh in hs]
            m_t = [jnp.maximum(m_inter[hh], jnp.max(d[hh], axis=1, keepdims=True)) for hh in hs]
            pmat = [jnp.exp(d[hh] - m_t[hh]) * s[hh] for hh in hs]
            w_inter = [jnp.exp(m_inter[hh] - m_t[hh]) for hh in hs]
            den = [jnp.sum(pmat[hh], axis=1, keepdims=True) + w_inter[hh] * qn_col[hh] for hh in hs]
            r = [1.0 / jnp.maximum(jnp.abs(den[hh]), jnp.exp(-m_t[hh])) for hh in hs]
            return [pmat[hh] * r[hh] for hh in hs], [w_inter[hh] * r[hh] for hh in hs]

        pf, sf = direction([b_c[hh] - (b_r[hh] - li_f[hh]) for hh in hs], tri_ge, b_c,
                           [mf_ref[hh] for hh in hs], [qn[hh][:, 0:1] for hh in hs])
        pb, sb = direction([a_c[hh] - (a_r[hh] - li_b[hh]) for hh in hs], tri_le, a_c,
                           [mb_in[hh] for hh in hs], [qn[hh][:, 1:2] for hh in hs])
        qf = [qs[hh].astype(F32) for hh in hs]
        h = [jnp.dot((pf[hh] + pb[hh]).astype(BF16), vs[hh], preferred_element_type=F32)
             + jnp.dot((qf[hh] * sf[hh]).astype(BF16), cf_ref[hh].astype(BF16), preferred_element_type=F32)
             + jnp.dot((qf[hh] * sb[hh]).astype(BF16), cb_in[hh], preferred_element_type=F32) for hh in hs]

        mu = [jnp.mean(h[hh], axis=1, keepdims=True) for hh in hs]
        hc = [h[hh] - mu[hh] for hh in hs]
        var = [jnp.mean(hc[hh] * hc[hh], axis=1, keepdims=True) for hh in hs]
        hn = [hc[hh] * lax.rsqrt(var[hh] + LN_EPS) * nw_ref[hh] for hh in hs]
        for hh in hs:
            gate_o = _sigmoid(uo_ref[0, :, hh * dv:(hh + 1) * dv].astype(F32))
            o_ref[0, :, hh * dv:(hh + 1) * dv] = (gate_o * hn[hh]).astype(o_ref.dtype)

        b_last = [jnp.sum(jnp.where(lane_r == L - 1, b_r[hh], 0.0), axis=1, keepdims=True) for hh in hs]
        update_state(cf_ref, nf_ref, mf_ref, [b_last[hh] - b_r[hh] + li_f[hh] for hh in hs], b_last)


def _mlstm(q, kt, proj3, gates_r, gates_c, norm_w, v_off, o_off):
    bsz, seq, qw = q.shape
    dqk = qw // N_HEADS
    dv = norm_w.shape[-1]
    L = MLSTM_CHUNK
    nc = seq // L
    hg = MLSTM_HEADS_PER_STEP
    groups = N_HEADS // hg
    assert N_HEADS % hg == 0 and v_off % (hg * dv) == 0 and o_off % (hg * dv) == 0
    v_blk0 = v_off // (hg * dv)
    o_blk0 = o_off // (hg * dv)
    kern = functools.partial(_mlstm_kernel, chunk=L, n_chunks=nc, heads=hg)

    def chunk_of(p, c):
        return jnp.where(p == 0, nc - 1 - c, c)

    def out_chunk(p, c):
        return jnp.where(p == 0, 0, c)

    return pl.pallas_call(
        kern,
        grid=(bsz * groups, 2, nc),
        in_specs=[
            pl.BlockSpec((1, L, hg * dqk), lambda bg, p, c: (bg // groups, out_chunk(p, c), bg % groups)),
            pl.BlockSpec((1, hg * dqk, L), lambda bg, p, c: (bg // groups, bg % groups, chunk_of(p, c))),
            pl.BlockSpec((1, L, hg * dv), lambda bg, p, c: (bg // groups, chunk_of(p, c), v_blk0 + bg % groups)),
            pl.BlockSpec((1, L, hg * dv), lambda bg, p, c: (bg // groups, out_chunk(p, c), o_blk0 + bg % groups)),
            pl.BlockSpec((1, hg, 1, SUBLANES, L),
                         lambda bg, p, c: (bg // groups, bg % groups, chunk_of(p, c), 0, 0)),
            pl.BlockSpec((1, hg, L, SUBLANES),
                         lambda bg, p, c: (bg // groups, bg % groups, out_chunk(p, c), 0)),
            pl.BlockSpec((hg, 1, dv), lambda bg, p, c: (bg % groups, 0, 0)),
        ],
        out_specs=pl.BlockSpec((1, L, hg * dv), lambda bg, p, c: (bg // groups, out_chunk(p, c), bg % groups)),
        out_shape=jax.ShapeDtypeStruct((bsz, seq, N_HEADS * dv), BF16),
        scratch_shapes=[
            pltpu.VMEM((hg, dqk, dv), F32), pltpu.VMEM((hg, dqk, 1), F32), pltpu.VMEM((hg, 1, 1), F32),
            pltpu.VMEM((hg, dqk, dv), F32), pltpu.VMEM((hg, dqk, 1), F32), pltpu.VMEM((hg, 1, 1), F32),
            pltpu.VMEM((nc, hg, dqk, dv), BF16), pltpu.VMEM((nc, hg, dqk, 1), F32),
            pltpu.VMEM((nc, hg, 1, 1), F32),
        ],
        compiler_params=_cparams(("parallel", "arbitrary", "arbitrary")),
        name="mlstm",
    )(q, kt, proj3, proj3, gates_r, gates_c, norm_w)


def _mixout_kernel(hg_ref, a_ref, uga_ref, ugb_ref, x_ref, wbd_ref, wo_ref, bga_ref, bgb_ref,
                   g_ref, b_ref, o_ref, op_ref, *, alpha):
    branch_b = jnp.dot(hg_ref[...], wbd_ref[...], preferred_element_type=F32)
    ga = _sigmoid(uga_ref[...].astype(F32) + bga_ref[...])
    gb = _sigmoid(ugb_ref[...].astype(F32) + bgb_ref[...])
    merged = ga * a_ref[...].astype(F32) + gb * branch_b
    mix = jnp.dot(merged.astype(BF16), wo_ref[...], preferred_element_type=F32)
    y = _layer_norm(alpha * x_ref[...] + mix, g_ref[...], b_ref[...])
    o_ref[...] = y
    op_ref[...] = _pack_bf16_pairs(y)


def _mixout(hg, branch_a, proj, x, w_b_down, w_out, b_gate, ln_g, ln_b, ga_col, alpha):
    t, d = x.shape
    tm = ROW_TILE
    inner = hg.shape[1]
    row = lambda i: (i, 0)
    const = lambda i: (0, 0)
    return pl.pallas_call(
        functools.partial(_mixout_kernel, alpha=alpha),
        grid=(t // tm,),
        in_specs=[pl.BlockSpec((tm, inner), row),
                  pl.BlockSpec((tm, d), row),
                  pl.BlockSpec((tm, d), lambda i: (i, ga_col)),
                  pl.BlockSpec((tm, d), lambda i: (i, ga_col + 1)),
                  pl.BlockSpec((tm, d), row),
                  pl.BlockSpec((inner, d), const),
                  pl.BlockSpec((d, d), const),
                  pl.BlockSpec((1, d), const),
                  pl.BlockSpec((1, d), lambda i: (0, 1)),
                  pl.BlockSpec((1, d), const),
                  pl.BlockSpec((1, d), const)],
        out_specs=[pl.BlockSpec((tm, d), row), pl.BlockSpec((tm, d // 2), row)],
        out_shape=[jax.ShapeDtypeStruct((t, d), F32), jax.ShapeDtypeStruct((t, d // 2), U32)],
        compiler_params=_cparams(("parallel",)),
        name="mixout",
    )(hg, branch_a, proj, proj, x, w_b_down, w_out, b_gate, b_gate, ln_g, ln_b)


def _router_kernel(x_ref, wh_ref, wl_ref, br_ref, eidx_ref, wts_ref, rank_ref, cnt_ref, run_ref,
                   *, n_experts):
    i = pl.program_id(0)
    tm = x_ref.shape[0]
    ne = n_experts
    per_group = ne // N_ROUTE_GROUPS

    @pl.when(i == 0)
    def _():
        run_ref[...] = jnp.zeros_like(run_ref)

    x = x_ref[...]
    xh = x.astype(BF16)
    xl = (x - xh.astype(F32)).astype(BF16)
    nt_dims = (((1,), (1,)), ((), ()))
    logits = (lax.dot_general(wh_ref[...], xh, nt_dims, preferred_element_type=F32)
              + lax.dot_general(wh_ref[...], xl, nt_dims, preferred_element_type=F32)
              + lax.dot_general(wl_ref[...], xh, nt_dims, preferred_element_type=F32))
    scores = _sigmoid(logits)
    sel = scores + br_ref[...]
    neg = jnp.float32(-jnp.inf)
    big = jnp.float32(1e9)
    rowi = lax.broadcasted_iota(I32, (ne, tm), 0).astype(F32)

    gi = lax.broadcasted_iota(I32, (N_ROUTE_GROUPS, tm), 0).astype(F32)
    work = jnp.zeros((N_ROUTE_GROUPS, tm), F32)
    for g in range(N_ROUTE_GROUPS):
        blk = sel[g * per_group:(g + 1) * per_group]
        ri = lax.broadcasted_iota(I32, (per_group, tm), 0).astype(F32) + float(g * per_group)
        m1 = jnp.max(blk, axis=0, keepdims=True)
        i1 = jnp.min(jnp.where(blk == m1, ri, big), axis=0, keepdims=True)
        m2 = jnp.max(jnp.where(ri == i1, neg, blk), axis=0, keepdims=True)
        work = jnp.where(gi == float(g), m1 + m2, work)
    row_group = jnp.floor(rowi * (1.0 / per_group))
    allowed = jnp.zeros((ne, tm), F32)
    for _ in range(TOPK_GROUPS):
        m = jnp.max(work, axis=0, keepdims=True)
        idx = jnp.min(jnp.where(work == m, gi, big), axis=0, keepdims=True)
        work = jnp.where(gi == idx, neg, work)
        allowed = jnp.where(row_group == idx, 1.0, allowed)
    selm = jnp.where(allowed > 0.5, sel, neg)

    member = jnp.zeros((ne, tm), F32)
    idxs, wks = [], []
    for _ in range(TOP_K):
        m = jnp.max(selm, axis=0, keepdims=True)
        idx = jnp.min(jnp.where(selm == m, rowi, big), axis=0, keepdims=True)
        hit = rowi == idx
        wks.append(jnp.sum(jnp.where(hit, scores, 0.0), axis=0, keepdims=True))
        idxs.append(idx)
        member = jnp.where(hit, 1.0, member)
        selm = jnp.where(hit, neg, selm)
    wsum = wks[0]
    for wk in wks[1:]:
        wsum = wsum + wk

    ti = lax.broadcasted_iota(I32, (tm, tm), 0)
    tj = lax.broadcasted_iota(I32, (tm, tm), 1)
    strict = jnp.where(ti < tj, 1.0, 0.0).astype(BF16)
    prefix = jnp.dot(member.astype(BF16), strict, preferred_element_type=F32) + run_ref[...]
    ranks = [jnp.sum(jnp.where(rowi == idx, prefix, 0.0), axis=0, keepdims=True) for idx in idxs]
    run_new = run_ref[...] + jnp.sum(member, axis=1, keepdims=True)
    run_ref[...] = run_new

    eidx_ref[...] = jnp.concatenate(idxs, axis=0).astype(I32)
    wts_ref[...] = jnp.concatenate([wk / wsum * ROUTED_SCALE for wk in wks], axis=0)
    rank_ref[...] = jnp.concatenate(ranks, axis=0).astype(I32)
    cnt_ref[...] = jnp.broadcast_to(run_new, cnt_ref.shape).astype(I32)


def _router(x1, wr_hi, wr_lo, b_router):
    t, d = x1.shape
    ne = wr_hi.shape[0]
    tm = ROUTER_TILE
    kern = functools.partial(_router_kernel, n_experts=ne)
    tok = lambda i: (0, i)
    const = lambda i: (0, 0)
    return pl.pallas_call(
        kern,
        grid=(t // tm,),
        in_specs=[pl.BlockSpec((tm, d), lambda i: (i, 0)),
                  pl.BlockSpec((ne, d), const),
                  pl.BlockSpec((ne, d), const),
                  pl.BlockSpec((ne, 1), const)],
        out_specs=[pl.BlockSpec((TOP_K, tm), tok), pl.BlockSpec((TOP_K, tm), tok),
                   pl.BlockSpec((TOP_K, tm), tok), pl.BlockSpec((ne, LANES), const)],
        out_shape=[jax.ShapeDtypeStruct((TOP_K, t), I32), jax.ShapeDtypeStruct((TOP_K, t), F32),
                   jax.ShapeDtypeStruct((TOP_K, t), I32), jax.ShapeDtypeStruct((ne, LANES), I32)],
        scratch_shapes=[pltpu.VMEM((ne, 1), F32)],
        compiler_params=_cparams(("arbitrary",)),
        name="router",
    )(x1, wr_hi, wr_lo, b_router)


def _slots_kernel(eidx_ref, rank_ref, ps_ref, pos_ref, *, n_experts):
    tm = eidx_ref.shape[1]
    rowi = lax.broadcasted_iota(I32, (n_experts, tm), 0)
    eidx = eidx_ref[...]
    starts = ps_ref[...]
    base = [jnp.sum(jnp.where(rowi == eidx[k:k + 1], starts, 0.0), axis=0, keepdims=True)
            for k in range(TOP_K)]
    pos_ref[...] = jnp.concatenate(base, axis=0).astype(I32) + rank_ref[...]


def _slots(eidx, rank, pstarts):
    k, t = eidx.shape
    ne = pstarts.shape[0]
    tm = SLOT_TILE
    tok = lambda i: (0, i)
    return pl.pallas_call(
        functools.partial(_slots_kernel, n_experts=ne),
        grid=(t // tm,),
        in_specs=[pl.BlockSpec((k, tm), tok), pl.BlockSpec((k, tm), tok),
                  pl.BlockSpec((ne, 1), lambda i: (0, 0))],
        out_specs=pl.BlockSpec((k, tm), tok),
        out_shape=jax.ShapeDtypeStruct((k, t), I32),
        compiler_params=_cparams(("parallel",)),
        name="slots",
    )(eidx, rank, pstarts.astype(F32).reshape(ne, 1))


def _row_copy(src_ref, src_row, dst_ref, dst_row, sem):
    return pltpu.make_async_copy(src_ref.at[pl.ds(src_row, 1)], dst_ref.at[pl.ds(dst_row, 1)], sem)


def _dispatch_kernel(pstart_ref, plen_ref, nu_ref, pos_ref, x_ref, xs_ref, zero_ref, sem_ref,
                     *, n_tiles, tile, n_experts, experts_per_step, tail_per_step):
    i = pl.program_id(0)
    tb = x_ref.shape[0]

    @pl.when(i == 0)
    def _():
        zero_ref[...] = jnp.zeros_like(zero_ref)

    def token_rows(act):
        def body(r, carry):
            for k in range(TOP_K):
                act(_row_copy(x_ref, r, xs_ref, pos_ref[0, k, r], sem_ref.at[0]), k % DMA_PRIORITIES)
            return carry
        lax.fori_loop(0, tb, body, 0)

    def zero_fill(act):
        for j in range(experts_per_step):
            e = i * experts_per_step + j

            @pl.when(e < n_experts)
            def _():
                start = pstart_ref[e]
                length = plen_ref[e]
                to_align = lax.rem(SUBLANES - lax.rem(start, SUBLANES), SUBLANES)
                n_single = jnp.minimum(to_align, length)

                def single(r, carry):
                    act(_row_copy(zero_ref, 0, xs_ref, start + r, sem_ref.at[1]), 0)
                    return carry
                lax.fori_loop(0, n_single, single, 0)

                def chunk(r, carry):
                    dst = pl.multiple_of(start + to_align + r * SUBLANES, SUBLANES)
                    act(pltpu.make_async_copy(zero_ref.at[pl.ds(0, SUBLANES)],
                                              xs_ref.at[pl.ds(dst, SUBLANES)], sem_ref.at[1]), 0)
                    return carry
                lax.fori_loop(0, (length - n_single) // SUBLANES, chunk, 0)

        for j in range(tail_per_step):
            tl = nu_ref[0] + i * tail_per_step + j

            @pl.when(tl < n_tiles)
            def _():
                dst = pl.multiple_of(tl * tile, tile)
                act(pltpu.make_async_copy(zero_ref, xs_ref.at[pl.ds(dst, tile)], sem_ref.at[1]), 0)

    token_rows(lambda cp, prio: cp.start(priority=prio))
    zero_fill(lambda cp, prio: cp.start(priority=prio))
    token_rows(lambda cp, prio: cp.wait())
    zero_fill(lambda cp, prio: cp.wait())


def _dispatch(x1p, pos3, pad_start, pad_len, n_used, n_tiles, n_experts):
    t, w = x1p.shape
    tb = MOVE_TILE
    tile = EXPERT_TILE
    steps = t // tb
    min_used = (t * TOP_K) // tile
    kern = functools.partial(
        _dispatch_kernel, n_tiles=n_tiles, tile=tile, n_experts=n_experts,
        experts_per_step=pl.cdiv(n_experts, steps), tail_per_step=pl.cdiv(n_tiles - min_used, steps))
    grid_spec = pltpu.PrefetchScalarGridSpec(
        num_scalar_prefetch=3,
        grid=(steps,),
        in_specs=[pl.BlockSpec((1, TOP_K, tb), lambda i, *_: (i, 0, 0), memory_space=pltpu.SMEM),
                  pl.BlockSpec((tb, w), lambda i, *_: (i, 0))],
        out_specs=pl.BlockSpec(memory_space=pl.ANY),
        scratch_shapes=[pltpu.VMEM((tile, w), U32), pltpu.SemaphoreType.DMA((2,))],
    )
    return pl.pallas_call(
        kern,
        grid_spec=grid_spec,
        out_shape=jax.ShapeDtypeStruct((n_tiles * tile, w), U32),
        compiler_params=_cparams(("arbitrary",)),
        name="dispatch",
    )(pad_start, pad_len, n_used, pos3, x1p)


def _experts_kernel(te_ref, nu_ref, ord_ref, nxt_ref, xs_ref, wg_hbm, wu_hbm, wd_hbm, ys_ref,
                    wgf_ref, wuf_ref, wdf_ref, wgb_ref, wub_ref, wdb_ref, sem_ref):
    i = pl.program_id(0)

    def weight_copies(e, slot):
        return (pltpu.make_async_copy(wg_hbm.at[e], wgf_ref.at[slot], sem_ref.at[slot]),
                pltpu.make_async_copy(wu_hbm.at[e], wuf_ref.at[slot], sem_ref.at[slot]),
                pltpu.make_async_copy(wd_hbm.at[e], wdf_ref.at[slot], sem_ref.at[slot]))

    @pl.when(i < nu_ref[0])
    def _():
        e = te_ref[i]
        slot = lax.rem(ord_ref[i], 2)
        first_tile_of_expert = jnp.logical_or(i == 0, e != te_ref[jnp.maximum(i - 1, 0)])

        @pl.when(i == 0)
        def _():
            for cp in weight_copies(e, slot):
                cp.start()

        @pl.when(first_tile_of_expert)
        def _():
            nxt = nxt_ref[i]

            @pl.when(nxt >= 0)
            def _():
                for cp in weight_copies(nxt, 1 - slot):
                    cp.start()

            for cp in weight_copies(e, slot):
                cp.wait()
            wgb_ref[...] = wgf_ref[slot].astype(BF16)
            wub_ref[...] = wuf_ref[slot].astype(BF16)
            wdb_ref[...] = wdf_ref[slot].astype(BF16)

        x = _unpack_bf16_pairs(xs_ref[...]).astype(BF16)
        gate = jnp.dot(x, wgb_ref[...], preferred_element_type=F32)
        up = jnp.dot(x, wub_ref[...], preferred_element_type=F32)
        hid = (gate * _sigmoid(gate) * up).astype(BF16)
        y = jnp.dot(hid, wdb_ref[...], preferred_element_type=F32)
        ys_ref[...] = _pack_bf16_pairs(y)

    @pl.when(i >= nu_ref[0])
    def _():
        ys_ref[...] = jnp.zeros_like(ys_ref)


def _experts(tile_expert, n_used, expert_ord, expert_next, xs, w_gate_e, w_up_e, w_down_e):
    n_slots, w = xs.shape
    tile = EXPERT_TILE
    n_tiles = n_slots // tile
    _, d, de = w_gate_e.shape
    grid_spec = pltpu.PrefetchScalarGridSpec(
        num_scalar_prefetch=4,
        grid=(n_tiles,),
        in_specs=[pl.BlockSpec((tile, w), lambda i, te, nu, *_: (jnp.minimum(i, nu[0] - 1), 0)),
                  pl.BlockSpec(memory_space=pl.ANY),
                  pl.BlockSpec(memory_space=pl.ANY),
                  pl.BlockSpec(memory_space=pl.ANY)],
        out_specs=pl.BlockSpec((tile, w), lambda i, *_: (i, 0)),
        scratch_shapes=[pltpu.VMEM((2, d, de), F32), pltpu.VMEM((2, d, de), F32), pltpu.VMEM((2, de, d), F32),
                        pltpu.VMEM((d, de), BF16), pltpu.VMEM((d, de), BF16), pltpu.VMEM((de, d), BF16),
                        pltpu.SemaphoreType.DMA((2,))],
    )
    return pl.pallas_call(
        _experts_kernel,
        grid_spec=grid_spec,
        out_shape=jax.ShapeDtypeStruct((n_slots, w), U32),
        compiler_params=_cparams(("arbitrary",)),
        name="experts",
    )(tile_expert, n_used, expert_ord, expert_next, xs, w_gate_e, w_up_e, w_down_e)


def _combine_kernel(pos_ref, posn_ref, wts_ref, x_ref, xp_ref, ys_ref, wgs_ref, wus_ref, wds_ref,
                    g_ref, b_ref, o_ref, buf_ref, sem_ref, *, alpha):
    i = pl.program_id(0)
    n = pl.num_programs(0)
    tb = x_ref.shape[0]
    slot = lax.rem(i, 2)

    def issue(p_ref, s):
        def body(r, carry):
            for k in range(TOP_K):
                _row_copy(ys_ref, p_ref[0, k, r], buf_ref.at[s, k], r, sem_ref.at[s]).start(
                    priority=k % DMA_PRIORITIES)
            return carry
        lax.fori_loop(0, tb, body, 0)

    @pl.when(i == 0)
    def _():
        issue(pos_ref, 0)

    @pl.when(i + 1 < n)
    def _():
        issue(posn_ref, 1 - slot)

    def drain(r, carry):
        for k in range(TOP_K):
            _row_copy(ys_ref, pos_ref[0, k, r], buf_ref.at[slot, k], r, sem_ref.at[slot]).wait()
        return carry

    lax.fori_loop(0, tb, drain, 0)

    wts = wts_ref[...]
    routed = jnp.zeros(x_ref.shape, F32)
    for k in range(TOP_K):
        routed = routed + wts[:, k:k + 1] * _unpack_bf16_pairs(buf_ref[slot, k])
    xb = _unpack_bf16_pairs(xp_ref[...]).astype(BF16)
    gate = jnp.dot(xb, wgs_ref[...], preferred_element_type=F32)
    up = jnp.dot(xb, wus_ref[...], preferred_element_type=F32)
    hid = (gate * _sigmoid(gate) * up).astype(BF16)
    shared = jnp.dot(hid, wds_ref[...], preferred_element_type=F32)
    o_ref[...] = _layer_norm(alpha * x_ref[...] + (routed + shared), g_ref[...], b_ref[...])


def _combine(pos3, wts_c, x1, x1p, ys, w_gate_s, w_up_s, w_down_s, ln_g, ln_b, alpha):
    t, d = x1.shape
    w = x1p.shape[1]
    tb = MOVE_TILE
    nt = t // tb
    ds = w_gate_s.shape[1]
    row = lambda i: (i, 0)
    const = lambda i: (0, 0)
    return pl.pallas_call(
        functools.partial(_combine_kernel, alpha=alpha),
        grid=(nt,),
        in_specs=[pl.BlockSpec((1, TOP_K, tb), lambda i: (i, 0, 0), memory_space=pltpu.SMEM),
                  pl.BlockSpec((1, TOP_K, tb), lambda i: (jnp.minimum(i + 1, nt - 1), 0, 0),
                               memory_space=pltpu.SMEM),
                  pl.BlockSpec((tb, TOP_K), row),
                  pl.BlockSpec((tb, d), row),
                  pl.BlockSpec((tb, w), row),
                  pl.BlockSpec(memory_space=pl.ANY),
                  pl.BlockSpec((d, ds), const),
                  pl.BlockSpec((d, ds), const),
                  pl.BlockSpec((ds, d), const),
                  pl.BlockSpec((1, d), const),
                  pl.BlockSpec((1, d), const)],
        out_specs=pl.BlockSpec((tb, d), row),
        out_shape=jax.ShapeDtypeStruct((t, d), F32),
        scratch_shapes=[pltpu.VMEM((2, TOP_K, tb, w), U32), pltpu.SemaphoreType.DMA((2,))],
        compiler_params=_cparams(("arbitrary",)),
        name="combine",
    )(pos3, pos3, wts_c, x1, x1p, ys, w_gate_s, w_up_s, w_down_s, ln_g, ln_b)


def _layer(alpha, x, w_in, b_if, b_gate, conv_qk, pool_w, pool_scale, mh_norm_w, w_b_down, w_out,
           ln1_g, ln1_b, w_router, b_router, w_gate_e, w_up_e, w_down_e,
           w_gate_s, w_up_s, w_down_s, ln2_g, ln2_b):
    bsz, seq, d = x.shape
    t = bsz * seq
    heads = N_HEADS
    pool_width = pool_w.shape[0] * pool_w.shape[1]
    qk_cols = conv_qk.shape[1]
    v_cols = mh_norm_w.shape[0] * mh_norm_w.shape[1]
    o_cols = v_cols
    if_cols = b_if.shape[0]
    dv = mh_norm_w.shape[1]
    dqk = qk_cols // (2 * heads)
    off_if = pool_width + qk_cols + v_cols + o_cols
    off_gate = off_if + if_cols

    xf = x.reshape(t, d)
    xb = xf.astype(BF16)
    w_main = jnp.concatenate([w_in[:, :off_if], w_in[:, off_gate:]], axis=1).astype(BF16)
    w_if = jnp.pad(w_in[:, off_if:off_gate], ((0, 0), (0, LANES - if_cols))).astype(BF16)
    bias_if = jnp.pad(b_if, (0, LANES - if_cols)).reshape(1, LANES)
    n_main = w_main.shape[1]

    proj = _matmul_bias(xb, w_main, jnp.zeros((1, n_main), F32), BF16, 1024, 1024)
    u_if = _matmul_bias(xb, w_if, bias_if, F32, 1024, LANES)
    proj3 = proj.reshape(bsz, seq, n_main)

    branch_a = _pool(proj3, pool_w.astype(BF16), pool_scale.reshape(1, pool_width))

    cw = 2 * LANES
    q_col0 = pool_width // cw
    half = qk_cols // 2
    q = _qkconv(proj3, conv_qk[:, :half], q_col0, float(dqk) ** -0.5, False)
    kt = _qkconv(proj3, conv_qk[:, half:], q_col0 + half // cw, 1.0, True)

    nc = seq // MLSTM_CHUNK
    gates = u_if[:, :if_cols].reshape(bsz, seq, 4, heads).transpose(0, 3, 2, 1)
    gates = jnp.pad(gates, ((0, 0), (0, 0), (0, SUBLANES - 4), (0, 0)))
    gates_c = gates.transpose(0, 1, 3, 2)
    gates_r = gates.reshape(bsz, heads, SUBLANES, nc, MLSTM_CHUNK).transpose(0, 1, 3, 2, 4)
    hg = _mlstm(q, kt, proj3, gates_r, gates_c, mh_norm_w.reshape(heads, 1, dv),
                pool_width + qk_cols, pool_width + qk_cols + v_cols)

    ga_col = (pool_width + qk_cols + v_cols + o_cols) // d
    x1, x1p = _mixout(hg.reshape(t, heads * dv), branch_a.reshape(t, pool_width), proj, xf,
                      w_b_down.astype(BF16), w_out.astype(BF16), b_gate.reshape(1, 2 * d),
                      ln1_g.reshape(1, d), ln1_b.reshape(1, d), ga_col, alpha)

    ne = w_router.shape[1]
    wr_t = w_router.T
    wr_hi = wr_t.astype(BF16)
    wr_lo = (wr_t - wr_hi.astype(F32)).astype(BF16)
    eidx, wts, rank, cnt = _router(x1, wr_hi, wr_lo, b_router.reshape(ne, 1))

    tile = EXPERT_TILE
    n_tiles = (t * TOP_K) // tile + ne
    counts = cnt[:, 0]
    pcounts = ((counts + tile - 1) // tile) * tile
    pends = jnp.cumsum(pcounts)
    pstarts = pends - pcounts
    pos = _slots(eidx, rank, pstarts)

    n_used = (pends[-1] // tile).astype(I32)
    tile_ids = jnp.minimum(jnp.arange(n_tiles, dtype=I32), n_used - 1)
    tile_expert = jnp.sum((pends[None, :] <= (tile_ids * tile)[:, None]).astype(I32), axis=1)
    tile_expert = jnp.minimum(tile_expert, ne - 1)
    new_expert = jnp.concatenate([jnp.ones((1,), I32), (tile_expert[1:] != tile_expert[:-1]).astype(I32)])
    expert_ord = jnp.cumsum(new_expert) - 1
    candidates = jnp.where(counts > 0, jnp.arange(ne, dtype=I32), ne)
    later_min = lax.cummin(candidates, axis=0, reverse=True)
    next_used = jnp.concatenate([later_min[1:], jnp.full((1,), ne, I32)])
    next_used = jnp.where(next_used >= ne, -1, next_used)
    expert_next = jnp.sum(jnp.where(tile_expert[:, None] == jnp.arange(ne, dtype=I32)[None, :],
                                    next_used[None, :], 0), axis=1).astype(I32)

    tb = MOVE_TILE
    pos3 = pos.reshape(TOP_K, t // tb, tb).transpose(1, 0, 2)
    n_used = n_used.reshape(1)
    xs = _dispatch(x1p, pos3, (pstarts + counts).astype(I32), (pcounts - counts).astype(I32), n_used, n_tiles, ne)
    ys = _experts(tile_expert, n_used, expert_ord.astype(I32), expert_next, xs, w_gate_e, w_up_e, w_down_e)
    out = _combine(pos3, wts.T, x1, x1p, ys, w_gate_s.astype(BF16), w_up_s.astype(BF16),
                   w_down_s.astype(BF16), ln2_g.reshape(1, d), ln2_b.reshape(1, d), alpha)
    return out.reshape(bsz, seq, d)


def kernel(x, w_in, b_if, b_gate, conv_qk, pool_w, pool_scale, mh_norm_w, w_b_down, w_out, ln1_g, ln1_b,
           w_router, b_router, w_gate_e, w_up_e, w_down_e, w_gate_s, w_up_s, w_down_s, ln2_g, ln2_b):
    depth = w_in.shape[0]
    alpha = (2.0 * depth) ** 0.25
    for l in range(depth):
        x = _layer(alpha, x, w_in[l], b_if[l], b_gate[l], conv_qk[l], pool_w[l], pool_scale[l], mh_norm_w[l],
                   w_b_down[l], w_out[l], ln1_g[l], ln1_b[l], w_router[l], b_router[l], w_gate_e[l],
                   w_up_e[l], w_down_e[l], w_gate_s[l], w_up_s[l], w_down_s[l], ln2_g[l], ln2_b[l])
    return x
```

```python
import dataclasses
import functools

import jax
import jax.numpy as jnp
import numpy as np
from jax import lax
from jax.experimental import pallas as pl
from jax.experimental.pallas import tpu as pltpu
from jax.experimental.pallas import tpu_sc as plsc

F32 = jnp.float32
BF16 = jnp.bfloat16
I32 = jnp.int32
U32 = jnp.uint32

N_HEADS = 4
POOL_GROUPS = 4
CONV_WIDTH = 5
LN_EPS = 1e-5
N_ROUTE_GROUPS = 8
TOPK_GROUPS = 4
TOP_K = 8
ROUTED_SCALE = 2.5

LANES = 128
SUBLANES = 8
BF16_ROWS = 16
VMEM_LIMIT = 56 * 1024 * 1024

MLSTM_CHUNK = 256
MLSTM_HEADS_PER_STEP = 2
SEQ_TILE = 512
ROW_TILE = 512
ROUTER_TILE = 256
EXPERT_TILE = 256
COMBINE_TILE = 128
SLOT_TILE = 512

SC_CORES = 2
SC_SUBCORES = 16
SC_LANES = 16
SC_WORKERS = SC_CORES * SC_SUBCORES
SC_GATHER_ROWS = 64
SC_SCAN_CHUNK = 2048

HI_MASK = 0xFFFF0000


def _cparams(sem):
    return pltpu.CompilerParams(dimension_semantics=sem, vmem_limit_bytes=VMEM_LIMIT)


def _sigmoid(x):
    return 1.0 / (1.0 + jnp.exp(-x))


def _layer_norm(y, g, b):
    mu = jnp.mean(y, axis=-1, keepdims=True)
    yc = y - mu
    var = jnp.mean(yc * yc, axis=-1, keepdims=True)
    return yc * lax.rsqrt(var + LN_EPS) * g + b


def _pack_bf16_pairs(y):
    c = y.shape[1] // 2
    bits = lax.bitcast_convert_type(y.astype(BF16).astype(F32), U32)
    return (bits[:, :c] >> 16) | (bits[:, c:] & jnp.uint32(HI_MASK))


def _unpack_bf16_pairs(p):
    lo = lax.bitcast_convert_type(p << 16, F32)
    hi = lax.bitcast_convert_type(p & jnp.uint32(HI_MASK), F32)
    return jnp.concatenate([lo, hi], axis=1)


def _matmul_kernel(x_ref, w_ref, b_ref, o_ref):
    acc = jnp.dot(x_ref[...], w_ref[...], preferred_element_type=F32)
    o_ref[...] = (acc + b_ref[...]).astype(o_ref.dtype)


def _matmul_bias(x, w, bias, out_dtype, tm, tn):
    m, k = x.shape
    n = w.shape[1]
    return pl.pallas_call(
        _matmul_kernel,
        grid=(m // tm, n // tn),
        in_specs=[pl.BlockSpec((tm, k), lambda i, j: (i, 0)),
                  pl.BlockSpec((k, tn), lambda i, j: (0, j)),
                  pl.BlockSpec((1, tn), lambda i, j: (0, j))],
        out_specs=pl.BlockSpec((tm, tn), lambda i, j: (i, j)),
        out_shape=jax.ShapeDtypeStruct((m, n), out_dtype),
        compiler_params=_cparams(("parallel", "parallel")),
        name="inproj",
    )(x, w, bias)


def _qkconv_kernel(prev_ref, main_ref, next_ref, w_ref, o_ref, *, ts, scale, transpose):
    t = pl.program_id(1)
    nt = pl.num_programs(1)
    main = main_ref[0].astype(F32)
    prev = prev_ref[0].astype(F32)[BF16_ROWS - SUBLANES:]
    nxt = next_ref[0].astype(F32)[:SUBLANES]
    prev = jnp.where(t > 0, prev, 0.0)
    nxt = jnp.where(t < nt - 1, nxt, 0.0)
    ext = jnp.concatenate([prev, main, nxt], axis=0)
    w = w_ref[...]
    pad = CONV_WIDTH // 2
    acc = jnp.zeros_like(main)
    for j in range(CONV_WIDTH):
        off = SUBLANES - pad + j
        acc = acc + ext[off:off + ts] * w[j:j + 1]
    y = acc * _sigmoid(acc) * scale
    if transpose:
        o_ref[0] = y.T.astype(o_ref.dtype)
    else:
        o_ref[0] = y.astype(o_ref.dtype)


def _qkconv(proj3, conv_w, col0, scale, transpose):
    bsz, seq, _ = proj3.shape
    ts = SEQ_TILE
    nt = seq // ts
    cw = 2 * LANES
    ncol = conv_w.shape[1] // cw
    hb = ts // BF16_ROWS
    n_hb = seq // BF16_ROWS
    kern = functools.partial(_qkconv_kernel, ts=ts, scale=scale, transpose=transpose)
    if transpose:
        out_shape = jax.ShapeDtypeStruct((bsz, ncol * cw, seq), BF16)
        out_spec = pl.BlockSpec((1, cw, ts), lambda b, t, j: (b, j, t))
    else:
        out_shape = jax.ShapeDtypeStruct((bsz, seq, ncol * cw), BF16)
        out_spec = pl.BlockSpec((1, ts, cw), lambda b, t, j: (b, t, j))
    return pl.pallas_call(
        kern,
        grid=(bsz, nt, ncol),
        in_specs=[pl.BlockSpec((1, BF16_ROWS, cw), lambda b, t, j: (b, jnp.maximum(t * hb - 1, 0), col0 + j)),
                  pl.BlockSpec((1, ts, cw), lambda b, t, j: (b, t, col0 + j)),
                  pl.BlockSpec((1, BF16_ROWS, cw),
                               lambda b, t, j: (b, jnp.minimum((t + 1) * hb, n_hb - 1), col0 + j)),
                  pl.BlockSpec((CONV_WIDTH, cw), lambda b, t, j: (0, j))],
        out_specs=out_spec,
        out_shape=out_shape,
        compiler_params=_cparams(("parallel", "parallel", "parallel")),
        name="qkconv_t" if transpose else "qkconv",
    )(proj3, proj3, proj3, conv_w)


def _pool_kernel(prev_ref, main_ref, next_ref, bm_ref, bp_ref, bn_ref, pw_ref, ps_ref, o_ref, *, ts, seq):
    g = pl.program_id(1)
    t = pl.program_id(2)
    nt = pl.num_programs(2)
    hw = jnp.left_shift(1, g)
    main = main_ref[0]
    prev = jnp.where(t > 0, prev_ref[0], jnp.zeros_like(prev_ref[0]))
    nxt = jnp.where(t < nt - 1, next_ref[0], jnp.zeros_like(next_ref[0]))
    s = (jnp.dot(bm_ref[0], main, preferred_element_type=F32)
         + jnp.dot(bp_ref[0], prev, preferred_element_type=F32)
         + jnp.dot(bn_ref[0], nxt, preferred_element_type=F32))
    tabs = t * ts + lax.broadcasted_iota(I32, (ts, 1), 0)
    cnt = jnp.minimum(tabs + hw, seq) - jnp.maximum(tabs - hw, 0)
    pooled = s / cnt.astype(F32) - main.astype(F32)
    mixed = jnp.dot(pooled.astype(BF16), pw_ref[0], preferred_element_type=F32) * ps_ref[...]
    o_ref[0] = mixed.astype(o_ref.dtype)


def _pool(proj3, pool_w, pool_scale):
    bsz, seq, _ = proj3.shape
    ts = SEQ_TILE
    nt = seq // ts
    cw = pool_w.shape[-1]
    hb = ts // LANES
    n_hb = seq // LANES
    i = np.arange(ts)[:, None]
    hws = [1 << g for g in range(POOL_GROUPS)]
    band_m = np.stack([(np.arange(ts)[None, :] >= i - hw) & (np.arange(ts)[None, :] < i + hw) for hw in hws])
    band_p = np.stack([(np.arange(LANES)[None, :] - LANES >= i - hw) for hw in hws])
    band_n = np.stack([(np.arange(LANES)[None, :] + ts < i + hw) for hw in hws])
    band_m, band_p, band_n = (jnp.asarray(b.astype(np.float32), BF16) for b in (band_m, band_p, band_n))
    kern = functools.partial(_pool_kernel, ts=ts, seq=seq)
    per_group = lambda b, g, t: (g, 0, 0)
    return pl.pallas_call(
        kern,
        grid=(bsz, POOL_GROUPS, nt),
        in_specs=[pl.BlockSpec((1, LANES, cw), lambda b, g, t: (b, jnp.maximum(t * hb - 1, 0), g)),
                  pl.BlockSpec((1, ts, cw), lambda b, g, t: (b, t, g)),
                  pl.BlockSpec((1, LANES, cw), lambda b, g, t: (b, jnp.minimum((t + 1) * hb, n_hb - 1), g)),
                  pl.BlockSpec((1, ts, ts), per_group),
                  pl.BlockSpec((1, ts, LANES), per_group),
                  pl.BlockSpec((1, ts, LANES), per_group),
                  pl.BlockSpec((1, cw, cw), per_group),
                  pl.BlockSpec((1, cw), lambda b, g, t: (0, g))],
        out_specs=pl.BlockSpec((1, ts, cw), lambda b, g, t: (b, t, g)),
        out_shape=jax.ShapeDtypeStruct((bsz, seq, POOL_GROUPS * cw), BF16),
        compiler_params=_cparams(("parallel", "parallel", "parallel")),
        name="pool",
    )(proj3, proj3, proj3, band_m, band_p, band_n, pool_w, pool_scale)


def _dot_split(a, b, a_is_value):
    val = a if a_is_value else b
    hi = val.astype(BF16)
    lo = (val - hi.astype(F32)).astype(BF16)
    if a_is_value:
        return (jnp.dot(hi, b, preferred_element_type=F32) + jnp.dot(lo, b, preferred_element_type=F32))
    return (jnp.dot(a, hi, preferred_element_type=F32) + jnp.dot(a, lo, preferred_element_type=F32))


def _mlstm_kernel(q_ref, kt_ref, v_ref, uo_ref, gr_ref, gc_ref, nw_ref, o_ref,
                  cf_ref, nf_ref, mf_ref, cb_ref, nb_ref, mb_ref, cbs_ref, nbs_ref, mbs_ref,
                  *, chunk, n_chunks, heads):
    L = chunk
    dqk = kt_ref.shape[1] // heads
    dv = v_ref.shape[2] // heads
    p = pl.program_id(1)
    c = pl.program_id(2)
    row = lax.broadcasted_iota(I32, (L, L), 0)
    col = lax.broadcasted_iota(I32, (L, L), 1)
    tri_le = row <= col
    tri_ge = row >= col
    m_le = jnp.where(tri_le, 1.0, 0.0).astype(BF16)
    m_ge = jnp.where(tri_ge, 1.0, 0.0).astype(BF16)
    lane_r = lax.broadcasted_iota(I32, (1, L), 1)
    neg_inf = jnp.float32(-jnp.inf)

    hs = range(heads)
    kts = [kt_ref[0, hh * dqk:(hh + 1) * dqk, :] for hh in hs]
    vs = [v_ref[0, :, hh * dv:(hh + 1) * dv] for hh in hs]
    gates = [gr_ref[0, hh, 0] for hh in hs]
    lf_r = [jax.nn.log_sigmoid(g) for g in gates]

    def update_state(c_ref, n_ref, m_ref, g_r, tot):
        m_prev = [m_ref[hh] for hh in hs]
        m_new = [jnp.maximum(tot[hh] + m_prev[hh], jnp.max(g_r[hh], axis=1, keepdims=True)) for hh in hs]
        decay = [jnp.exp(tot[hh] + m_prev[hh] - m_new[hh]) for hh in hs]
        kw = [kts[hh].astype(F32) * jnp.exp(g_r[hh] - m_new[hh]) for hh in hs]
        upd = [jnp.dot(kw[hh].astype(BF16), vs[hh], preferred_element_type=F32) for hh in hs]
        for hh in hs:
            c_ref[hh] = decay[hh] * c_ref[hh] + upd[hh]
            n_ref[hh] = decay[hh] * n_ref[hh] + jnp.sum(kw[hh], axis=1, keepdims=True)
            m_ref[hh] = m_new[hh]

    @pl.when(p == 0)
    def _backward_states():
        @pl.when(c == 0)
        def _():
            cb_ref[...] = jnp.zeros_like(cb_ref)
            nb_ref[...] = jnp.zeros_like(nb_ref)
            mb_ref[...] = jnp.zeros_like(mb_ref)

        cc = n_chunks - 1 - c
        cbs_ref[cc] = cb_ref[...].astype(BF16)
        nbs_ref[cc] = nb_ref[...]
        mbs_ref[cc] = mb_ref[...]
        a_r = [_dot_split(lf_r[hh], m_ge, True)[3:4] for hh in hs]
        a0 = [jnp.sum(jnp.where(lane_r == 0, a_r[hh], 0.0), axis=1, keepdims=True) for hh in hs]
        g_r = [a0[hh] - a_r[hh] + gates[hh][2:3] for hh in hs]
        update_state(cb_ref, nb_ref, mb_ref, g_r, a0)

    @pl.when(p == 1)
    def _outputs():
        @pl.when(c == 0)
        def _():
            cf_ref[...] = jnp.zeros_like(cf_ref)
            nf_ref[...] = jnp.zeros_like(nf_ref)
            mf_ref[...] = jnp.zeros_like(mf_ref)

        nb_in = nbs_ref[c]
        mb_in = mbs_ref[c]
        cb_in = cbs_ref[c]
        qs = [q_ref[0, :, hh * dqk:(hh + 1) * dqk] for hh in hs]
        lf_c = [jax.nn.log_sigmoid(gc_ref[0, hh]) for hh in hs]
        b_r = [_dot_split(lf_r[hh], m_le, True)[1:2] for hh in hs]
        a_r = [_dot_split(lf_r[hh], m_ge, True)[3:4] for hh in hs]
        b_c = [_dot_split(m_ge, lf_c[hh], False)[:, 1:2] for hh in hs]
        a_c = [_dot_split(m_le, lf_c[hh], False)[:, 3:4] for hh in hs]
        li_f = [gates[hh][0:1] for hh in hs]
        li_b = [gates[hh][2:3] for hh in hs]

        s = [jnp.dot(qs[hh], kts[hh], preferred_element_type=F32) for hh in hs]
        nlane = lax.broadcasted_iota(I32, (dqk, LANES), 1)
        nmat = [jnp.where(nlane == 0, nf_ref[hh], jnp.where(nlane == 1, nb_in[hh], 0.0)).astype(BF16)
                for hh in hs]
        qn = [jnp.dot(qs[hh], nmat[hh], preferred_element_type=F32) for hh in hs]

        def direction(d, mask, cum_c, m_prev, qn_col):
            d = [jnp.where(mask, d[hh], neg_inf) for hh in hs]
            m_inter = [cum_c[hh] + m_prev[hh] for hh in hs]
            m_t = [jnp.maximum(m_inter[hh], jnp.max(d[hh], axis=1, keepdims=True)) for hh in hs]
            pmat = [jnp.exp(d[hh] - m_t[hh]) * s[hh] for hh in hs]
            w_inter = [jnp.exp(m_inter[hh] - m_t[hh]) for hh in hs]
            den = [jnp.sum(pmat[hh], axis=1, keepdims=True) + w_inter[hh] * qn_col[hh] for hh in hs]
            r = [1.0 / jnp.maximum(jnp.abs(den[hh]), jnp.exp(-m_t[hh])) for hh in hs]
            return [pmat[hh] * r[hh] for hh in hs], [w_inter[hh] * r[hh] for hh in hs]

        pf, sf = direction([b_c[hh] - (b_r[hh] - li_f[hh]) for hh in hs], tri_ge, b_c,
                           [mf_ref[hh] for hh in hs], [qn[hh][:, 0:1] for hh in hs])
        pb, sb = direction([a_c[hh] - (a_r[hh] - li_b[hh]) for hh in hs], tri_le, a_c,
                           [mb_in[hh] for hh in hs], [qn[hh][:, 1:2] for hh in hs])
        qf = [qs[hh].astype(F32) for hh in hs]
        h = [jnp.dot((pf[hh] + pb[hh]).astype(BF16), vs[hh], preferred_element_type=F32)
             + jnp.dot((qf[hh] * sf[hh]).astype(BF16), cf_ref[hh].astype(BF16), preferred_element_type=F32)
             + jnp.dot((qf[hh] * sb[hh]).astype(BF16), cb_in[hh], preferred_element_type=F32) for hh in hs]

        mu = [jnp.mean(h[hh], axis=1, keepdims=True) for hh in hs]
        hc = [h[hh] - mu[hh] for hh in hs]
        var = [jnp.mean(hc[hh] * hc[hh], axis=1, keepdims=True) for hh in hs]
        hn = [hc[hh] * lax.rsqrt(var[hh] + LN_EPS) * nw_ref[hh] for hh in hs]
        for hh in hs:
            gate_o = _sigmoid(uo_ref[0, :, hh * dv:(hh + 1) * dv].astype(F32))
            o_ref[0, :, hh * dv:(hh + 1) * dv] = (gate_o * hn[hh]).astype(o_ref.dtype)

        b_last = [jnp.sum(jnp.where(lane_r == L - 1, b_r[hh], 0.0), axis=1, keepdims=True) for hh in hs]
        update_state(cf_ref, nf_ref, mf_ref, [b_last[hh] - b_r[hh] + li_f[hh] for hh in hs], b_last)


def _mlstm(q, kt, proj3, gates_r, gates_c, norm_w, v_off, o_off):
    bsz, seq, qw = q.shape
    dqk = qw // N_HEADS
    dv = norm_w.shape[-1]
    L = MLSTM_CHUNK
    nc = seq // L
    hg = MLSTM_HEADS_PER_STEP
    groups = N_HEADS // hg
    assert N_HEADS % hg == 0 and v_off % (hg * dv) == 0 and o_off % (hg * dv) == 0
    v_blk0 = v_off // (hg * dv)
    o_blk0 = o_off // (hg * dv)
    kern = functools.partial(_mlstm_kernel, chunk=L, n_chunks=nc, heads=hg)

    def chunk_of(p, c):
        return jnp.where(p == 0, nc - 1 - c, c)

    def out_chunk(p, c):
        return jnp.where(p == 0, 0, c)

    return pl.pallas_call(
        kern,
        grid=(bsz * groups, 2, nc),
        in_specs=[
            pl.BlockSpec((1, L, hg * dqk), lambda bg, p, c: (bg // groups, out_chunk(p, c), bg % groups)),
            pl.BlockSpec((1, hg * dqk, L), lambda bg, p, c: (bg // groups, bg % groups, chunk_of(p, c))),
            pl.BlockSpec((1, L, hg * dv), lambda bg, p, c: (bg // groups, chunk_of(p, c), v_blk0 + bg % groups)),
            pl.BlockSpec((1, L, hg * dv), lambda bg, p, c: (bg // groups, out_chunk(p, c), o_blk0 + bg % groups)),
            pl.BlockSpec((1, hg, 1, SUBLANES, L),
                         lambda bg, p, c: (bg // groups, bg % groups, chunk_of(p, c), 0, 0)),
            pl.BlockSpec((1, hg, L, SUBLANES),
                         lambda bg, p, c: (bg // groups, bg % groups, out_chunk(p, c), 0)),
            pl.BlockSpec((hg, 1, dv), lambda bg, p, c: (bg % groups, 0, 0)),
        ],
        out_specs=pl.BlockSpec((1, L, hg * dv), lambda bg, p, c: (bg // groups, out_chunk(p, c), bg % groups)),
        out_shape=jax.ShapeDtypeStruct((bsz, seq, N_HEADS * dv), BF16),
        scratch_shapes=[
            pltpu.VMEM((hg, dqk, dv), F32), pltpu.VMEM((hg, dqk, 1), F32), pltpu.VMEM((hg, 1, 1), F32),
            pltpu.VMEM((hg, dqk, dv), F32), pltpu.VMEM((hg, dqk, 1), F32), pltpu.VMEM((hg, 1, 1), F32),
            pltpu.VMEM((nc, hg, dqk, dv), BF16), pltpu.VMEM((nc, hg, dqk, 1), F32),
            pltpu.VMEM((nc, hg, 1, 1), F32),
        ],
        compiler_params=_cparams(("parallel", "arbitrary", "arbitrary")),
        name="mlstm",
    )(q, kt, proj3, proj3, gates_r, gates_c, norm_w)


def _mixout_kernel(hg_ref, a_ref, uga_ref, ugb_ref, x_ref, wbd_ref, wo_ref, bga_ref, bgb_ref,
                   g_ref, b_ref, o_ref, op_ref, *, alpha):
    branch_b = jnp.dot(hg_ref[...], wbd_ref[...], preferred_element_type=F32)
    ga = _sigmoid(uga_ref[...].astype(F32) + bga_ref[...])
    gb = _sigmoid(ugb_ref[...].astype(F32) + bgb_ref[...])
    merged = ga * a_ref[...].astype(F32) + gb * branch_b
    mix = jnp.dot(merged.astype(BF16), wo_ref[...], preferred_element_type=F32)
    y = _layer_norm(alpha * x_ref[...] + mix, g_ref[...], b_ref[...])
    o_ref[...] = y
    op_ref[...] = _pack_bf16_pairs(y)


def _mixout(hg, branch_a, proj, x, w_b_down, w_out, b_gate, ln_g, ln_b, ga_col, alpha):
    t, d = x.shape
    tm = ROW_TILE
    inner = hg.shape[1]
    row = lambda i: (i, 0)
    const = lambda i: (0, 0)
    return pl.pallas_call(
        functools.partial(_mixout_kernel, alpha=alpha),
        grid=(t // tm,),
        in_specs=[pl.BlockSpec((tm, inner), row),
                  pl.BlockSpec((tm, d), row),
                  pl.BlockSpec((tm, d), lambda i: (i, ga_col)),
                  pl.BlockSpec((tm, d), lambda i: (i, ga_col + 1)),
                  pl.BlockSpec((tm, d), row),
                  pl.BlockSpec((inner, d), const),
                  pl.BlockSpec((d, d), const),
                  pl.BlockSpec((1, d), const),
                  pl.BlockSpec((1, d), lambda i: (0, 1)),
                  pl.BlockSpec((1, d), const),
                  pl.BlockSpec((1, d), const)],
        out_specs=[pl.BlockSpec((tm, d), row), pl.BlockSpec((tm, d // 2), row)],
        out_shape=[jax.ShapeDtypeStruct((t, d), F32), jax.ShapeDtypeStruct((t, d // 2), U32)],
        compiler_params=_cparams(("parallel",)),
        name="mixout",
    )(hg, branch_a, proj, proj, x, w_b_down, w_out, b_gate, b_gate, ln_g, ln_b)


def _router_kernel(x_ref, wh_ref, wl_ref, br_ref, eidx_ref, wts_ref, rank_ref, cnt_ref, run_ref,
                   *, n_experts):
    i = pl.program_id(0)
    tm = x_ref.shape[0]
    ne = n_experts
    per_group = ne // N_ROUTE_GROUPS

    @pl.when(i == 0)
    def _():
        run_ref[...] = jnp.zeros_like(run_ref)

    x = x_ref[...]
    xh = x.astype(BF16)
    xl = (x - xh.astype(F32)).astype(BF16)
    nt_dims = (((1,), (1,)), ((), ()))
    logits = (lax.dot_general(wh_ref[...], xh, nt_dims, preferred_element_type=F32)
              + lax.dot_general(wh_ref[...], xl, nt_dims, preferred_element_type=F32)
              + lax.dot_general(wl_ref[...], xh, nt_dims, preferred_element_type=F32))
    scores = _sigmoid(logits)
    sel = scores + br_ref[...]
    neg = jnp.float32(-jnp.inf)
    big = jnp.float32(1e9)
    rowi = lax.broadcasted_iota(I32, (ne, tm), 0).astype(F32)

    gi = lax.broadcasted_iota(I32, (N_ROUTE_GROUPS, tm), 0).astype(F32)
    work = jnp.zeros((N_ROUTE_GROUPS, tm), F32)
    for g in range(N_ROUTE_GROUPS):
        blk = sel[g * per_group:(g + 1) * per_group]
        ri = lax.broadcasted_iota(I32, (per_group, tm), 0).astype(F32) + float(g * per_group)
        m1 = jnp.max(blk, axis=0, keepdims=True)
        i1 = jnp.min(jnp.where(blk == m1, ri, big), axis=0, keepdims=True)
        m2 = jnp.max(jnp.where(ri == i1, neg, blk), axis=0, keepdims=True)
        work = jnp.where(gi == float(g), m1 + m2, work)
    row_group = jnp.floor(rowi * (1.0 / per_group))
    allowed = jnp.zeros((ne, tm), F32)
    for _ in range(TOPK_GROUPS):
        m = jnp.max(work, axis=0, keepdims=True)
        idx = jnp.min(jnp.where(work == m, gi, big), axis=0, keepdims=True)
        work = jnp.where(gi == idx, neg, work)
        allowed = jnp.where(row_group == idx, 1.0, allowed)
    selm = jnp.where(allowed > 0.5, sel, neg)

    member = jnp.zeros((ne, tm), F32)
    idxs, wks = [], []
    for _ in range(TOP_K):
        m = jnp.max(selm, axis=0, keepdims=True)
        idx = jnp.min(jnp.where(selm == m, rowi, big), axis=0, keepdims=True)
        hit = rowi == idx
        wks.append(jnp.sum(jnp.where(hit, scores, 0.0), axis=0, keepdims=True))
        idxs.append(idx)
        member = jnp.where(hit, 1.0, member)
        selm = jnp.where(hit, neg, selm)
    wsum = wks[0]
    for wk in wks[1:]:
        wsum = wsum + wk

    ti = lax.broadcasted_iota(I32, (tm, tm), 0)
    tj = lax.broadcasted_iota(I32, (tm, tm), 1)
    strict = jnp.where(ti < tj, 1.0, 0.0).astype(BF16)
    prefix = jnp.dot(member.astype(BF16), strict, preferred_element_type=F32) + run_ref[...]
    ranks = [jnp.sum(jnp.where(rowi == idx, prefix, 0.0), axis=0, keepdims=True) for idx in idxs]
    run_new = run_ref[...] + jnp.sum(member, axis=1, keepdims=True)
    run_ref[...] = run_new

    eidx_ref[...] = jnp.concatenate(idxs, axis=0).astype(I32)
    wts_ref[...] = jnp.concatenate([wk / wsum * ROUTED_SCALE for wk in wks], axis=0)
    rank_ref[...] = jnp.concatenate(ranks, axis=0).astype(I32)
    cnt_ref[...] = jnp.broadcast_to(run_new, cnt_ref.shape).astype(I32)


def _router(x1, wr_hi, wr_lo, b_router):
    t, d = x1.shape
    ne = wr_hi.shape[0]
    tm = ROUTER_TILE
    kern = functools.partial(_router_kernel, n_experts=ne)
    tok = lambda i: (0, i)
    const = lambda i: (0, 0)
    return pl.pallas_call(
        kern,
        grid=(t // tm,),
        in_specs=[pl.BlockSpec((tm, d), lambda i: (i, 0)),
                  pl.BlockSpec((ne, d), const),
                  pl.BlockSpec((ne, d), const),
                  pl.BlockSpec((ne, 1), const)],
        out_specs=[pl.BlockSpec((TOP_K, tm), tok), pl.BlockSpec((TOP_K, tm), tok),
                   pl.BlockSpec((TOP_K, tm), tok), pl.BlockSpec((ne, LANES), const)],
        out_shape=[jax.ShapeDtypeStruct((TOP_K, t), I32), jax.ShapeDtypeStruct((TOP_K, t), F32),
                   jax.ShapeDtypeStruct((TOP_K, t), I32), jax.ShapeDtypeStruct((ne, LANES), I32)],
        scratch_shapes=[pltpu.VMEM((ne, 1), F32)],
        compiler_params=_cparams(("arbitrary",)),
        name="router",
    )(x1, wr_hi, wr_lo, b_router)


def _slots_kernel(eidx_ref, rank_ref, ps_ref, pos_ref, *, n_experts):
    tm = eidx_ref.shape[1]
    rowi = lax.broadcasted_iota(I32, (n_experts, tm), 0)
    eidx = eidx_ref[...]
    starts = ps_ref[...]
    base = [jnp.sum(jnp.where(rowi == eidx[k:k + 1], starts, 0.0), axis=0, keepdims=True)
            for k in range(TOP_K)]
    pos_ref[...] = jnp.concatenate(base, axis=0).astype(I32) + rank_ref[...]


def _slots(eidx, rank, pstarts):
    k, t = eidx.shape
    ne = pstarts.shape[0]
    tm = SLOT_TILE
    tok = lambda i: (0, i)
    return pl.pallas_call(
        functools.partial(_slots_kernel, n_experts=ne),
        grid=(t // tm,),
        in_specs=[pl.BlockSpec((k, tm), tok), pl.BlockSpec((k, tm), tok),
                  pl.BlockSpec((ne, 1), lambda i: (0, 0))],
        out_specs=pl.BlockSpec((k, tm), tok),
        out_shape=jax.ShapeDtypeStruct((k, t), I32),
        compiler_params=_cparams(("parallel",)),
        name="slots",
    )(eidx, rank, pstarts.astype(F32).reshape(ne, 1))


def _sc_worker_id():
    return lax.axis_index("subcore") * SC_CORES + lax.axis_index("core")


def _sc_mesh():
    return plsc.VectorSubcoreMesh(core_axis_name="core", subcore_axis_name="subcore")


def _sc_vector_params():
    cp = pltpu.CompilerParams()
    if "needs_layout_passes" in pltpu.CompilerParams.__dataclass_fields__:
        cp = dataclasses.replace(cp, needs_layout_passes=False)
    return cp


def _sc_gather(table, idx):
    n = idx.shape[0]
    w = table.shape[1]
    ch = SC_GATHER_ROWS
    n_ch = n // (SC_WORKERS * ch)
    assert n % (SC_WORKERS * ch * 2) == 0

    @functools.partial(
        pl.kernel, mesh=_sc_mesh(),
        out_type=jax.ShapeDtypeStruct((n, w), table.dtype),
        scratch_types=[pltpu.VMEM((n_ch, ch), I32), pltpu.VMEM((2, ch, w), table.dtype),
                       pltpu.SemaphoreType.DMA((2,))],
    )
    def kern(table_hbm, idx_hbm, out_hbm, idx_v, rows_v, sem):
        first = _sc_worker_id() * n_ch
        pltpu.sync_copy(idx_hbm.at[pl.ds(first, n_ch)], idx_v)

        def gather(j, b):
            return pltpu.make_async_copy(table_hbm.at[idx_v.at[j]], rows_v.at[b], sem.at[b])

        gather(0, 0).start()

        @pl.loop(0, n_ch, step=2)
        def _(j):
            for b in range(2):
                jj = j + b

                @pl.when(jj + 1 < n_ch)
                def _():
                    gather(jj + 1, 1 - b).start()

                gather(jj, b).wait()
                row0 = pl.multiple_of((first + jj) * ch, ch)
                pltpu.sync_copy(rows_v.at[b], out_hbm.at[pl.ds(row0, ch)])

    return kern(table, idx.reshape(n // ch, ch))


def _sc_invert(pos_flat, n_slots, n_tokens):
    n = pos_flat.shape[0]
    per_w = n_slots // SC_WORKERS
    chunk = SC_SCAN_CHUNK
    assert n_slots % (SC_WORKERS * SC_LANES) == 0 and n % chunk == 0 and n_tokens % chunk == 0

    @functools.partial(
        pl.kernel, mesh=_sc_mesh(),
        out_type=jax.ShapeDtypeStruct((n_slots,), I32),
        scratch_types=[pltpu.VMEM((per_w,), I32), pltpu.VMEM((chunk,), I32)],
        compiler_params=_sc_vector_params(),
    )
    def kern(pos_hbm, out_hbm, table_v, pos_v):
        lo = _sc_worker_id() * per_w
        lane = lax.iota(I32, SC_LANES)

        @pl.loop(0, per_w // SC_LANES)
        def _(i):
            table_v[pl.ds(pl.multiple_of(i * SC_LANES, SC_LANES), SC_LANES)] = jnp.zeros((SC_LANES,), I32)

        @pl.loop(0, n // chunk)
        def _(c):
            pltpu.sync_copy(pos_hbm.at[pl.ds(pl.multiple_of(c * chunk, chunk), chunk)], pos_v)
            tok0 = lax.rem(c, n_tokens // chunk) * chunk

            @pl.loop(0, chunk // SC_LANES)
            def _(j):
                off = pl.multiple_of(j * SC_LANES, SC_LANES)
                local = pos_v[pl.ds(off, SC_LANES)] - lo
                mine = (local >= 0) & (local < per_w)
                plsc.store_scatter(table_v, [jnp.where(mine, local, 0)], tok0 + off + lane, mask=mine)

        pltpu.sync_copy(table_v, out_hbm.at[pl.ds(pl.multiple_of(lo, SC_LANES), per_w)])

    return kern(pos_flat)


def _experts_kernel(te_ref, nu_ref, ord_ref, nxt_ref, xs_ref, wg_hbm, wu_hbm, wd_hbm, ys_ref,
                    wgf_ref, wuf_ref, wdf_ref, wgb_ref, wub_ref, wdb_ref, sem_ref):
    i = pl.program_id(0)

    def weight_copies(e, slot):
        return (pltpu.make_async_copy(wg_hbm.at[e], wgf_ref.at[slot], sem_ref.at[slot]),
                pltpu.make_async_copy(wu_hbm.at[e], wuf_ref.at[slot], sem_ref.at[slot]),
                pltpu.make_async_copy(wd_hbm.at[e], wdf_ref.at[slot], sem_ref.at[slot]))

    @pl.when(i < nu_ref[0])
    def _():
        e = te_ref[i]
        slot = lax.rem(ord_ref[i], 2)
        first_tile_of_expert = jnp.logical_or(i == 0, e != te_ref[jnp.maximum(i - 1, 0)])

        @pl.when(i == 0)
        def _():
            for cp in weight_copies(e, slot):
                cp.start()

        @pl.when(first_tile_of_expert)
        def _():
            nxt = nxt_ref[i]

            @pl.when(nxt >= 0)
            def _():
                for cp in weight_copies(nxt, 1 - slot):
                    cp.start()

            for cp in weight_copies(e, slot):
                cp.wait()
            wgb_ref[...] = wgf_ref[slot].astype(BF16)
            wub_ref[...] = wuf_ref[slot].astype(BF16)
            wdb_ref[...] = wdf_ref[slot].astype(BF16)

        x = _unpack_bf16_pairs(xs_ref[...]).astype(BF16)
        gate = jnp.dot(x, wgb_ref[...], preferred_element_type=F32)
        up = jnp.dot(x, wub_ref[...], preferred_element_type=F32)
        hid = (gate * _sigmoid(gate) * up).astype(BF16)
        y = jnp.dot(hid, wdb_ref[...], preferred_element_type=F32)
        ys_ref[...] = _pack_bf16_pairs(y)

    @pl.when(i >= nu_ref[0])
    def _():
        ys_ref[...] = jnp.zeros_like(ys_ref)


def _experts(tile_expert, n_used, expert_ord, expert_next, xs, w_gate_e, w_up_e, w_down_e):
    n_slots, w = xs.shape
    tile = EXPERT_TILE
    n_tiles = n_slots // tile
    _, d, de = w_gate_e.shape
    grid_spec = pltpu.PrefetchScalarGridSpec(
        num_scalar_prefetch=4,
        grid=(n_tiles,),
        in_specs=[pl.BlockSpec((tile, w), lambda i, te, nu, *_: (jnp.minimum(i, nu[0] - 1), 0)),
                  pl.BlockSpec(memory_space=pl.ANY),
                  pl.BlockSpec(memory_space=pl.ANY),
                  pl.BlockSpec(memory_space=pl.ANY)],
        out_specs=pl.BlockSpec((tile, w), lambda i, *_: (i, 0)),
        scratch_shapes=[pltpu.VMEM((2, d, de), F32), pltpu.VMEM((2, d, de), F32), pltpu.VMEM((2, de, d), F32),
                        pltpu.VMEM((d, de), BF16), pltpu.VMEM((d, de), BF16), pltpu.VMEM((de, d), BF16),
                        pltpu.SemaphoreType.DMA((2,))],
    )
    return pl.pallas_call(
        _experts_kernel,
        grid_spec=grid_spec,
        out_shape=jax.ShapeDtypeStruct((n_slots, w), U32),
        compiler_params=_cparams(("arbitrary",)),
        name="experts",
    )(tile_expert, n_used, expert_ord, expert_next, xs, w_gate_e, w_up_e, w_down_e)


def _combine_kernel(wts_ref, x_ref, xp_ref, yg_ref, wgs_ref, wus_ref, wds_ref, g_ref, b_ref, o_ref, *, alpha):
    w = xp_ref.shape[1]
    wts = wts_ref[...]
    routed = jnp.zeros(x_ref.shape, F32)
    for k in range(TOP_K):
        routed = routed + wts[:, k:k + 1] * _unpack_bf16_pairs(yg_ref[:, k * w:(k + 1) * w])
    xb = _unpack_bf16_pairs(xp_ref[...]).astype(BF16)
    gate = jnp.dot(xb, wgs_ref[...], preferred_element_type=F32)
    up = jnp.dot(xb, wus_ref[...], preferred_element_type=F32)
    hid = (gate * _sigmoid(gate) * up).astype(BF16)
    shared = jnp.dot(hid, wds_ref[...], preferred_element_type=F32)
    o_ref[...] = _layer_norm(alpha * x_ref[...] + (routed + shared), g_ref[...], b_ref[...])


def _combine(wts_c, x1, x1p, yg, w_gate_s, w_up_s, w_down_s, ln_g, ln_b, alpha):
    t, d = x1.shape
    w = x1p.shape[1]
    tb = COMBINE_TILE
    ds = w_gate_s.shape[1]
    row = lambda i: (i, 0)
    const = lambda i: (0, 0)
    return pl.pallas_call(
        functools.partial(_combine_kernel, alpha=alpha),
        grid=(t // tb,),
        in_specs=[pl.BlockSpec((tb, TOP_K), row),
                  pl.BlockSpec((tb, d), row),
                  pl.BlockSpec((tb, w), row),
                  pl.BlockSpec((tb, TOP_K * w), row),
                  pl.BlockSpec((d, ds), const),
                  pl.BlockSpec((d, ds), const),
                  pl.BlockSpec((ds, d), const),
                  pl.BlockSpec((1, d), const),
                  pl.BlockSpec((1, d), const)],
        out_specs=pl.BlockSpec((tb, d), row),
        out_shape=jax.ShapeDtypeStruct((t, d), F32),
        compiler_params=_cparams(("parallel",)),
        name="combine",
    )(wts_c, x1, x1p, yg, w_gate_s, w_up_s, w_down_s, ln_g, ln_b)


def _layer(alpha, x, w_in, b_if, b_gate, conv_qk, pool_w, pool_scale, mh_norm_w, w_b_down, w_out,
           ln1_g, ln1_b, w_router, b_router, w_gate_e, w_up_e, w_down_e,
           w_gate_s, w_up_s, w_down_s, ln2_g, ln2_b):
    bsz, seq, d = x.shape
    t = bsz * seq
    heads = N_HEADS
    pool_width = pool_w.shape[0] * pool_w.shape[1]
    qk_cols = conv_qk.shape[1]
    v_cols = mh_norm_w.shape[0] * mh_norm_w.shape[1]
    o_cols = v_cols
    if_cols = b_if.shape[0]
    dv = mh_norm_w.shape[1]
    dqk = qk_cols // (2 * heads)
    off_if = pool_width + qk_cols + v_cols + o_cols
    off_gate = off_if + if_cols

    xf = x.reshape(t, d)
    xb = xf.astype(BF16)
    w_main = jnp.concatenate([w_in[:, :off_if], w_in[:, off_gate:]], axis=1).astype(BF16)
    w_if = jnp.pad(w_in[:, off_if:off_gate], ((0, 0), (0, LANES - if_cols))).astype(BF16)
    bias_if = jnp.pad(b_if, (0, LANES - if_cols)).reshape(1, LANES)
    n_main = w_main.shape[1]

    proj = _matmul_bias(xb, w_main, jnp.zeros((1, n_main), F32), BF16, 1024, 1024)
    u_if = _matmul_bias(xb, w_if, bias_if, F32, 1024, LANES)
    proj3 = proj.reshape(bsz, seq, n_main)

    branch_a = _pool(proj3, pool_w.astype(BF16), pool_scale.reshape(1, pool_width))

    cw = 2 * LANES
    q_col0 = pool_width // cw
    half = qk_cols // 2
    q = _qkconv(proj3, conv_qk[:, :half], q_col0, float(dqk) ** -0.5, False)
    kt = _qkconv(proj3, conv_qk[:, half:], q_col0 + half // cw, 1.0, True)

    nc = seq // MLSTM_CHUNK
    gates = u_if[:, :if_cols].reshape(bsz, seq, 4, heads).transpose(0, 3, 2, 1)
    gates = jnp.pad(gates, ((0, 0), (0, 0), (0, SUBLANES - 4), (0, 0)))
    gates_c = gates.transpose(0, 1, 3, 2)
    gates_r = gates.reshape(bsz, heads, SUBLANES, nc, MLSTM_CHUNK).transpose(0, 1, 3, 2, 4)
    hg = _mlstm(q, kt, proj3, gates_r, gates_c, mh_norm_w.reshape(heads, 1, dv),
                pool_width + qk_cols, pool_width + qk_cols + v_cols)

    ga_col = (pool_width + qk_cols + v_cols + o_cols) // d
    x1, x1p = _mixout(hg.reshape(t, heads * dv), branch_a.reshape(t, pool_width), proj, xf,
                      w_b_down.astype(BF16), w_out.astype(BF16), b_gate.reshape(1, 2 * d),
                      ln1_g.reshape(1, d), ln1_b.reshape(1, d), ga_col, alpha)

    ne = w_router.shape[1]
    wr_t = w_router.T
    wr_hi = wr_t.astype(BF16)
    wr_lo = (wr_t - wr_hi.astype(F32)).astype(BF16)
    eidx, wts, rank, cnt = _router(x1, wr_hi, wr_lo, b_router.reshape(ne, 1))

    tile = EXPERT_TILE
    n_tiles = (t * TOP_K) // tile + ne
    counts = cnt[:, 0]
    pcounts = ((counts + tile - 1) // tile) * tile
    pends = jnp.cumsum(pcounts)
    pstarts = pends - pcounts
    pos = _slots(eidx, rank, pstarts)

    n_used = (pends[-1] // tile).astype(I32)
    tile_ids = jnp.minimum(jnp.arange(n_tiles, dtype=I32), n_used - 1)
    tile_expert = jnp.sum((pends[None, :] <= (tile_ids * tile)[:, None]).astype(I32), axis=1)
    tile_expert = jnp.minimum(tile_expert, ne - 1)
    new_expert = jnp.concatenate([jnp.ones((1,), I32), (tile_expert[1:] != tile_expert[:-1]).astype(I32)])
    expert_ord = jnp.cumsum(new_expert) - 1
    candidates = jnp.where(counts > 0, jnp.arange(ne, dtype=I32), ne)
    later_min = lax.cummin(candidates, axis=0, reverse=True)
    next_used = jnp.concatenate([later_min[1:], jnp.full((1,), ne, I32)])
    next_used = jnp.where(next_used >= ne, -1, next_used)
    expert_next = jnp.sum(jnp.where(tile_expert[:, None] == jnp.arange(ne, dtype=I32)[None, :],
                                    next_used[None, :], 0), axis=1).astype(I32)

    n_slots = n_tiles * tile
    slot_tok = _sc_invert(pos.reshape(-1), n_slots, t)
    xs = _sc_gather(x1p, slot_tok)
    ys = _experts(tile_expert, n_used.reshape(1), expert_ord.astype(I32), expert_next, xs,
                  w_gate_e, w_up_e, w_down_e)
    yg = _sc_gather(ys, pos.T.reshape(-1)).reshape(t, TOP_K * x1p.shape[1])
    out = _combine(wts.T, x1, x1p, yg, w_gate_s.astype(BF16), w_up_s.astype(BF16),
                   w_down_s.astype(BF16), ln2_g.reshape(1, d), ln2_b.reshape(1, d), alpha)
    return out.reshape(bsz, seq, d)


def kernel(x, w_in, b_if, b_gate, conv_qk, pool_w, pool_scale, mh_norm_w, w_b_down, w_out, ln1_g, ln1_b,
           w_router, b_router, w_gate_e, w_up_e, w_down_e, w_gate_s, w_up_s, w_down_s, ln2_g, ln2_b):
    depth = w_in.shape[0]
    alpha = (2.0 * depth) ** 0.25
    for l in range(depth):
        x = _layer(alpha, x, w_in[l], b_if[l], b_gate[l], conv_qk[l], pool_w[l], pool_scale[l], mh_norm_w[l],
                   w_b_down[l], w_out[l], ln1_g[l], ln1_b[l], w_router[l], b_router[l], w_gate_e[l],
                   w_up_e[l], w_down_e[l], w_gate_s[l], w_up_s[l], w_down_s[l], ln2_g[l], ln2_b[l])
    return x
```

```python
import dataclasses
import functools

import jax
import jax.numpy as jnp
import numpy as np
from jax import lax
from jax.experimental import pallas as pl
from jax.experimental.pallas import tpu as pltpu
from jax.experimental.pallas import tpu_sc as plsc

F32 = jnp.float32
BF16 = jnp.bfloat16
I32 = jnp.int32
U32 = jnp.uint32

N_HEADS = 4
POOL_GROUPS = 4
CONV_WIDTH = 5
LN_EPS = 1e-5
N_ROUTE_GROUPS = 8
TOPK_GROUPS = 4
TOP_K = 8
ROUTED_SCALE = 2.5

LANES = 128
SUBLANES = 8
BF16_ROWS = 16
VMEM_LIMIT = 56 * 1024 * 1024

MLSTM_CHUNK = 256
MLSTM_HEADS_PER_STEP = 2
SEQ_TILE = 512
ROW_TILE = 512
ROUTER_TILE = 256
EXPERT_TILE = 256
COMBINE_TILE = 128
SLOT_TILE = 512
WEIGHT_SLOTS = 3

SC_CORES = 2
SC_SUBCORES = 16
SC_LANES = 16
SC_WORKERS = SC_CORES * SC_SUBCORES
SC_GATHER_ROWS = 64
SC_SCAN_CHUNK = 2048

HI_MASK = 0xFFFF0000


def _cparams(sem):
    return pltpu.CompilerParams(dimension_semantics=sem, vmem_limit_bytes=VMEM_LIMIT)


def _sigmoid(x):
    return 1.0 / (1.0 + jnp.exp(-x))


def _layer_norm(y, g, b):
    mu = jnp.mean(y, axis=-1, keepdims=True)
    yc = y - mu
    var = jnp.mean(yc * yc, axis=-1, keepdims=True)
    return yc * lax.rsqrt(var + LN_EPS) * g + b


def _pack_bf16_pairs(y):
    c = y.shape[1] // 2
    bits = lax.bitcast_convert_type(y.astype(BF16).astype(F32), U32)
    return (bits[:, :c] >> 16) | (bits[:, c:] & jnp.uint32(HI_MASK))


def _unpack_bf16_pairs(p):
    lo = lax.bitcast_convert_type(p << 16, F32)
    hi = lax.bitcast_convert_type(p & jnp.uint32(HI_MASK), F32)
    return jnp.concatenate([lo, hi], axis=1)


def _matmul_kernel(x_ref, w_ref, b_ref, o_ref):
    acc = jnp.dot(x_ref[...], w_ref[...], preferred_element_type=F32)
    o_ref[...] = (acc + b_ref[...]).astype(o_ref.dtype)


def _matmul_bias(x, w, bias, out_dtype, tm, tn):
    m, k = x.shape
    n = w.shape[1]
    return pl.pallas_call(
        _matmul_kernel,
        grid=(m // tm, n // tn),
        in_specs=[pl.BlockSpec((tm, k), lambda i, j: (i, 0)),
                  pl.BlockSpec((k, tn), lambda i, j: (0, j)),
                  pl.BlockSpec((1, tn), lambda i, j: (0, j))],
        out_specs=pl.BlockSpec((tm, tn), lambda i, j: (i, j)),
        out_shape=jax.ShapeDtypeStruct((m, n), out_dtype),
        compiler_params=_cparams(("parallel", "parallel")),
        name="inproj",
    )(x, w, bias)


def _qkconv_kernel(prev_ref, main_ref, next_ref, w_ref, o_ref, *, ts, scale, transpose):
    t = pl.program_id(1)
    nt = pl.num_programs(1)
    main = main_ref[0].astype(F32)
    prev = prev_ref[0].astype(F32)[BF16_ROWS - SUBLANES:]
    nxt = next_ref[0].astype(F32)[:SUBLANES]
    prev = jnp.where(t > 0, prev, 0.0)
    nxt = jnp.where(t < nt - 1, nxt, 0.0)
    ext = jnp.concatenate([prev, main, nxt], axis=0)
    w = w_ref[...]
    pad = CONV_WIDTH // 2
    acc = jnp.zeros_like(main)
    for j in range(CONV_WIDTH):
        off = SUBLANES - pad + j
        acc = acc + ext[off:off + ts] * w[j:j + 1]
    y = acc * _sigmoid(acc) * scale
    if transpose:
        o_ref[0] = y.T.astype(o_ref.dtype)
    else:
        o_ref[0] = y.astype(o_ref.dtype)


def _qkconv(proj3, conv_w, col0, scale, transpose):
    bsz, seq, _ = proj3.shape
    ts = SEQ_TILE
    nt = seq // ts
    cw = 2 * LANES
    ncol = conv_w.shape[1] // cw
    hb = ts // BF16_ROWS
    n_hb = seq // BF16_ROWS
    kern = functools.partial(_qkconv_kernel, ts=ts, scale=scale, transpose=transpose)
    if transpose:
        out_shape = jax.ShapeDtypeStruct((bsz, ncol * cw, seq), BF16)
        out_spec = pl.BlockSpec((1, cw, ts), lambda b, t, j: (b, j, t))
    else:
        out_shape = jax.ShapeDtypeStruct((bsz, seq, ncol * cw), BF16)
        out_spec = pl.BlockSpec((1, ts, cw), lambda b, t, j: (b, t, j))
    return pl.pallas_call(
        kern,
        grid=(bsz, nt, ncol),
        in_specs=[pl.BlockSpec((1, BF16_ROWS, cw), lambda b, t, j: (b, jnp.maximum(t * hb - 1, 0), col0 + j)),
                  pl.BlockSpec((1, ts, cw), lambda b, t, j: (b, t, col0 + j)),
                  pl.BlockSpec((1, BF16_ROWS, cw),
                               lambda b, t, j: (b, jnp.minimum((t + 1) * hb, n_hb - 1), col0 + j)),
                  pl.BlockSpec((CONV_WIDTH, cw), lambda b, t, j: (0, j))],
        out_specs=out_spec,
        out_shape=out_shape,
        compiler_params=_cparams(("parallel", "parallel", "parallel")),
        name="qkconv_t" if transpose else "qkconv",
    )(proj3, proj3, proj3, conv_w)


def _pool_kernel(prev_ref, main_ref, next_ref, bm_ref, bp_ref, bn_ref, pw_ref, ps_ref, o_ref, *, ts, seq):
    g = pl.program_id(1)
    t = pl.program_id(2)
    nt = pl.num_programs(2)
    hw = jnp.left_shift(1, g)
    main = main_ref[0]
    prev = jnp.where(t > 0, prev_ref[0], jnp.zeros_like(prev_ref[0]))
    nxt = jnp.where(t < nt - 1, next_ref[0], jnp.zeros_like(next_ref[0]))
    s = (jnp.dot(bm_ref[0], main, preferred_element_type=F32)
         + jnp.dot(bp_ref[0], prev, preferred_element_type=F32)
         + jnp.dot(bn_ref[0], nxt, preferred_element_type=F32))
    tabs = t * ts + lax.broadcasted_iota(I32, (ts, 1), 0)
    cnt = jnp.minimum(tabs + hw, seq) - jnp.maximum(tabs - hw, 0)
    pooled = s / cnt.astype(F32) - main.astype(F32)
    mixed = jnp.dot(pooled.astype(BF16), pw_ref[0], preferred_element_type=F32) * ps_ref[...]
    o_ref[0] = mixed.astype(o_ref.dtype)


def _pool(proj3, pool_w, pool_scale):
    bsz, seq, _ = proj3.shape
    ts = SEQ_TILE
    nt = seq // ts
    cw = pool_w.shape[-1]
    hb = ts // LANES
    n_hb = seq // LANES
    i = np.arange(ts)[:, None]
    hws = [1 << g for g in range(POOL_GROUPS)]
    band_m = np.stack([(np.arange(ts)[None, :] >= i - hw) & (np.arange(ts)[None, :] < i + hw) for hw in hws])
    band_p = np.stack([(np.arange(LANES)[None, :] - LANES >= i - hw) for hw in hws])
    band_n = np.stack([(np.arange(LANES)[None, :] + ts < i + hw) for hw in hws])
    band_m, band_p, band_n = (jnp.asarray(b.astype(np.float32), BF16) for b in (band_m, band_p, band_n))
    kern = functools.partial(_pool_kernel, ts=ts, seq=seq)
    per_group = lambda b, g, t: (g, 0, 0)
    return pl.pallas_call(
        kern,
        grid=(bsz, POOL_GROUPS, nt),
        in_specs=[pl.BlockSpec((1, LANES, cw), lambda b, g, t: (b, jnp.maximum(t * hb - 1, 0), g)),
                  pl.BlockSpec((1, ts, cw), lambda b, g, t: (b, t, g)),
                  pl.BlockSpec((1, LANES, cw), lambda b, g, t: (b, jnp.minimum((t + 1) * hb, n_hb - 1), g)),
                  pl.BlockSpec((1, ts, ts), per_group),
                  pl.BlockSpec((1, ts, LANES), per_group),
                  pl.BlockSpec((1, ts, LANES), per_group),
                  pl.BlockSpec((1, cw, cw), per_group),
                  pl.BlockSpec((1, cw), lambda b, g, t: (0, g))],
        out_specs=pl.BlockSpec((1, ts, cw), lambda b, g, t: (b, t, g)),
        out_shape=jax.ShapeDtypeStruct((bsz, seq, POOL_GROUPS * cw), BF16),
        compiler_params=_cparams(("parallel", "parallel", "parallel")),
        name="pool",
    )(proj3, proj3, proj3, band_m, band_p, band_n, pool_w, pool_scale)


def _dot_split(a, b, a_is_value):
    val = a if a_is_value else b
    hi = val.astype(BF16)
    lo = (val - hi.astype(F32)).astype(BF16)
    if a_is_value:
        return (jnp.dot(hi, b, preferred_element_type=F32) + jnp.dot(lo, b, preferred_element_type=F32))
    return (jnp.dot(a, hi, preferred_element_type=F32) + jnp.dot(a, lo, preferred_element_type=F32))


def _mlstm_kernel(q_ref, kt_ref, v_ref, uo_ref, gr_ref, gc_ref, nw_ref, o_ref,
                  cf_ref, nf_ref, mf_ref, cb_ref, nb_ref, mb_ref, cbs_ref, nbs_ref, mbs_ref,
                  *, chunk, n_chunks, heads):
    L = chunk
    dqk = kt_ref.shape[1] // heads
    dv = v_ref.shape[2] // heads
    p = pl.program_id(1)
    c = pl.program_id(2)
    row = lax.broadcasted_iota(I32, (L, L), 0)
    col = lax.broadcasted_iota(I32, (L, L), 1)
    tri_le = row <= col
    tri_ge = row >= col
    m_le = jnp.where(tri_le, 1.0, 0.0).astype(BF16)
    m_ge = jnp.where(tri_ge, 1.0, 0.0).astype(BF16)
    lane_r = lax.broadcasted_iota(I32, (1, L), 1)
    neg_inf = jnp.float32(-jnp.inf)

    hs = range(heads)
    kts = [kt_ref[0, hh * dqk:(hh + 1) * dqk, :] for hh in hs]
    vs = [v_ref[0, :, hh * dv:(hh + 1) * dv] for hh in hs]
    gates = [gr_ref[0, hh, 0] for hh in hs]
    lf_r = [jax.nn.log_sigmoid(g) for g in gates]

    def update_state(c_ref, n_ref, m_ref, g_r, tot):
        m_prev = [m_ref[hh] for hh in hs]
        m_new = [jnp.maximum(tot[hh] + m_prev[hh], jnp.max(g_r[hh], axis=1, keepdims=True)) for hh in hs]
        decay = [jnp.exp(tot[hh] + m_prev[hh] - m_new[hh]) for hh in hs]
        kw = [kts[hh].astype(F32) * jnp.exp(g_r[hh] - m_new[hh]) for hh in hs]
        upd = [jnp.dot(kw[hh].astype(BF16), vs[hh], preferred_element_type=F32) for hh in hs]
        for hh in hs:
            c_ref[hh] = decay[hh] * c_ref[hh] + upd[hh]
            n_ref[hh] = decay[hh] * n_ref[hh] + jnp.sum(kw[hh], axis=1, keepdims=True)
            m_ref[hh] = m_new[hh]

    @pl.when(p == 0)
    def _backward_states():
        @pl.when(c == 0)
        def _():
            cb_ref[...] = jnp.zeros_like(cb_ref)
            nb_ref[...] = jnp.zeros_like(nb_ref)
            mb_ref[...] = jnp.zeros_like(mb_ref)

        cc = n_chunks - 1 - c
        cbs_ref[cc] = cb_ref[...].astype(BF16)
        nbs_ref[cc] = nb_ref[...]
        mbs_ref[cc] = mb_ref[...]
        a_r = [_dot_split(lf_r[hh], m_ge, True)[3:4] for hh in hs]
        a0 = [jnp.sum(jnp.where(lane_r == 0, a_r[hh], 0.0), axis=1, keepdims=True) for hh in hs]
        g_r = [a0[hh] - a_r[hh] + gates[hh][2:3] for hh in hs]
        update_state(cb_ref, nb_ref, mb_ref, g_r, a0)

    @pl.when(p == 1)
    def _outputs():
        @pl.when(c == 0)
        def _():
            cf_ref[...] = jnp.zeros_like(cf_ref)
            nf_ref[...] = jnp.zeros_like(nf_ref)
            mf_ref[...] = jnp.zeros_like(mf_ref)

        nb_in = nbs_ref[c]
        mb_in = mbs_ref[c]
        cb_in = cbs_ref[c]
        qs = [q_ref[0, :, hh * dqk:(hh + 1) * dqk] for hh in hs]
        lf_c = [jax.nn.log_sigmoid(gc_ref[0, hh]) for hh in hs]
        b_r = [_dot_split(lf_r[hh], m_le, True)[1:2] for hh in hs]
        a_r = [_dot_split(lf_r[hh], m_ge, True)[3:4] for hh in hs]
        b_c = [_dot_split(m_ge, lf_c[hh], False)[:, 1:2] for hh in hs]
        a_c = [_dot_split(m_le, lf_c[hh], False)[:, 3:4] for hh in hs]
        li_f = [gates[hh][0:1] for hh in hs]
        li_b = [gates[hh][2:3] for hh in hs]

        s = [jnp.dot(qs[hh], kts[hh], preferred_element_type=F32) for hh in hs]
        nlane = lax.broadcasted_iota(I32, (dqk, LANES), 1)
        nmat = [jnp.where(nlane == 0, nf_ref[hh], jnp.where(nlane == 1, nb_in[hh], 0.0)).astype(BF16)
                for hh in hs]
        qn = [jnp.dot(qs[hh], nmat[hh], preferred_element_type=F32) for hh in hs]

        def direction(d, mask, cum_c, m_prev, qn_col):
            d = [jnp.where(mask, d[hh], neg_inf) for hh in hs]
            m_inter = [cum_c[hh] + m_prev[hh] for hh in hs]
            m_t = [jnp.maximum(m_inter[hh], jnp.max(d[hh], axis=1, keepdims=True)) for hh in hs]
            pmat = [jnp.exp(d[hh] - m_t[hh]) * s[hh] for hh in hs]
            w_inter = [jnp.exp(m_inter[hh] - m_t[hh]) for hh in hs]
            den = [jnp.sum(pmat[hh], axis=1, keepdims=True) + w_inter[hh] * qn_col[hh] for hh in hs]
            r = [1.0 / jnp.maximum(jnp.abs(den[hh]), jnp.exp(-m_t[hh])) for hh in hs]
            return [pmat[hh] * r[hh] for hh in hs], [w_inter[hh] * r[hh] for hh in hs]

        pf, sf = direction([b_c[hh] - (b_r[hh] - li_f[hh]) for hh in hs], tri_ge, b_c,
                           [mf_ref[hh] for hh in hs], [qn[hh][:, 0:1] for hh in hs])
        pb, sb = direction([a_c[hh] - (a_r[hh] - li_b[hh]) for hh in hs], tri_le, a_c,
                           [mb_in[hh] for hh in hs], [qn[hh][:, 1:2] for hh in hs])
        qf = [qs[hh].astype(F32) for hh in hs]
        h = [jnp.dot((pf[hh] + pb[hh]).astype(BF16), vs[hh], preferred_element_type=F32)
             + jnp.dot((qf[hh] * sf[hh]).astype(BF16), cf_ref[hh].astype(BF16), preferred_element_type=F32)
             + jnp.dot((qf[hh] * sb[hh]).astype(BF16), cb_in[hh], preferred_element_type=F32) for hh in hs]

        mu = [jnp.mean(h[hh], axis=1, keepdims=True) for hh in hs]
        hc = [h[hh] - mu[hh] for hh in hs]
        var = [jnp.mean(hc[hh] * hc[hh], axis=1, keepdims=True) for hh in hs]
        hn = [hc[hh] * lax.rsqrt(var[hh] + LN_EPS) * nw_ref[hh] for hh in hs]
        for hh in hs:
            gate_o = _sigmoid(uo_ref[0, :, hh * dv:(hh + 1) * dv].astype(F32))
            o_ref[0, :, hh * dv:(hh + 1) * dv] = (gate_o * hn[hh]).astype(o_ref.dtype)

        b_last = [jnp.sum(jnp.where(lane_r == L - 1, b_r[hh], 0.0), axis=1, keepdims=True) for hh in hs]
        update_state(cf_ref, nf_ref, mf_ref, [b_last[hh] - b_r[hh] + li_f[hh] for hh in hs], b_last)


def _mlstm(q, kt, proj3, gates_r, gates_c, norm_w, v_off, o_off):
    bsz, seq, qw = q.shape
    dqk = qw // N_HEADS
    dv = norm_w.shape[-1]
    L = MLSTM_CHUNK
    nc = seq // L
    hg = MLSTM_HEADS_PER_STEP
    groups = N_HEADS // hg
    assert N_HEADS % hg == 0 and v_off % (hg * dv) == 0 and o_off % (hg * dv) == 0
    v_blk0 = v_off // (hg * dv)
    o_blk0 = o_off // (hg * dv)
    kern = functools.partial(_mlstm_kernel, chunk=L, n_chunks=nc, heads=hg)

    def chunk_of(p, c):
        return jnp.where(p == 0, nc - 1 - c, c)

    def out_chunk(p, c):
        return jnp.where(p == 0, 0, c)

    return pl.pallas_call(
        kern,
        grid=(bsz * groups, 2, nc),
        in_specs=[
            pl.BlockSpec((1, L, hg * dqk), lambda bg, p, c: (bg // groups, out_chunk(p, c), bg % groups)),
            pl.BlockSpec((1, hg * dqk, L), lambda bg, p, c: (bg // groups, bg % groups, chunk_of(p, c))),
            pl.BlockSpec((1, L, hg * dv), lambda bg, p, c: (bg // groups, chunk_of(p, c), v_blk0 + bg % groups)),
            pl.BlockSpec((1, L, hg * dv), lambda bg, p, c: (bg // groups, out_chunk(p, c), o_blk0 + bg % groups)),
            pl.BlockSpec((1, hg, 1, SUBLANES, L),
                         lambda bg, p, c: (bg // groups, bg % groups, chunk_of(p, c), 0, 0)),
            pl.BlockSpec((1, hg, L, SUBLANES),
                         lambda bg, p, c: (bg // groups, bg % groups, out_chunk(p, c), 0)),
            pl.BlockSpec((hg, 1, dv), lambda bg, p, c: (bg % groups, 0, 0)),
        ],
        out_specs=pl.BlockSpec((1, L, hg * dv), lambda bg, p, c: (bg // groups, out_chunk(p, c), bg % groups)),
        out_shape=jax.ShapeDtypeStruct((bsz, seq, N_HEADS * dv), BF16),
        scratch_shapes=[
            pltpu.VMEM((hg, dqk, dv), F32), pltpu.VMEM((hg, dqk, 1), F32), pltpu.VMEM((hg, 1, 1), F32),
            pltpu.VMEM((hg, dqk, dv), F32), pltpu.VMEM((hg, dqk, 1), F32), pltpu.VMEM((hg, 1, 1), F32),
            pltpu.VMEM((nc, hg, dqk, dv), BF16), pltpu.VMEM((nc, hg, dqk, 1), F32),
            pltpu.VMEM((nc, hg, 1, 1), F32),
        ],
        compiler_params=_cparams(("parallel", "arbitrary", "arbitrary")),
        name="mlstm",
    )(q, kt, proj3, proj3, gates_r, gates_c, norm_w)


def _mixout_kernel(hg_ref, a_ref, uga_ref, ugb_ref, x_ref, wbd_ref, wo_ref, bga_ref, bgb_ref,
                   g_ref, b_ref, o_ref, op_ref, *, alpha):
    branch_b = jnp.dot(hg_ref[...], wbd_ref[...], preferred_element_type=F32)
    ga = _sigmoid(uga_ref[...].astype(F32) + bga_ref[...])
    gb = _sigmoid(ugb_ref[...].astype(F32) + bgb_ref[...])
    merged = ga * a_ref[...].astype(F32) + gb * branch_b
    mix = jnp.dot(merged.astype(BF16), wo_ref[...], preferred_element_type=F32)
    y = _layer_norm(alpha * x_ref[...] + mix, g_ref[...], b_ref[...])
    o_ref[...] = y
    op_ref[...] = _pack_bf16_pairs(y)


def _mixout(hg, branch_a, proj, x, w_b_down, w_out, b_gate, ln_g, ln_b, ga_col, alpha):
    t, d = x.shape
    tm = ROW_TILE
    inner = hg.shape[1]
    row = lambda i: (i, 0)
    const = lambda i: (0, 0)
    return pl.pallas_call(
        functools.partial(_mixout_kernel, alpha=alpha),
        grid=(t // tm,),
        in_specs=[pl.BlockSpec((tm, inner), row),
                  pl.BlockSpec((tm, d), row),
                  pl.BlockSpec((tm, d), lambda i: (i, ga_col)),
                  pl.BlockSpec((tm, d), lambda i: (i, ga_col + 1)),
                  pl.BlockSpec((tm, d), row),
                  pl.BlockSpec((inner, d), const),
                  pl.BlockSpec((d, d), const),
                  pl.BlockSpec((1, d), const),
                  pl.BlockSpec((1, d), lambda i: (0, 1)),
                  pl.BlockSpec((1, d), const),
                  pl.BlockSpec((1, d), const)],
        out_specs=[pl.BlockSpec((tm, d), row), pl.BlockSpec((tm, d // 2), row)],
        out_shape=[jax.ShapeDtypeStruct((t, d), F32), jax.ShapeDtypeStruct((t, d // 2), U32)],
        compiler_params=_cparams(("parallel",)),
        name="mixout",
    )(hg, branch_a, proj, proj, x, w_b_down, w_out, b_gate, b_gate, ln_g, ln_b)


def _router_kernel(x_ref, wh_ref, wl_ref, br_ref, eidx_ref, wts_ref, rank_ref, cnt_ref, run_ref,
                   *, n_experts):
    i = pl.program_id(0)
    tm = x_ref.shape[0]
    ne = n_experts
    per_group = ne // N_ROUTE_GROUPS

    @pl.when(i == 0)
    def _():
        run_ref[...] = jnp.zeros_like(run_ref)

    x = x_ref[...]
    xh = x.astype(BF16)
    xl = (x - xh.astype(F32)).astype(BF16)
    nt_dims = (((1,), (1,)), ((), ()))
    logits = (lax.dot_general(wh_ref[...], xh, nt_dims, preferred_element_type=F32)
              + lax.dot_general(wh_ref[...], xl, nt_dims, preferred_element_type=F32)
              + lax.dot_general(wl_ref[...], xh, nt_dims, preferred_element_type=F32))
    scores = _sigmoid(logits)
    sel = scores + br_ref[...]
    neg = jnp.float32(-jnp.inf)
    big = jnp.float32(1e9)
    rowi = lax.broadcasted_iota(I32, (ne, tm), 0).astype(F32)

    gi = lax.broadcasted_iota(I32, (N_ROUTE_GROUPS, tm), 0).astype(F32)
    work = jnp.zeros((N_ROUTE_GROUPS, tm), F32)
    for g in range(N_ROUTE_GROUPS):
        blk = sel[g * per_group:(g + 1) * per_group]
        ri = lax.broadcasted_iota(I32, (per_group, tm), 0).astype(F32) + float(g * per_group)
        m1 = jnp.max(blk, axis=0, keepdims=True)
        i1 = jnp.min(jnp.where(blk == m1, ri, big), axis=0, keepdims=True)
        m2 = jnp.max(jnp.where(ri == i1, neg, blk), axis=0, keepdims=True)
        work = jnp.where(gi == float(g), m1 + m2, work)
    row_group = jnp.floor(rowi * (1.0 / per_group))
    allowed = jnp.zeros((ne, tm), F32)
    for _ in range(TOPK_GROUPS):
        m = jnp.max(work, axis=0, keepdims=True)
        idx = jnp.min(jnp.where(work == m, gi, big), axis=0, keepdims=True)
        work = jnp.where(gi == idx, neg, work)
        allowed = jnp.where(row_group == idx, 1.0, allowed)
    selm = jnp.where(allowed > 0.5, sel, neg)

    member = jnp.zeros((ne, tm), F32)
    idxs, wks = [], []
    for _ in range(TOP_K):
        m = jnp.max(selm, axis=0, keepdims=True)
        idx = jnp.min(jnp.where(selm == m, rowi, big), axis=0, keepdims=True)
        hit = rowi == idx
        wks.append(jnp.sum(jnp.where(hit, scores, 0.0), axis=0, keepdims=True))
        idxs.append(idx)
        member = jnp.where(hit, 1.0, member)
        selm = jnp.where(hit, neg, selm)
    wsum = wks[0]
    for wk in wks[1:]:
        wsum = wsum + wk

    ti = lax.broadcasted_iota(I32, (tm, tm), 0)
    tj = lax.broadcasted_iota(I32, (tm, tm), 1)
    strict = jnp.where(ti < tj, 1.0, 0.0).astype(BF16)
    prefix = jnp.dot(member.astype(BF16), strict, preferred_element_type=F32) + run_ref[...]
    ranks = [jnp.sum(jnp.where(rowi == idx, prefix, 0.0), axis=0, keepdims=True) for idx in idxs]
    run_new = run_ref[...] + jnp.sum(member, axis=1, keepdims=True)
    run_ref[...] = run_new

    eidx_ref[...] = jnp.concatenate(idxs, axis=0).astype(I32)
    wts_ref[...] = jnp.concatenate([wk / wsum * ROUTED_SCALE for wk in wks], axis=0)
    rank_ref[...] = jnp.concatenate(ranks, axis=0).astype(I32)
    cnt_ref[...] = jnp.broadcast_to(run_new, cnt_ref.shape).astype(I32)


def _router(x1, wr_hi, wr_lo, b_router):
    t, d = x1.shape
    ne = wr_hi.shape[0]
    tm = ROUTER_TILE
    kern = functools.partial(_router_kernel, n_experts=ne)
    tok = lambda i: (0, i)
    const = lambda i: (0, 0)
    return pl.pallas_call(
        kern,
        grid=(t // tm,),
        in_specs=[pl.BlockSpec((tm, d), lambda i: (i, 0)),
                  pl.BlockSpec((ne, d), const),
                  pl.BlockSpec((ne, d), const),
                  pl.BlockSpec((ne, 1), const)],
        out_specs=[pl.BlockSpec((TOP_K, tm), tok), pl.BlockSpec((TOP_K, tm), tok),
                   pl.BlockSpec((TOP_K, tm), tok), pl.BlockSpec((ne, LANES), const)],
        out_shape=[jax.ShapeDtypeStruct((TOP_K, t), I32), jax.ShapeDtypeStruct((TOP_K, t), F32),
                   jax.ShapeDtypeStruct((TOP_K, t), I32), jax.ShapeDtypeStruct((ne, LANES), I32)],
        scratch_shapes=[pltpu.VMEM((ne, 1), F32)],
        compiler_params=_cparams(("arbitrary",)),
        name="router",
    )(x1, wr_hi, wr_lo, b_router)


def _slots_kernel(eidx_ref, rank_ref, ps_ref, pos_ref, *, n_experts):
    tm = eidx_ref.shape[1]
    rowi = lax.broadcasted_iota(I32, (n_experts, tm), 0)
    eidx = eidx_ref[...]
    starts = ps_ref[...]
    base = [jnp.sum(jnp.where(rowi == eidx[k:k + 1], starts, 0.0), axis=0, keepdims=True)
            for k in range(TOP_K)]
    pos_ref[...] = jnp.concatenate(base, axis=0).astype(I32) + rank_ref[...]


def _slots(eidx, rank, pstarts):
    k, t = eidx.shape
    ne = pstarts.shape[0]
    tm = SLOT_TILE
    tok = lambda i: (0, i)
    return pl.pallas_call(
        functools.partial(_slots_kernel, n_experts=ne),
        grid=(t // tm,),
        in_specs=[pl.BlockSpec((k, tm), tok), pl.BlockSpec((k, tm), tok),
                  pl.BlockSpec((ne, 1), lambda i: (0, 0))],
        out_specs=pl.BlockSpec((k, tm), tok),
        out_shape=jax.ShapeDtypeStruct((k, t), I32),
        compiler_params=_cparams(("parallel",)),
        name="slots",
    )(eidx, rank, pstarts.astype(F32).reshape(ne, 1))


def _sc_worker_id():
    return lax.axis_index("subcore") * SC_CORES + lax.axis_index("core")


def _sc_mesh():
    return plsc.VectorSubcoreMesh(core_axis_name="core", subcore_axis_name="subcore")


def _sc_vector_params():
    cp = pltpu.CompilerParams()
    if "needs_layout_passes" in pltpu.CompilerParams.__dataclass_fields__:
        cp = dataclasses.replace(cp, needs_layout_passes=False)
    return cp


def _sc_gather(table, idx):
    n = idx.shape[0]
    w = table.shape[1]
    ch = SC_GATHER_ROWS
    n_ch = n // (SC_WORKERS * ch)
    assert n % (SC_WORKERS * ch * 2) == 0

    @functools.partial(
        pl.kernel, mesh=_sc_mesh(),
        out_type=jax.ShapeDtypeStruct((n, w), table.dtype),
        scratch_types=[pltpu.VMEM((n_ch, ch), I32), pltpu.VMEM((2, ch, w), table.dtype),
                       pltpu.SemaphoreType.DMA((2,))],
    )
    def kern(table_hbm, idx_hbm, out_hbm, idx_v, rows_v, sem):
        first = _sc_worker_id() * n_ch
        pltpu.sync_copy(idx_hbm.at[pl.ds(first, n_ch)], idx_v)

        def gather(j, b):
            return pltpu.make_async_copy(table_hbm.at[idx_v.at[j]], rows_v.at[b], sem.at[b])

        gather(0, 0).start()

        @pl.loop(0, n_ch, step=2)
        def _(j):
            for b in range(2):
                jj = j + b

                @pl.when(jj + 1 < n_ch)
                def _():
                    gather(jj + 1, 1 - b).start()

                gather(jj, b).wait()
                row0 = pl.multiple_of((first + jj) * ch, ch)
                pltpu.sync_copy(rows_v.at[b], out_hbm.at[pl.ds(row0, ch)])

    return kern(table, idx.reshape(n // ch, ch))


def _sc_invert(pos_flat, n_slots, n_tokens):
    n = pos_flat.shape[0]
    per_w = n_slots // SC_WORKERS
    chunk = SC_SCAN_CHUNK
    assert n_slots % (SC_WORKERS * SC_LANES) == 0 and n % chunk == 0 and n_tokens % chunk == 0

    @functools.partial(
        pl.kernel, mesh=_sc_mesh(),
        out_type=jax.ShapeDtypeStruct((n_slots,), I32),
        scratch_types=[pltpu.VMEM((per_w,), I32), pltpu.VMEM((chunk,), I32)],
        compiler_params=_sc_vector_params(),
    )
    def kern(pos_hbm, out_hbm, table_v, pos_v):
        lo = _sc_worker_id() * per_w
        lane = lax.iota(I32, SC_LANES)

        @pl.loop(0, per_w // SC_LANES)
        def _(i):
            off = pl.multiple_of(i * SC_LANES, SC_LANES)
            table_v[pl.ds(off, SC_LANES)] = lax.rem(lo + off + lane, jnp.full((SC_LANES,), n_tokens, I32))

        @pl.loop(0, n // chunk)
        def _(c):
            pltpu.sync_copy(pos_hbm.at[pl.ds(pl.multiple_of(c * chunk, chunk), chunk)], pos_v)
            tok0 = lax.rem(c, n_tokens // chunk) * chunk

            @pl.loop(0, chunk // SC_LANES)
            def _(j):
                off = pl.multiple_of(j * SC_LANES, SC_LANES)
                local = pos_v[pl.ds(off, SC_LANES)] - lo
                mine = (local >= 0) & (local < per_w)
                plsc.store_scatter(table_v, [jnp.where(mine, local, 0)], tok0 + off + lane, mask=mine)

        pltpu.sync_copy(table_v, out_hbm.at[pl.ds(pl.multiple_of(lo, SC_LANES), per_w)])

    return kern(pos_flat)


def _experts_kernel(te_ref, nu_ref, ord_ref, nxt_ref, nxt2_ref, xs_ref, wg_hbm, wu_hbm, wd_hbm, ys_ref,
                    wgf_ref, wuf_ref, wdf_ref, wgb_ref, wub_ref, wdb_ref, sem_ref):
    i = pl.program_id(0)

    def weight_copies(e, slot):
        return (pltpu.make_async_copy(wg_hbm.at[e], wgf_ref.at[slot], sem_ref.at[slot]),
                pltpu.make_async_copy(wu_hbm.at[e], wuf_ref.at[slot], sem_ref.at[slot]),
                pltpu.make_async_copy(wd_hbm.at[e], wdf_ref.at[slot], sem_ref.at[slot]))

    @pl.when(i < nu_ref[0])
    def _():
        e = te_ref[i]
        ordinal = ord_ref[i]
        slot = lax.rem(ordinal, WEIGHT_SLOTS)
        first_tile_of_expert = jnp.logical_or(i == 0, e != te_ref[jnp.maximum(i - 1, 0)])

        @pl.when(i == 0)
        def _():
            for cp in weight_copies(e, slot):
                cp.start()

            @pl.when(nxt_ref[i] >= 0)
            def _():
                for cp in weight_copies(nxt_ref[i], lax.rem(ordinal + 1, WEIGHT_SLOTS)):
                    cp.start()

        @pl.when(first_tile_of_expert)
        def _():
            ahead = nxt2_ref[i]

            @pl.when(ahead >= 0)
            def _():
                for cp in weight_copies(ahead, lax.rem(ordinal + 2, WEIGHT_SLOTS)):
                    cp.start()

            for cp in weight_copies(e, slot):
                cp.wait()
            wgb_ref[...] = wgf_ref[slot].astype(BF16)
            wub_ref[...] = wuf_ref[slot].astype(BF16)
            wdb_ref[...] = wdf_ref[slot].astype(BF16)

        x = _unpack_bf16_pairs(xs_ref[...]).astype(BF16)
        gate = jnp.dot(x, wgb_ref[...], preferred_element_type=F32)
        up = jnp.dot(x, wub_ref[...], preferred_element_type=F32)
        hid = (gate * _sigmoid(gate) * up).astype(BF16)
        y = jnp.dot(hid, wdb_ref[...], preferred_element_type=F32)
        ys_ref[...] = _pack_bf16_pairs(y)

    @pl.when(i >= nu_ref[0])
    def _():
        ys_ref[...] = jnp.zeros_like(ys_ref)


def _experts(tile_expert, n_used, expert_ord, expert_next, expert_next2, xs, w_gate_e, w_up_e, w_down_e):
    n_slots, w = xs.shape
    tile = EXPERT_TILE
    n_tiles = n_slots // tile
    _, d, de = w_gate_e.shape
    grid_spec = pltpu.PrefetchScalarGridSpec(
        num_scalar_prefetch=5,
        grid=(n_tiles,),
        in_specs=[pl.BlockSpec((tile, w), lambda i, te, nu, *_: (jnp.minimum(i, nu[0] - 1), 0)),
                  pl.BlockSpec(memory_space=pl.ANY),
                  pl.BlockSpec(memory_space=pl.ANY),
                  pl.BlockSpec(memory_space=pl.ANY)],
        out_specs=pl.BlockSpec((tile, w), lambda i, *_: (i, 0)),
        scratch_shapes=[pltpu.VMEM((WEIGHT_SLOTS, d, de), F32), pltpu.VMEM((WEIGHT_SLOTS, d, de), F32),
                        pltpu.VMEM((WEIGHT_SLOTS, de, d), F32),
                        pltpu.VMEM((d, de), BF16), pltpu.VMEM((d, de), BF16), pltpu.VMEM((de, d), BF16),
                        pltpu.SemaphoreType.DMA((WEIGHT_SLOTS,))],
    )
    return pl.pallas_call(
        _experts_kernel,
        grid_spec=grid_spec,
        out_shape=jax.ShapeDtypeStruct((n_slots, w), U32),
        compiler_params=_cparams(("arbitrary",)),
        name="experts",
    )(tile_expert, n_used, expert_ord, expert_next, expert_next2, xs, w_gate_e, w_up_e, w_down_e)


def _combine_kernel(wts_ref, x_ref, xp_ref, yg_ref, wgs_ref, wus_ref, wds_ref, g_ref, b_ref, o_ref, *, alpha):
    wts = wts_ref[...]
    routed = jnp.zeros(x_ref.shape, F32)
    for k in range(TOP_K):
        routed = routed + wts[:, k:k + 1] * _unpack_bf16_pairs(yg_ref[k])
    xb = _unpack_bf16_pairs(xp_ref[...]).astype(BF16)
    gate = jnp.dot(xb, wgs_ref[...], preferred_element_type=F32)
    up = jnp.dot(xb, wus_ref[...], preferred_element_type=F32)
    hid = (gate * _sigmoid(gate) * up).astype(BF16)
    shared = jnp.dot(hid, wds_ref[...], preferred_element_type=F32)
    o_ref[...] = _layer_norm(alpha * x_ref[...] + (routed + shared), g_ref[...], b_ref[...])


def _combine(wts_c, x1, x1p, yg, w_gate_s, w_up_s, w_down_s, ln_g, ln_b, alpha):
    t, d = x1.shape
    w = x1p.shape[1]
    tb = COMBINE_TILE
    ds = w_gate_s.shape[1]
    row = lambda i: (i, 0)
    const = lambda i: (0, 0)
    return pl.pallas_call(
        functools.partial(_combine_kernel, alpha=alpha),
        grid=(t // tb,),
        in_specs=[pl.BlockSpec((tb, TOP_K), row),
                  pl.BlockSpec((tb, d), row),
                  pl.BlockSpec((tb, w), row),
                  pl.BlockSpec((TOP_K, tb, w), lambda i: (0, i, 0)),
                  pl.BlockSpec((d, ds), const),
                  pl.BlockSpec((d, ds), const),
                  pl.BlockSpec((ds, d), const),
                  pl.BlockSpec((1, d), const),
                  pl.BlockSpec((1, d), const)],
        out_specs=pl.BlockSpec((tb, d), row),
        out_shape=jax.ShapeDtypeStruct((t, d), F32),
        compiler_params=_cparams(("parallel",)),
        name="combine",
    )(wts_c, x1, x1p, yg, w_gate_s, w_up_s, w_down_s, ln_g, ln_b)


def _layer(alpha, x, w_in, b_if, b_gate, conv_qk, pool_w, pool_scale, mh_norm_w, w_b_down, w_out,
           ln1_g, ln1_b, w_router, b_router, w_gate_e, w_up_e, w_down_e,
           w_gate_s, w_up_s, w_down_s, ln2_g, ln2_b):
    bsz, seq, d = x.shape
    t = bsz * seq
    heads = N_HEADS
    pool_width = pool_w.shape[0] * pool_w.shape[1]
    qk_cols = conv_qk.shape[1]
    v_cols = mh_norm_w.shape[0] * mh_norm_w.shape[1]
    o_cols = v_cols
    if_cols = b_if.shape[0]
    dv = mh_norm_w.shape[1]
    dqk = qk_cols // (2 * heads)
    off_if = pool_width + qk_cols + v_cols + o_cols
    off_gate = off_if + if_cols

    xf = x.reshape(t, d)
    xb = xf.astype(BF16)
    w_main = jnp.concatenate([w_in[:, :off_if], w_in[:, off_gate:]], axis=1).astype(BF16)
    w_if = jnp.pad(w_in[:, off_if:off_gate], ((0, 0), (0, LANES - if_cols))).astype(BF16)
    bias_if = jnp.pad(b_if, (0, LANES - if_cols)).reshape(1, LANES)
    n_main = w_main.shape[1]

    proj = _matmul_bias(xb, w_main, jnp.zeros((1, n_main), F32), BF16, 1024, 1024)
    u_if = _matmul_bias(xb, w_if, bias_if, F32, 1024, LANES)
    proj3 = proj.reshape(bsz, seq, n_main)

    branch_a = _pool(proj3, pool_w.astype(BF16), pool_scale.reshape(1, pool_width))

    cw = 2 * LANES
    q_col0 = pool_width // cw
    half = qk_cols // 2
    q = _qkconv(proj3, conv_qk[:, :half], q_col0, float(dqk) ** -0.5, False)
    kt = _qkconv(proj3, conv_qk[:, half:], q_col0 + half // cw, 1.0, True)

    nc = seq // MLSTM_CHUNK
    gates = u_if[:, :if_cols].reshape(bsz, seq, 4, heads).transpose(0, 3, 2, 1)
    gates = jnp.pad(gates, ((0, 0), (0, 0), (0, SUBLANES - 4), (0, 0)))
    gates_c = gates.transpose(0, 1, 3, 2)
    gates_r = gates.reshape(bsz, heads, SUBLANES, nc, MLSTM_CHUNK).transpose(0, 1, 3, 2, 4)
    hg = _mlstm(q, kt, proj3, gates_r, gates_c, mh_norm_w.reshape(heads, 1, dv),
                pool_width + qk_cols, pool_width + qk_cols + v_cols)

    ga_col = (pool_width + qk_cols + v_cols + o_cols) // d
    x1, x1p = _mixout(hg.reshape(t, heads * dv), branch_a.reshape(t, pool_width), proj, xf,
                      w_b_down.astype(BF16), w_out.astype(BF16), b_gate.reshape(1, 2 * d),
                      ln1_g.reshape(1, d), ln1_b.reshape(1, d), ga_col, alpha)

    ne = w_router.shape[1]
    wr_t = w_router.T
    wr_hi = wr_t.astype(BF16)
    wr_lo = (wr_t - wr_hi.astype(F32)).astype(BF16)
    eidx, wts, rank, cnt = _router(x1, wr_hi, wr_lo, b_router.reshape(ne, 1))

    tile = EXPERT_TILE
    n_tiles = (t * TOP_K) // tile + ne
    counts = cnt[:, 0]
    pcounts = ((counts + tile - 1) // tile) * tile
    pends = jnp.cumsum(pcounts)
    pstarts = pends - pcounts
    pos = _slots(eidx, rank, pstarts)

    n_used = (pends[-1] // tile).astype(I32)
    tile_ids = jnp.minimum(jnp.arange(n_tiles, dtype=I32), n_used - 1)
    tile_expert = jnp.sum((pends[None, :] <= (tile_ids * tile)[:, None]).astype(I32), axis=1)
    tile_expert = jnp.minimum(tile_expert, ne - 1)
    new_expert = jnp.concatenate([jnp.ones((1,), I32), (tile_expert[1:] != tile_expert[:-1]).astype(I32)])
    expert_ord = jnp.cumsum(new_expert) - 1
    candidates = jnp.where(counts > 0, jnp.arange(ne, dtype=I32), ne)
    later_min = lax.cummin(candidates, axis=0, reverse=True)
    next_used = jnp.concatenate([later_min[1:], jnp.full((1,), ne, I32)])
    next_used = jnp.where(next_used >= ne, -1, next_used)
    next_used2 = jnp.where(next_used >= 0, next_used[jnp.maximum(next_used, 0)], -1)
    tile_onehot = tile_expert[:, None] == jnp.arange(ne, dtype=I32)[None, :]
    expert_next = jnp.sum(jnp.where(tile_onehot, next_used[None, :], 0), axis=1).astype(I32)
    expert_next2 = jnp.sum(jnp.where(tile_onehot, next_used2[None, :], 0), axis=1).astype(I32)

    n_slots = n_tiles * tile
    slot_tok = _sc_invert(pos.reshape(-1), n_slots, t)
    xs = _sc_gather(x1p, slot_tok)
    ys = _experts(tile_expert, n_used.reshape(1), expert_ord.astype(I32), expert_next, expert_next2, xs,
                  w_gate_e, w_up_e, w_down_e)
    yg = _sc_gather(ys, pos.reshape(-1)).reshape(TOP_K, t, x1p.shape[1])
    out = _combine(wts.T, x1, x1p, yg, w_gate_s.astype(BF16), w_up_s.astype(BF16),
                   w_down_s.astype(BF16), ln2_g.reshape(1, d), ln2_b.reshape(1, d), alpha)
    return out.reshape(bsz, seq, d)


def kernel(x, w_in, b_if, b_gate, conv_qk, pool_w, pool_scale, mh_norm_w, w_b_down, w_out, ln1_g, ln1_b,
           w_router, b_router, w_gate_e, w_up_e, w_down_e, w_gate_s, w_up_s, w_down_s, ln2_g, ln2_b):
    depth = w_in.shape[0]
    alpha = (2.0 * depth) ** 0.25
    for l in range(depth):
        x = _layer(alpha, x, w_in[l], b_if[l], b_gate[l], conv_qk[l], pool_w[l], pool_scale[l], mh_norm_w[l],
                   w_b_down[l], w_out[l], ln1_g[l], ln1_b[l], w_router[l], b_router[l], w_gate_e[l],
                   w_up_e[l], w_down_e[l], w_gate_s[l], w_up_s[l], w_down_s[l], ln2_g[l], ln2_b[l])
    return x
```

```python
import dataclasses
import functools

import jax
import jax.numpy as jnp
import numpy as np
from jax import lax
from jax.experimental import pallas as pl
from jax.experimental.pallas import tpu as pltpu
from jax.experimental.pallas import tpu_sc as plsc

F32 = jnp.float32
BF16 = jnp.bfloat16
I32 = jnp.int32
U32 = jnp.uint32

N_HEADS = 4
POOL_GROUPS = 4
CONV_WIDTH = 5
LN_EPS = 1e-5
N_ROUTE_GROUPS = 8
TOPK_GROUPS = 4
TOP_K = 8
ROUTED_SCALE = 2.5

LANES = 128
SUBLANES = 8
BF16_ROWS = 16
VMEM_LIMIT = 56 * 1024 * 1024

INPROJ_TILE_M = 2048
INPROJ_TILE_N = 1536
MLSTM_CHUNK = 256
MLSTM_HEADS_PER_STEP = 2
MLSTM_BATCH_PER_STEP = 2
SEQ_TILE = 1024
ROW_TILE = 1024
ROUTER_TILE = 512
EXPERT_TILE = 256
COMBINE_TILE = 512
SLOT_TILE = 512
WEIGHT_SLOTS = 3
WEIGHT_DMA_PRIORITY = 0
ROW_SLOTS = 3

SC_CORES = 2
SC_SUBCORES = 16
SC_LANES = 16
SC_WORKERS = SC_CORES * SC_SUBCORES
SC_GATHER_ROWS = 64
SC_SCAN_CHUNK = 2048
SC_SCAN_UNROLL = 8

HI_MASK = 0xFFFF0000


def _cparams(sem):
    return pltpu.CompilerParams(dimension_semantics=sem, vmem_limit_bytes=VMEM_LIMIT)


def _sigmoid(x):
    return 1.0 / (1.0 + jnp.exp(-x))


def _layer_norm(y, g, b):
    mu = jnp.mean(y, axis=-1, keepdims=True)
    yc = y - mu
    var = jnp.mean(yc * yc, axis=-1, keepdims=True)
    return yc * lax.rsqrt(var + LN_EPS) * g + b


def _pack_bf16_pairs(y):
    c = y.shape[1] // 2
    bits = lax.bitcast_convert_type(y.astype(BF16).astype(F32), U32)
    return (bits[:, :c] >> 16) | (bits[:, c:] & jnp.uint32(HI_MASK))


def _unpack_bf16_pairs(p):
    lo = lax.bitcast_convert_type(p << 16, F32)
    hi = lax.bitcast_convert_type(p & jnp.uint32(HI_MASK), F32)
    return jnp.concatenate([lo, hi], axis=1)


def _matmul_kernel(x_ref, w_ref, b_ref, o_ref):
    acc = jnp.dot(x_ref[...], w_ref[...], preferred_element_type=F32)
    o_ref[...] = (acc + b_ref[...]).astype(o_ref.dtype)


def _matmul_bias(x, w, bias, out_dtype, tm, tn):
    m, k = x.shape
    n = w.shape[1]
    return pl.pallas_call(
        _matmul_kernel,
        grid=(m // tm, n // tn),
        in_specs=[pl.BlockSpec((tm, k), lambda i, j: (i, 0)),
                  pl.BlockSpec((k, tn), lambda i, j: (0, j)),
                  pl.BlockSpec((1, tn), lambda i, j: (0, j))],
        out_specs=pl.BlockSpec((tm, tn), lambda i, j: (i, j)),
        out_shape=jax.ShapeDtypeStruct((m, n), out_dtype),
        compiler_params=_cparams(("parallel", "parallel")),
        name="inproj",
    )(x, w, bias)


def _qkconv_kernel(prev_ref, main_ref, next_ref, w_ref, o_ref, *, ts, scale, transpose):
    t = pl.program_id(1)
    nt = pl.num_programs(1)
    main = main_ref[0].astype(F32)
    prev = prev_ref[0].astype(F32)[BF16_ROWS - SUBLANES:]
    nxt = next_ref[0].astype(F32)[:SUBLANES]
    prev = jnp.where(t > 0, prev, 0.0)
    nxt = jnp.where(t < nt - 1, nxt, 0.0)
    ext = jnp.concatenate([prev, main, nxt], axis=0)
    w = w_ref[...]
    pad = CONV_WIDTH // 2
    acc = jnp.zeros_like(main)
    for j in range(CONV_WIDTH):
        off = SUBLANES - pad + j
        acc = acc + ext[off:off + ts] * w[j:j + 1]
    y = acc * _sigmoid(acc) * scale
    if transpose:
        o_ref[0] = y.T.astype(o_ref.dtype)
    else:
        o_ref[0] = y.astype(o_ref.dtype)


def _qkconv(proj3, conv_w, col0, scale, transpose):
    bsz, seq, _ = proj3.shape
    ts = SEQ_TILE
    nt = seq // ts
    cw = 2 * LANES
    ncol = conv_w.shape[1] // cw
    hb = ts // BF16_ROWS
    n_hb = seq // BF16_ROWS
    kern = functools.partial(_qkconv_kernel, ts=ts, scale=scale, transpose=transpose)
    if transpose:
        out_shape = jax.ShapeDtypeStruct((bsz, ncol * cw, seq), BF16)
        out_spec = pl.BlockSpec((1, cw, ts), lambda b, t, j: (b, j, t))
    else:
        out_shape = jax.ShapeDtypeStruct((bsz, seq, ncol * cw), BF16)
        out_spec = pl.BlockSpec((1, ts, cw), lambda b, t, j: (b, t, j))
    return pl.pallas_call(
        kern,
        grid=(bsz, nt, ncol),
        in_specs=[pl.BlockSpec((1, BF16_ROWS, cw), lambda b, t, j: (b, jnp.maximum(t * hb - 1, 0), col0 + j)),
                  pl.BlockSpec((1, ts, cw), lambda b, t, j: (b, t, col0 + j)),
                  pl.BlockSpec((1, BF16_ROWS, cw),
                               lambda b, t, j: (b, jnp.minimum((t + 1) * hb, n_hb - 1), col0 + j)),
                  pl.BlockSpec((CONV_WIDTH, cw), lambda b, t, j: (0, j))],
        out_specs=out_spec,
        out_shape=out_shape,
        compiler_params=_cparams(("parallel", "parallel", "parallel")),
        name="qkconv_t" if transpose else "qkconv",
    )(proj3, proj3, proj3, conv_w)


def _pool_kernel(prev_ref, main_ref, next_ref, bm_ref, bp_ref, bn_ref, pw_ref, ps_ref, o_ref, *, ts, seq):
    g = pl.program_id(1)
    t = pl.program_id(2)
    nt = pl.num_programs(2)
    hw = jnp.left_shift(1, g)
    main = main_ref[0]
    prev = jnp.where(t > 0, prev_ref[0], jnp.zeros_like(prev_ref[0]))
    nxt = jnp.where(t < nt - 1, next_ref[0], jnp.zeros_like(next_ref[0]))
    s = (jnp.dot(bm_ref[0], main, preferred_element_type=F32)
         + jnp.dot(bp_ref[0], prev, preferred_element_type=F32)
         + jnp.dot(bn_ref[0], nxt, preferred_element_type=F32))
    tabs = t * ts + lax.broadcasted_iota(I32, (ts, 1), 0)
    cnt = jnp.minimum(tabs + hw, seq) - jnp.maximum(tabs - hw, 0)
    pooled = s / cnt.astype(F32) - main.astype(F32)
    mixed = jnp.dot(pooled.astype(BF16), pw_ref[0], preferred_element_type=F32) * ps_ref[...]
    o_ref[0] = mixed.astype(o_ref.dtype)


def _pool(proj3, pool_w, pool_scale):
    bsz, seq, _ = proj3.shape
    ts = SEQ_TILE
    nt = seq // ts
    cw = pool_w.shape[-1]
    hb = ts // LANES
    n_hb = seq // LANES
    i = np.arange(ts)[:, None]
    hws = [1 << g for g in range(POOL_GROUPS)]
    band_m = np.stack([(np.arange(ts)[None, :] >= i - hw) & (np.arange(ts)[None, :] < i + hw) for hw in hws])
    band_p = np.stack([(np.arange(LANES)[None, :] - LANES >= i - hw) for hw in hws])
    band_n = np.stack([(np.arange(LANES)[None, :] + ts < i + hw) for hw in hws])
    band_m, band_p, band_n = (jnp.asarray(b.astype(np.float32), BF16) for b in (band_m, band_p, band_n))
    kern = functools.partial(_pool_kernel, ts=ts, seq=seq)
    per_group = lambda b, g, t: (g, 0, 0)
    return pl.pallas_call(
        kern,
        grid=(bsz, POOL_GROUPS, nt),
        in_specs=[pl.BlockSpec((1, LANES, cw), lambda b, g, t: (b, jnp.maximum(t * hb - 1, 0), g)),
                  pl.BlockSpec((1, ts, cw), lambda b, g, t: (b, t, g)),
                  pl.BlockSpec((1, LANES, cw), lambda b, g, t: (b, jnp.minimum((t + 1) * hb, n_hb - 1), g)),
                  pl.BlockSpec((1, ts, ts), per_group),
                  pl.BlockSpec((1, ts, LANES), per_group),
                  pl.BlockSpec((1, ts, LANES), per_group),
                  pl.BlockSpec((1, cw, cw), per_group),
                  pl.BlockSpec((1, cw), lambda b, g, t: (0, g))],
        out_specs=pl.BlockSpec((1, ts, cw), lambda b, g, t: (b, t, g)),
        out_shape=jax.ShapeDtypeStruct((bsz, seq, POOL_GROUPS * cw), BF16),
        compiler_params=_cparams(("parallel", "parallel", "parallel")),
        name="pool",
    )(proj3, proj3, proj3, band_m, band_p, band_n, pool_w, pool_scale)


def _dot_split(a, b, a_is_value):
    val = a if a_is_value else b
    hi = val.astype(BF16)
    lo = (val - hi.astype(F32)).astype(BF16)
    if a_is_value:
        return (jnp.dot(hi, b, preferred_element_type=F32) + jnp.dot(lo, b, preferred_element_type=F32))
    return (jnp.dot(a, hi, preferred_element_type=F32) + jnp.dot(a, lo, preferred_element_type=F32))


def _mlstm_kernel(q_ref, kt_ref, v_ref, uo_ref, gr_ref, gc_ref, nw_ref, o_ref,
                  cf_ref, nf_ref, mf_ref, cb_ref, nb_ref, mb_ref, cbs_ref, nbs_ref, mbs_ref,
                  *, chunk, n_chunks, heads):
    L = chunk
    nb = q_ref.shape[0]
    dqk = kt_ref.shape[1] // heads
    dv = v_ref.shape[2] // heads
    p = pl.program_id(1)
    c = pl.program_id(2)
    row = lax.broadcasted_iota(I32, (L, L), 0)
    col = lax.broadcasted_iota(I32, (L, L), 1)
    tri_le = row <= col
    tri_ge = row >= col
    m_le = jnp.where(tri_le, 1.0, 0.0).astype(BF16)
    m_ge = jnp.where(tri_ge, 1.0, 0.0).astype(BF16)
    lane_r = lax.broadcasted_iota(I32, (1, L), 1)
    neg_inf = jnp.float32(-jnp.inf)

    chains = [(bb, hd) for bb in range(nb) for hd in range(heads)]
    hs = range(len(chains))
    kts = [kt_ref[bb, hd * dqk:(hd + 1) * dqk, :] for bb, hd in chains]
    vs = [v_ref[bb, :, hd * dv:(hd + 1) * dv] for bb, hd in chains]
    gates = [gr_ref[bb, hd, 0] for bb, hd in chains]
    lf_r = [jax.nn.log_sigmoid(g) for g in gates]

    def update_state(c_ref, n_ref, m_ref, g_r, tot):
        m_prev = [m_ref[hh] for hh in hs]
        m_new = [jnp.maximum(tot[hh] + m_prev[hh], jnp.max(g_r[hh], axis=1, keepdims=True)) for hh in hs]
        decay = [jnp.exp(tot[hh] + m_prev[hh] - m_new[hh]) for hh in hs]
        kw = [kts[hh].astype(F32) * jnp.exp(g_r[hh] - m_new[hh]) for hh in hs]
        upd = [jnp.dot(kw[hh].astype(BF16), vs[hh], preferred_element_type=F32) for hh in hs]
        for hh in hs:
            c_ref[hh] = decay[hh] * c_ref[hh] + upd[hh]
            n_ref[hh] = decay[hh] * n_ref[hh] + jnp.sum(kw[hh], axis=1, keepdims=True)
            m_ref[hh] = m_new[hh]

    @pl.when(p == 0)
    def _backward_states():
        @pl.when(c == 0)
        def _():
            cb_ref[...] = jnp.zeros_like(cb_ref)
            nb_ref[...] = jnp.zeros_like(nb_ref)
            mb_ref[...] = jnp.zeros_like(mb_ref)

        cc = n_chunks - 1 - c
        cbs_ref[cc] = cb_ref[...].astype(BF16)
        nbs_ref[cc] = nb_ref[...]
        mbs_ref[cc] = mb_ref[...]
        a_r = [_dot_split(lf_r[hh], m_ge, True)[3:4] for hh in hs]
        a0 = [jnp.sum(jnp.where(lane_r == 0, a_r[hh], 0.0), axis=1, keepdims=True) for hh in hs]
        g_r = [a0[hh] - a_r[hh] + gates[hh][2:3] for hh in hs]
        update_state(cb_ref, nb_ref, mb_ref, g_r, a0)

    @pl.when(p == 1)
    def _outputs():
        @pl.when(c == 0)
        def _():
            cf_ref[...] = jnp.zeros_like(cf_ref)
            nf_ref[...] = jnp.zeros_like(nf_ref)
            mf_ref[...] = jnp.zeros_like(mf_ref)

        nb_in = nbs_ref[c]
        mb_in = mbs_ref[c]
        cb_in = cbs_ref[c]
        qs = [q_ref[bb, :, hd * dqk:(hd + 1) * dqk] for bb, hd in chains]
        lf_c = [jax.nn.log_sigmoid(gc_ref[bb, hd]) for bb, hd in chains]
        b_r = [_dot_split(lf_r[hh], m_le, True)[1:2] for hh in hs]
        a_r = [_dot_split(lf_r[hh], m_ge, True)[3:4] for hh in hs]
        b_c = [_dot_split(m_ge, lf_c[hh], False)[:, 1:2] for hh in hs]
        a_c = [_dot_split(m_le, lf_c[hh], False)[:, 3:4] for hh in hs]
        li_f = [gates[hh][0:1] for hh in hs]
        li_b = [gates[hh][2:3] for hh in hs]

        s = [jnp.dot(qs[hh], kts[hh], preferred_element_type=F32) for hh in hs]
        nlane = lax.broadcasted_iota(I32, (dqk, LANES), 1)
        nmat = [jnp.where(nlane == 0, nf_ref[hh], jnp.where(nlane == 1, nb_in[hh], 0.0)).astype(BF16)
                for hh in hs]
        qn = [jnp.dot(qs[hh], nmat[hh], preferred_element_type=F32) for hh in hs]

        def direction(d, mask, cum_c, m_prev, qn_col):
            d = [jnp.where(mask, d[hh], neg_inf) for hh in hs]
            m_inter = [cum_c[hh] + m_prev[hh] for hh in hs]
            m_t = [jnp.maximum(m_inter[hh], jnp.max(d[hh], axis=1, keepdims=True)) for hh in hs]
            pmat = [jnp.exp(d[hh] - m_t[hh]) * s[hh] for hh in hs]
            w_inter = [jnp.exp(m_inter[hh] - m_t[hh]) for hh in hs]
            den = [jnp.sum(pmat[hh], axis=1, keepdims=True) + w_inter[hh] * qn_col[hh] for hh in hs]
            r = [1.0 / jnp.maximum(jnp.abs(den[hh]), jnp.exp(-m_t[hh])) for hh in hs]
            return [pmat[hh] * r[hh] for hh in hs], [w_inter[hh] * r[hh] for hh in hs]

        pf, sf = direction([b_c[hh] - (b_r[hh] - li_f[hh]) for hh in hs], tri_ge, b_c,
                           [mf_ref[hh] for hh in hs], [qn[hh][:, 0:1] for hh in hs])
        pb, sb = direction([a_c[hh] - (a_r[hh] - li_b[hh]) for hh in hs], tri_le, a_c,
                           [mb_in[hh] for hh in hs], [qn[hh][:, 1:2] for hh in hs])
        qf = [qs[hh].astype(F32) for hh in hs]
        h = [jnp.dot((pf[hh] + pb[hh]).astype(BF16), vs[hh], preferred_element_type=F32)
             + jnp.dot((qf[hh] * sf[hh]).astype(BF16), cf_ref[hh].astype(BF16), preferred_element_type=F32)
             + jnp.dot((qf[hh] * sb[hh]).astype(BF16), cb_in[hh], preferred_element_type=F32) for hh in hs]

        mu = [jnp.mean(h[hh], axis=1, keepdims=True) for hh in hs]
        hc = [h[hh] - mu[hh] for hh in hs]
        var = [jnp.mean(hc[hh] * hc[hh], axis=1, keepdims=True) for hh in hs]
        hn = [hc[hh] * lax.rsqrt(var[hh] + LN_EPS) * nw_ref[chains[hh][1]] for hh in hs]
        for hh, (bb, hd) in enumerate(chains):
            gate_o = _sigmoid(uo_ref[bb, :, hd * dv:(hd + 1) * dv].astype(F32))
            o_ref[bb, :, hd * dv:(hd + 1) * dv] = (gate_o * hn[hh]).astype(o_ref.dtype)

        b_last = [jnp.sum(jnp.where(lane_r == L - 1, b_r[hh], 0.0), axis=1, keepdims=True) for hh in hs]
        update_state(cf_ref, nf_ref, mf_ref, [b_last[hh] - b_r[hh] + li_f[hh] for hh in hs], b_last)


def _mlstm(q, kt, proj3, gates_r, gates_c, norm_w, v_off, o_off):
    bsz, seq, qw = q.shape
    dqk = qw // N_HEADS
    dv = norm_w.shape[-1]
    L = MLSTM_CHUNK
    nc = seq // L
    hg = MLSTM_HEADS_PER_STEP
    bg = MLSTM_BATCH_PER_STEP
    groups = N_HEADS // hg
    assert N_HEADS % hg == 0 and bsz % bg == 0 and v_off % (hg * dv) == 0 and o_off % (hg * dv) == 0
    v_blk0 = v_off // (hg * dv)
    o_blk0 = o_off // (hg * dv)
    kern = functools.partial(_mlstm_kernel, chunk=L, n_chunks=nc, heads=hg)

    def chunk_of(p, c):
        return jnp.where(p == 0, nc - 1 - c, c)

    def out_chunk(p, c):
        return jnp.where(p == 0, 0, c)

    return pl.pallas_call(
        kern,
        grid=((bsz // bg) * groups, 2, nc),
        in_specs=[
            pl.BlockSpec((bg, L, hg * dqk), lambda g, p, c: (g // groups, out_chunk(p, c), g % groups)),
            pl.BlockSpec((bg, hg * dqk, L), lambda g, p, c: (g // groups, g % groups, chunk_of(p, c))),
            pl.BlockSpec((bg, L, hg * dv), lambda g, p, c: (g // groups, chunk_of(p, c), v_blk0 + g % groups)),
            pl.BlockSpec((bg, L, hg * dv), lambda g, p, c: (g // groups, out_chunk(p, c), o_blk0 + g % groups)),
            pl.BlockSpec((bg, hg, 1, SUBLANES, L),
                         lambda g, p, c: (g // groups, g % groups, chunk_of(p, c), 0, 0)),
            pl.BlockSpec((bg, hg, L, SUBLANES),
                         lambda g, p, c: (g // groups, g % groups, out_chunk(p, c), 0)),
            pl.BlockSpec((hg, 1, dv), lambda g, p, c: (g % groups, 0, 0)),
        ],
        out_specs=pl.BlockSpec((bg, L, hg * dv), lambda g, p, c: (g // groups, out_chunk(p, c), g % groups)),
        out_shape=jax.ShapeDtypeStruct((bsz, seq, N_HEADS * dv), BF16),
        scratch_shapes=[
            pltpu.VMEM((bg * hg, dqk, dv), F32), pltpu.VMEM((bg * hg, dqk, 1), F32),
            pltpu.VMEM((bg * hg, 1, 1), F32),
            pltpu.VMEM((bg * hg, dqk, dv), F32), pltpu.VMEM((bg * hg, dqk, 1), F32),
            pltpu.VMEM((bg * hg, 1, 1), F32),
            pltpu.VMEM((nc, bg * hg, dqk, dv), BF16), pltpu.VMEM((nc, bg * hg, dqk, 1), F32),
            pltpu.VMEM((nc, bg * hg, 1, 1), F32),
        ],
        compiler_params=_cparams(("parallel", "arbitrary", "arbitrary")),
        name="mlstm",
    )(q, kt, proj3, proj3, gates_r, gates_c, norm_w)


def _mixout_kernel(hg_ref, a_ref, uga_ref, ugb_ref, x_ref, wbd_ref, wo_ref, bga_ref, bgb_ref,
                   g_ref, b_ref, o_ref, op_ref, *, alpha):
    branch_b = jnp.dot(hg_ref[...], wbd_ref[...], preferred_element_type=F32)
    ga = _sigmoid(uga_ref[...].astype(F32) + bga_ref[...])
    gb = _sigmoid(ugb_ref[...].astype(F32) + bgb_ref[...])
    merged = ga * a_ref[...].astype(F32) + gb * branch_b
    mix = jnp.dot(merged.astype(BF16), wo_ref[...], preferred_element_type=F32)
    y = _layer_norm(alpha * x_ref[...] + mix, g_ref[...], b_ref[...])
    o_ref[...] = y
    op_ref[...] = _pack_bf16_pairs(y)


def _mixout(hg, branch_a, proj, x, w_b_down, w_out, b_gate, ln_g, ln_b, ga_col, alpha):
    t, d = x.shape
    tm = ROW_TILE
    inner = hg.shape[1]
    row = lambda i: (i, 0)
    const = lambda i: (0, 0)
    return pl.pallas_call(
        functools.partial(_mixout_kernel, alpha=alpha),
        grid=(t // tm,),
        in_specs=[pl.BlockSpec((tm, inner), row),
                  pl.BlockSpec((tm, d), row),
                  pl.BlockSpec((tm, d), lambda i: (i, ga_col)),
                  pl.BlockSpec((tm, d), lambda i: (i, ga_col + 1)),
                  pl.BlockSpec((tm, d), row),
                  pl.BlockSpec((inner, d), const),
                  pl.BlockSpec((d, d), const),
                  pl.BlockSpec((1, d), const),
                  pl.BlockSpec((1, d), lambda i: (0, 1)),
                  pl.BlockSpec((1, d), const),
                  pl.BlockSpec((1, d), const)],
        out_specs=[pl.BlockSpec((tm, d), row), pl.BlockSpec((tm, d // 2), row)],
        out_shape=[jax.ShapeDtypeStruct((t, d), F32), jax.ShapeDtypeStruct((t, d // 2), U32)],
        compiler_params=_cparams(("parallel",)),
        name="mixout",
    )(hg, branch_a, proj, proj, x, w_b_down, w_out, b_gate, b_gate, ln_g, ln_b)


def _router_kernel(x_ref, wh_ref, wl_ref, br_ref, eidx_ref, wts_ref, rank_ref, cnt_ref, run_ref,
                   *, n_experts):
    i = pl.program_id(0)
    tm = x_ref.shape[0]
    ne = n_experts
    per_group = ne // N_ROUTE_GROUPS

    @pl.when(i == 0)
    def _():
        run_ref[...] = jnp.zeros_like(run_ref)

    x = x_ref[...]
    xh = x.astype(BF16)
    xl = (x - xh.astype(F32)).astype(BF16)
    nt_dims = (((1,), (1,)), ((), ()))
    logits = (lax.dot_general(wh_ref[...], xh, nt_dims, preferred_element_type=F32)
              + lax.dot_general(wh_ref[...], xl, nt_dims, preferred_element_type=F32)
              + lax.dot_general(wl_ref[...], xh, nt_dims, preferred_element_type=F32))
    scores = _sigmoid(logits)
    sel = scores + br_ref[...]
    neg = jnp.float32(-jnp.inf)
    big = jnp.float32(1e9)
    rowi = lax.broadcasted_iota(I32, (ne, tm), 0).astype(F32)

    gi = lax.broadcasted_iota(I32, (N_ROUTE_GROUPS, tm), 0).astype(F32)
    work = jnp.zeros((N_ROUTE_GROUPS, tm), F32)
    for g in range(N_ROUTE_GROUPS):
        blk = sel[g * per_group:(g + 1) * per_group]
        ri = lax.broadcasted_iota(I32, (per_group, tm), 0).astype(F32) + float(g * per_group)
        m1 = jnp.max(blk, axis=0, keepdims=True)
        i1 = jnp.min(jnp.where(blk == m1, ri, big), axis=0, keepdims=True)
        m2 = jnp.max(jnp.where(ri == i1, neg, blk), axis=0, keepdims=True)
        work = jnp.where(gi == float(g), m1 + m2, work)
    row_group = jnp.floor(rowi * (1.0 / per_group))
    allowed = jnp.zeros((ne, tm), F32)
    for _ in range(TOPK_GROUPS):
        m = jnp.max(work, axis=0, keepdims=True)
        idx = jnp.min(jnp.where(work == m, gi, big), axis=0, keepdims=True)
        work = jnp.where(gi == idx, neg, work)
        allowed = jnp.where(row_group == idx, 1.0, allowed)
    selm = jnp.where(allowed > 0.5, sel, neg)

    member = jnp.zeros((ne, tm), F32)
    idxs, wks = [], []
    for _ in range(TOP_K):
        m = jnp.max(selm, axis=0, keepdims=True)
        idx = jnp.min(jnp.where(selm == m, rowi, big), axis=0, keepdims=True)
        hit = rowi == idx
        wks.append(jnp.sum(jnp.where(hit, scores, 0.0), axis=0, keepdims=True))
        idxs.append(idx)
        member = jnp.where(hit, 1.0, member)
        selm = jnp.where(hit, neg, selm)
    wsum = wks[0]
    for wk in wks[1:]:
        wsum = wsum + wk

    ti = lax.broadcasted_iota(I32, (tm, tm), 0)
    tj = lax.broadcasted_iota(I32, (tm, tm), 1)
    strict = jnp.where(ti < tj, 1.0, 0.0).astype(BF16)
    prefix = jnp.dot(member.astype(BF16), strict, preferred_element_type=F32) + run_ref[...]
    ranks = [jnp.sum(jnp.where(rowi == idx, prefix, 0.0), axis=0, keepdims=True) for idx in idxs]
    run_new = run_ref[...] + jnp.sum(member, axis=1, keepdims=True)
    run_ref[...] = run_new

    eidx_ref[...] = jnp.concatenate(idxs, axis=0).astype(I32)
    wts_ref[...] = jnp.concatenate([wk / wsum * ROUTED_SCALE for wk in wks], axis=0)
    rank_ref[...] = jnp.concatenate(ranks, axis=0).astype(I32)
    cnt_ref[...] = jnp.broadcast_to(run_new, cnt_ref.shape).astype(I32)


def _router(x1, wr_hi, wr_lo, b_router):
    t, d = x1.shape
    ne = wr_hi.shape[0]
    tm = ROUTER_TILE
    kern = functools.partial(_router_kernel, n_experts=ne)
    tok = lambda i: (0, i)
    const = lambda i: (0, 0)
    return pl.pallas_call(
        kern,
        grid=(t // tm,),
        in_specs=[pl.BlockSpec((tm, d), lambda i: (i, 0)),
                  pl.BlockSpec((ne, d), const),
                  pl.BlockSpec((ne, d), const),
                  pl.BlockSpec((ne, 1), const)],
        out_specs=[pl.BlockSpec((TOP_K, tm), tok), pl.BlockSpec((TOP_K, tm), tok),
                   pl.BlockSpec((TOP_K, tm), tok), pl.BlockSpec((ne, LANES), const)],
        out_shape=[jax.ShapeDtypeStruct((TOP_K, t), I32), jax.ShapeDtypeStruct((TOP_K, t), F32),
                   jax.ShapeDtypeStruct((TOP_K, t), I32), jax.ShapeDtypeStruct((ne, LANES), I32)],
        scratch_shapes=[pltpu.VMEM((ne, 1), F32)],
        compiler_params=_cparams(("arbitrary",)),
        name="router",
    )(x1, wr_hi, wr_lo, b_router)


def _slots_kernel(eidx_ref, rank_ref, ps_ref, pos_ref, *, n_experts):
    tm = eidx_ref.shape[1]
    rowi = lax.broadcasted_iota(I32, (n_experts, tm), 0)
    eidx = eidx_ref[...]
    starts = ps_ref[...]
    base = [jnp.sum(jnp.where(rowi == eidx[k:k + 1], starts, 0.0), axis=0, keepdims=True)
            for k in range(TOP_K)]
    pos_ref[...] = jnp.concatenate(base, axis=0).astype(I32) + rank_ref[...]


def _slots(eidx, rank, pstarts):
    k, t = eidx.shape
    ne = pstarts.shape[0]
    tm = SLOT_TILE
    tok = lambda i: (0, i)
    return pl.pallas_call(
        functools.partial(_slots_kernel, n_experts=ne),
        grid=(t // tm,),
        in_specs=[pl.BlockSpec((k, tm), tok), pl.BlockSpec((k, tm), tok),
                  pl.BlockSpec((ne, 1), lambda i: (0, 0))],
        out_specs=pl.BlockSpec((k, tm), tok),
        out_shape=jax.ShapeDtypeStruct((k, t), I32),
        compiler_params=_cparams(("parallel",)),
        name="slots",
    )(eidx, rank, pstarts.astype(F32).reshape(ne, 1))


def _sc_worker_id():
    return lax.axis_index("subcore") * SC_CORES + lax.axis_index("core")


def _sc_mesh():
    return plsc.VectorSubcoreMesh(core_axis_name="core", subcore_axis_name="subcore")


def _sc_vector_params():
    cp = pltpu.CompilerParams()
    if "needs_layout_passes" in pltpu.CompilerParams.__dataclass_fields__:
        cp = dataclasses.replace(cp, needs_layout_passes=False)
    return cp


def _sc_gather(table, idx):
    n = idx.shape[0]
    w = table.shape[1]
    ch = SC_GATHER_ROWS
    n_ch = n // (SC_WORKERS * ch)
    assert n % (SC_WORKERS * ch * 2) == 0

    @functools.partial(
        pl.kernel, mesh=_sc_mesh(),
        out_type=jax.ShapeDtypeStruct((n, w), table.dtype),
        scratch_types=[pltpu.VMEM((n_ch, ch), I32), pltpu.VMEM((2, ch, w), table.dtype),
                       pltpu.SemaphoreType.DMA((2,))],
    )
    def kern(table_hbm, idx_hbm, out_hbm, idx_v, rows_v, sem):
        first = _sc_worker_id() * n_ch
        pltpu.sync_copy(idx_hbm.at[pl.ds(first, n_ch)], idx_v)

        def gather(j, b):
            return pltpu.make_async_copy(table_hbm.at[idx_v.at[j]], rows_v.at[b], sem.at[b])

        gather(0, 0).start()

        @pl.loop(0, n_ch, step=2)
        def _(j):
            for b in range(2):
                jj = j + b

                @pl.when(jj + 1 < n_ch)
                def _():
                    gather(jj + 1, 1 - b).start()

                gather(jj, b).wait()
                row0 = pl.multiple_of((first + jj) * ch, ch)
                pltpu.sync_copy(rows_v.at[b], out_hbm.at[pl.ds(row0, ch)])

    return kern(table, idx.reshape(n // ch, ch))


def _sc_invert(pos_flat, n_slots, n_tokens):
    n = pos_flat.shape[0]
    per_w = n_slots // SC_WORKERS
    chunk = SC_SCAN_CHUNK
    assert n_slots % (SC_WORKERS * SC_LANES) == 0 and n % chunk == 0 and n_tokens % chunk == 0

    @functools.partial(
        pl.kernel, mesh=_sc_mesh(),
        out_type=jax.ShapeDtypeStruct((n_slots,), I32),
        scratch_types=[pltpu.VMEM((per_w,), I32), pltpu.VMEM((chunk,), I32)],
        compiler_params=_sc_vector_params(),
    )
    def kern(pos_hbm, out_hbm, table_v, pos_v):
        lo = _sc_worker_id() * per_w
        lane = lax.iota(I32, SC_LANES)

        @pl.loop(0, per_w // SC_LANES)
        def _(i):
            off = pl.multiple_of(i * SC_LANES, SC_LANES)
            table_v[pl.ds(off, SC_LANES)] = lax.rem(lo + off + lane, jnp.full((SC_LANES,), n_tokens, I32))

        @pl.loop(0, n // chunk)
        def _(c):
            pltpu.sync_copy(pos_hbm.at[pl.ds(pl.multiple_of(c * chunk, chunk), chunk)], pos_v)
            tok0 = lax.rem(c, n_tokens // chunk) * chunk

            @plsc.parallel_loop(0, chunk // SC_LANES, 1, unroll=SC_SCAN_UNROLL)
            def _(j):
                off = pl.multiple_of(j * SC_LANES, SC_LANES)
                local = pos_v[pl.ds(off, SC_LANES)] - lo
                mine = (local >= 0) & (local < per_w)
                plsc.store_scatter(table_v, [jnp.where(mine, local, 0)], tok0 + off + lane, mask=mine)

        pltpu.sync_copy(table_v, out_hbm.at[pl.ds(pl.multiple_of(lo, SC_LANES), per_w)])

    return kern(pos_flat)


def _experts_kernel(te_ref, nu_ref, ord_ref, nxt_ref, nxt2_ref, xs_hbm, wg_hbm, wu_hbm, wd_hbm, ys_ref,
                    xbuf_ref, wgf_ref, wuf_ref, wdf_ref, wgb_ref, wub_ref, wdb_ref, sem_ref, xsem_ref):
    i = pl.program_id(0)
    n_used = nu_ref[0]
    tile = xbuf_ref.shape[1]

    def row_copy(step):
        slot = lax.rem(step, ROW_SLOTS)
        src = xs_hbm.at[pl.ds(pl.multiple_of(step * tile, tile), tile)]
        return pltpu.make_async_copy(src, xbuf_ref.at[slot], xsem_ref.at[slot])

    def weight_copies(e, slot):
        return (pltpu.make_async_copy(wg_hbm.at[e], wgf_ref.at[slot], sem_ref.at[slot]),
                pltpu.make_async_copy(wu_hbm.at[e], wuf_ref.at[slot], sem_ref.at[slot]),
                pltpu.make_async_copy(wd_hbm.at[e], wdf_ref.at[slot], sem_ref.at[slot]))

    @pl.when(i < n_used)
    def _():
        @pl.when(i == 0)
        def _():
            for ahead in range(ROW_SLOTS - 1):
                @pl.when(ahead < n_used)
                def _():
                    row_copy(ahead).start()

        @pl.when(i + (ROW_SLOTS - 1) < n_used)
        def _():
            row_copy(i + (ROW_SLOTS - 1)).start()

        e = te_ref[i]
        ordinal = ord_ref[i]
        slot = lax.rem(ordinal, WEIGHT_SLOTS)
        first_tile_of_expert = jnp.logical_or(i == 0, e != te_ref[jnp.maximum(i - 1, 0)])

        @pl.when(i == 0)
        def _():
            for cp in weight_copies(e, slot):
                cp.start(priority=WEIGHT_DMA_PRIORITY)

            @pl.when(nxt_ref[i] >= 0)
            def _():
                for cp in weight_copies(nxt_ref[i], lax.rem(ordinal + 1, WEIGHT_SLOTS)):
                    cp.start(priority=WEIGHT_DMA_PRIORITY)

        @pl.when(first_tile_of_expert)
        def _():
            ahead = nxt2_ref[i]

            @pl.when(ahead >= 0)
            def _():
                for cp in weight_copies(ahead, lax.rem(ordinal + 2, WEIGHT_SLOTS)):
                    cp.start(priority=WEIGHT_DMA_PRIORITY)

            for cp in weight_copies(e, slot):
                cp.wait()
            wgb_ref[...] = wgf_ref[slot].astype(BF16)
            wub_ref[...] = wuf_ref[slot].astype(BF16)
            wdb_ref[...] = wdf_ref[slot].astype(BF16)

        row_copy(i).wait()
        x = _unpack_bf16_pairs(xbuf_ref[lax.rem(i, ROW_SLOTS)]).astype(BF16)
        gate = jnp.dot(x, wgb_ref[...], preferred_element_type=F32)
        up = jnp.dot(x, wub_ref[...], preferred_element_type=F32)
        hid = (gate * _sigmoid(gate) * up).astype(BF16)
        y = jnp.dot(hid, wdb_ref[...], preferred_element_type=F32)
        ys_ref[...] = _pack_bf16_pairs(y)

    @pl.when(i >= n_used)
    def _():
        ys_ref[...] = jnp.zeros_like(ys_ref)


def _experts(tile_expert, n_used, expert_ord, expert_next, expert_next2, xs, w_gate_e, w_up_e, w_down_e):
    n_slots, w = xs.shape
    tile = EXPERT_TILE
    n_tiles = n_slots // tile
    _, d, de = w_gate_e.shape
    grid_spec = pltpu.PrefetchScalarGridSpec(
        num_scalar_prefetch=5,
        grid=(n_tiles,),
        in_specs=[pl.BlockSpec(memory_space=pl.ANY),
                  pl.BlockSpec(memory_space=pl.ANY),
                  pl.BlockSpec(memory_space=pl.ANY),
                  pl.BlockSpec(memory_space=pl.ANY)],
        out_specs=pl.BlockSpec((tile, w), lambda i, *_: (i, 0)),
        scratch_shapes=[pltpu.VMEM((ROW_SLOTS, tile, w), U32),
                        pltpu.VMEM((WEIGHT_SLOTS, d, de), F32), pltpu.VMEM((WEIGHT_SLOTS, d, de), F32),
                        pltpu.VMEM((WEIGHT_SLOTS, de, d), F32),
                        pltpu.VMEM((d, de), BF16), pltpu.VMEM((d, de), BF16), pltpu.VMEM((de, d), BF16),
                        pltpu.SemaphoreType.DMA((WEIGHT_SLOTS,)), pltpu.SemaphoreType.DMA((ROW_SLOTS,))],
    )
    return pl.pallas_call(
        _experts_kernel,
        grid_spec=grid_spec,
        out_shape=jax.ShapeDtypeStruct((n_slots, w), U32),
        compiler_params=_cparams(("arbitrary",)),
        name="experts",
    )(tile_expert, n_used, expert_ord, expert_next, expert_next2, xs, w_gate_e, w_up_e, w_down_e)


def _combine_kernel(wts_ref, x_ref, xp_ref, yg_ref, wgs_ref, wus_ref, wds_ref, g_ref, b_ref, o_ref, *, alpha):
    wts = wts_ref[...]
    routed = jnp.zeros(x_ref.shape, F32)
    for k in range(TOP_K):
        routed = routed + wts[:, k:k + 1] * _unpack_bf16_pairs(yg_ref[k])
    xb = _unpack_bf16_pairs(xp_ref[...]).astype(BF16)
    gate = jnp.dot(xb, wgs_ref[...], preferred_element_type=F32)
    up = jnp.dot(xb, wus_ref[...], preferred_element_type=F32)
    hid = (gate * _sigmoid(gate) * up).astype(BF16)
    shared = jnp.dot(hid, wds_ref[...], preferred_element_type=F32)
    o_ref[...] = _layer_norm(alpha * x_ref[...] + (routed + shared), g_ref[...], b_ref[...])


def _combine(wts_c, x1, x1p, yg, w_gate_s, w_up_s, w_down_s, ln_g, ln_b, alpha):
    t, d = x1.shape
    w = x1p.shape[1]
    tb = COMBINE_TILE
    ds = w_gate_s.shape[1]
    row = lambda i: (i, 0)
    const = lambda i: (0, 0)
    return pl.pallas_call(
        functools.partial(_combine_kernel, alpha=alpha),
        grid=(t // tb,),
        in_specs=[pl.BlockSpec((tb, TOP_K), row),
                  pl.BlockSpec((tb, d), row),
                  pl.BlockSpec((tb, w), row),
                  pl.BlockSpec((TOP_K, tb, w), lambda i: (0, i, 0)),
                  pl.BlockSpec((d, ds), const),
                  pl.BlockSpec((d, ds), const),
                  pl.BlockSpec((ds, d), const),
                  pl.BlockSpec((1, d), const),
                  pl.BlockSpec((1, d), const)],
        out_specs=pl.BlockSpec((tb, d), row),
        out_shape=jax.ShapeDtypeStruct((t, d), F32),
        compiler_params=_cparams(("parallel",)),
        name="combine",
    )(wts_c, x1, x1p, yg, w_gate_s, w_up_s, w_down_s, ln_g, ln_b)


def _layer(alpha, x, w_in, b_if, b_gate, conv_qk, pool_w, pool_scale, mh_norm_w, w_b_down, w_out,
           ln1_g, ln1_b, w_router, b_router, w_gate_e, w_up_e, w_down_e,
           w_gate_s, w_up_s, w_down_s, ln2_g, ln2_b):
    bsz, seq, d = x.shape
    t = bsz * seq
    heads = N_HEADS
    pool_width = pool_w.shape[0] * pool_w.shape[1]
    qk_cols = conv_qk.shape[1]
    v_cols = mh_norm_w.shape[0] * mh_norm_w.shape[1]
    o_cols = v_cols
    if_cols = b_if.shape[0]
    dv = mh_norm_w.shape[1]
    dqk = qk_cols // (2 * heads)
    off_if = pool_width + qk_cols + v_cols + o_cols
    off_gate = off_if + if_cols

    xf = x.reshape(t, d)
    xb = xf.astype(BF16)
    w_main = jnp.concatenate([w_in[:, :off_if], w_in[:, off_gate:]], axis=1).astype(BF16)
    w_if = jnp.pad(w_in[:, off_if:off_gate], ((0, 0), (0, LANES - if_cols))).astype(BF16)
    bias_if = jnp.pad(b_if, (0, LANES - if_cols)).reshape(1, LANES)
    n_main = w_main.shape[1]

    proj = _matmul_bias(xb, w_main, jnp.zeros((1, n_main), F32), BF16, INPROJ_TILE_M, INPROJ_TILE_N)
    u_if = _matmul_bias(xb, w_if, bias_if, F32, INPROJ_TILE_M, LANES)
    proj3 = proj.reshape(bsz, seq, n_main)

    branch_a = _pool(proj3, pool_w.astype(BF16), pool_scale.reshape(1, pool_width))

    cw = 2 * LANES
    q_col0 = pool_width // cw
    half = qk_cols // 2
    q = _qkconv(proj3, conv_qk[:, :half], q_col0, float(dqk) ** -0.5, False)
    kt = _qkconv(proj3, conv_qk[:, half:], q_col0 + half // cw, 1.0, True)

    nc = seq // MLSTM_CHUNK
    gates = u_if[:, :if_cols].reshape(bsz, seq, 4, heads).transpose(0, 3, 2, 1)
    gates = jnp.pad(gates, ((0, 0), (0, 0), (0, SUBLANES - 4), (0, 0)))
    gates_c = gates.transpose(0, 1, 3, 2)
    gates_r = gates.reshape(bsz, heads, SUBLANES, nc, MLSTM_CHUNK).transpose(0, 1, 3, 2, 4)
    hg = _mlstm(q, kt, proj3, gates_r, gates_c, mh_norm_w.reshape(heads, 1, dv),
                pool_width + qk_cols, pool_width + qk_cols + v_cols)

    ga_col = (pool_width + qk_cols + v_cols + o_cols) // d
    x1, x1p = _mixout(hg.reshape(t, heads * dv), branch_a.reshape(t, pool_width), proj, xf,
                      w_b_down.astype(BF16), w_out.astype(BF16), b_gate.reshape(1, 2 * d),
                      ln1_g.reshape(1, d), ln1_b.reshape(1, d), ga_col, alpha)

    ne = w_router.shape[1]
    wr_t = w_router.T
    wr_hi = wr_t.astype(BF16)
    wr_lo = (wr_t - wr_hi.astype(F32)).astype(BF16)
    eidx, wts, rank, cnt = _router(x1, wr_hi, wr_lo, b_router.reshape(ne, 1))

    tile = EXPERT_TILE
    n_tiles = (t * TOP_K) // tile + ne
    counts = cnt[:, 0]
    pcounts = ((counts + tile - 1) // tile) * tile
    pends = jnp.cumsum(pcounts)
    pstarts = pends - pcounts
    pos = _slots(eidx, rank, pstarts)

    n_used = (pends[-1] // tile).astype(I32)
    tile_ids = jnp.minimum(jnp.arange(n_tiles, dtype=I32), n_used - 1)
    tile_expert = jnp.sum((pends[None, :] <= (tile_ids * tile)[:, None]).astype(I32), axis=1)
    tile_expert = jnp.minimum(tile_expert, ne - 1)
    new_expert = jnp.concatenate([jnp.ones((1,), I32), (tile_expert[1:] != tile_expert[:-1]).astype(I32)])
    expert_ord = jnp.cumsum(new_expert) - 1
    candidates = jnp.where(counts > 0, jnp.arange(ne, dtype=I32), ne)
    later_min = lax.cummin(candidates, axis=0, reverse=True)
    next_used = jnp.concatenate([later_min[1:], jnp.full((1,), ne, I32)])
    next_used = jnp.where(next_used >= ne, -1, next_used)
    next_used2 = jnp.where(next_used >= 0, next_used[jnp.maximum(next_used, 0)], -1)
    tile_onehot = tile_expert[:, None] == jnp.arange(ne, dtype=I32)[None, :]
    expert_next = jnp.sum(jnp.where(tile_onehot, next_used[None, :], 0), axis=1).astype(I32)
    expert_next2 = jnp.sum(jnp.where(tile_onehot, next_used2[None, :], 0), axis=1).astype(I32)

    n_slots = n_tiles * tile
    slot_tok = _sc_invert(pos.reshape(-1), n_slots, t)
    xs = _sc_gather(x1p, slot_tok)
    ys = _experts(tile_expert, n_used.reshape(1), expert_ord.astype(I32), expert_next, expert_next2, xs,
                  w_gate_e, w_up_e, w_down_e)
    yg = _sc_gather(ys, pos.reshape(-1)).reshape(TOP_K, t, x1p.shape[1])
    out = _combine(wts.T, x1, x1p, yg, w_gate_s.astype(BF16), w_up_s.astype(BF16),
                   w_down_s.astype(BF16), ln2_g.reshape(1, d), ln2_b.reshape(1, d), alpha)
    return out.reshape(bsz, seq, d)


def kernel(x, w_in, b_if, b_gate, conv_qk, pool_w, pool_scale, mh_norm_w, w_b_down, w_out, ln1_g, ln1_b,
           w_router, b_router, w_gate_e, w_up_e, w_down_e, w_gate_s, w_up_s, w_down_s, ln2_g, ln2_b):
    depth = w_in.shape[0]
    alpha = (2.0 * depth) ** 0.25
    for l in range(depth):
        x = _layer(alpha, x, w_in[l], b_if[l], b_gate[l], conv_qk[l], pool_w[l], pool_scale[l], mh_norm_w[l],
                   w_b_down[l], w_out[l], ln1_g[l], ln1_b[l], w_router[l], b_router[l], w_gate_e[l],
                   w_up_e[l], w_down_e[l], w_gate_s[l], w_up_s[l], w_down_s[l], ln2_g[l], ln2_b[l])
    return x
```

```python
import dataclasses
import functools

import jax
import jax.numpy as jnp
import numpy as np
from jax import lax
from jax.experimental import pallas as pl
from jax.experimental.pallas import tpu as pltpu
from jax.experimental.pallas import tpu_sc as plsc

F32 = jnp.float32
BF16 = jnp.bfloat16
I32 = jnp.int32
U32 = jnp.uint32

N_HEADS = 4
POOL_GROUPS = 4
CONV_WIDTH = 5
LN_EPS = 1e-5
N_ROUTE_GROUPS = 8
TOPK_GROUPS = 4
TOP_K = 8
ROUTED_SCALE = 2.5

LANES = 128
SUBLANES = 8
BF16_ROWS = 16
VMEM_LIMIT = 56 * 1024 * 1024

INPROJ_TILE_M = 2048
INPROJ_TILE_N = 1536
MLSTM_CHUNK = 256
MLSTM_HEADS_PER_STEP = 2
MLSTM_BATCH_PER_STEP = 2
SEQ_TILE = 1024
ROW_TILE = 1024
ROUTER_TILE = 512
EXPERT_TILE = 256
COMBINE_TILE = 512
SLOT_TILE = 512
WEIGHT_SLOTS = 3
WEIGHT_DMA_PRIORITY = 0
ROW_SLOTS = 3

SC_CORES = 2
SC_SUBCORES = 16
SC_LANES = 16
SC_WORKERS = SC_CORES * SC_SUBCORES
SC_GATHER_ROWS = 64
SC_SCAN_CHUNK = 2048
SC_SCAN_UNROLL = 8

HI_MASK = 0xFFFF0000


def _cparams(sem):
    return pltpu.CompilerParams(dimension_semantics=sem, vmem_limit_bytes=VMEM_LIMIT)


def _sigmoid(x):
    return 1.0 / (1.0 + jnp.exp(-x))


def _layer_norm(y, g, b):
    mu = jnp.mean(y, axis=-1, keepdims=True)
    yc = y - mu
    var = jnp.mean(yc * yc, axis=-1, keepdims=True)
    return yc * lax.rsqrt(var + LN_EPS) * g + b


def _pack_bf16_pairs(y):
    c = y.shape[1] // 2
    bits = lax.bitcast_convert_type(y.astype(BF16).astype(F32), U32)
    return (bits[:, :c] >> 16) | (bits[:, c:] & jnp.uint32(HI_MASK))


def _unpack_bf16_pairs(p):
    lo = lax.bitcast_convert_type(p << 16, F32)
    hi = lax.bitcast_convert_type(p & jnp.uint32(HI_MASK), F32)
    return jnp.concatenate([lo, hi], axis=1)


def _inproj_kernel(x_ref, w_ref, wg_ref, bg_ref, o_ref, g_ref, xb_ref):
    @pl.when(pl.program_id(1) == 0)
    def _():
        xb_ref[...] = x_ref[...].astype(BF16)
        nt_dims = (((1,), (1,)), ((), ()))
        g_ref[0] = lax.dot_general(wg_ref[...], xb_ref[...], nt_dims, preferred_element_type=F32) + bg_ref[...]

    o_ref[...] = jnp.dot(xb_ref[...], w_ref[...], preferred_element_type=F32).astype(o_ref.dtype)


def _inproj(x, w, w_gate_t, bias_gate, seq):
    t, d = x.shape
    n = w.shape[1]
    r = w_gate_t.shape[0]
    tm, tn = INPROJ_TILE_M, INPROJ_TILE_N
    per_seq = seq // tm
    return pl.pallas_call(
        _inproj_kernel,
        grid=(t // tm, n // tn),
        in_specs=[pl.BlockSpec((tm, d), lambda i, j: (i, 0)),
                  pl.BlockSpec((d, tn), lambda i, j: (0, j)),
                  pl.BlockSpec((r, d), lambda i, j: (0, 0)),
                  pl.BlockSpec((r, 1), lambda i, j: (0, 0))],
        out_specs=[pl.BlockSpec((tm, tn), lambda i, j: (i, j)),
                   pl.BlockSpec((1, r, tm), lambda i, j: (i // per_seq, 0, i % per_seq))],
        out_shape=[jax.ShapeDtypeStruct((t, n), BF16), jax.ShapeDtypeStruct((t // seq, r, seq), F32)],
        scratch_shapes=[pltpu.VMEM((tm, d), BF16)],
        compiler_params=_cparams(("parallel", "arbitrary")),
        name="inproj",
    )(x, w, w_gate_t, bias_gate)


def _qkconv_kernel(prev_ref, main_ref, next_ref, w_ref, o_ref, *, ts, scale, transpose):
    t = pl.program_id(1)
    nt = pl.num_programs(1)
    main = main_ref[0].astype(F32)
    prev = prev_ref[0].astype(F32)[BF16_ROWS - SUBLANES:]
    nxt = next_ref[0].astype(F32)[:SUBLANES]
    prev = jnp.where(t > 0, prev, 0.0)
    nxt = jnp.where(t < nt - 1, nxt, 0.0)
    ext = jnp.concatenate([prev, main, nxt], axis=0)
    w = w_ref[...]
    pad = CONV_WIDTH // 2
    acc = jnp.zeros_like(main)
    for j in range(CONV_WIDTH):
        off = SUBLANES - pad + j
        acc = acc + ext[off:off + ts] * w[j:j + 1]
    y = acc * _sigmoid(acc) * scale
    if transpose:
        o_ref[0] = y.T.astype(o_ref.dtype)
    else:
        o_ref[0] = y.astype(o_ref.dtype)


def _qkconv(proj3, conv_w, col0, scale, transpose):
    bsz, seq, _ = proj3.shape
    ts = SEQ_TILE
    nt = seq // ts
    cw = 2 * LANES
    ncol = conv_w.shape[1] // cw
    hb = ts // BF16_ROWS
    n_hb = seq // BF16_ROWS
    kern = functools.partial(_qkconv_kernel, ts=ts, scale=scale, transpose=transpose)
    if transpose:
        out_shape = jax.ShapeDtypeStruct((bsz, ncol * cw, seq), BF16)
        out_spec = pl.BlockSpec((1, cw, ts), lambda b, t, j: (b, j, t))
    else:
        out_shape = jax.ShapeDtypeStruct((bsz, seq, ncol * cw), BF16)
        out_spec = pl.BlockSpec((1, ts, cw), lambda b, t, j: (b, t, j))
    return pl.pallas_call(
        kern,
        grid=(bsz, nt, ncol),
        in_specs=[pl.BlockSpec((1, BF16_ROWS, cw), lambda b, t, j: (b, jnp.maximum(t * hb - 1, 0), col0 + j)),
                  pl.BlockSpec((1, ts, cw), lambda b, t, j: (b, t, col0 + j)),
                  pl.BlockSpec((1, BF16_ROWS, cw),
                               lambda b, t, j: (b, jnp.minimum((t + 1) * hb, n_hb - 1), col0 + j)),
                  pl.BlockSpec((CONV_WIDTH, cw), lambda b, t, j: (0, j))],
        out_specs=out_spec,
        out_shape=out_shape,
        compiler_params=_cparams(("parallel", "parallel", "parallel")),
        name="qkconv_t" if transpose else "qkconv",
    )(proj3, proj3, proj3, conv_w)


def _pool_kernel(prev_ref, main_ref, next_ref, bm_ref, bp_ref, bn_ref, pw_ref, ps_ref, o_ref, *, ts, seq):
    g = pl.program_id(1)
    t = pl.program_id(2)
    nt = pl.num_programs(2)
    hw = jnp.left_shift(1, g)
    main = main_ref[0]
    prev = jnp.where(t > 0, prev_ref[0], jnp.zeros_like(prev_ref[0]))
    nxt = jnp.where(t < nt - 1, next_ref[0], jnp.zeros_like(next_ref[0]))
    s = (jnp.dot(bm_ref[0], main, preferred_element_type=F32)
         + jnp.dot(bp_ref[0], prev, preferred_element_type=F32)
         + jnp.dot(bn_ref[0], nxt, preferred_element_type=F32))
    tabs = t * ts + lax.broadcasted_iota(I32, (ts, 1), 0)
    cnt = jnp.minimum(tabs + hw, seq) - jnp.maximum(tabs - hw, 0)
    pooled = s / cnt.astype(F32) - main.astype(F32)
    mixed = jnp.dot(pooled.astype(BF16), pw_ref[0], preferred_element_type=F32) * ps_ref[...]
    o_ref[0] = mixed.astype(o_ref.dtype)


def _pool(proj3, pool_w, pool_scale):
    bsz, seq, _ = proj3.shape
    ts = SEQ_TILE
    nt = seq // ts
    cw = pool_w.shape[-1]
    hb = ts // LANES
    n_hb = seq // LANES
    i = np.arange(ts)[:, None]
    hws = [1 << g for g in range(POOL_GROUPS)]
    band_m = np.stack([(np.arange(ts)[None, :] >= i - hw) & (np.arange(ts)[None, :] < i + hw) for hw in hws])
    band_p = np.stack([(np.arange(LANES)[None, :] - LANES >= i - hw) for hw in hws])
    band_n = np.stack([(np.arange(LANES)[None, :] + ts < i + hw) for hw in hws])
    band_m, band_p, band_n = (jnp.asarray(b.astype(np.float32), BF16) for b in (band_m, band_p, band_n))
    kern = functools.partial(_pool_kernel, ts=ts, seq=seq)
    per_group = lambda b, g, t: (g, 0, 0)
    return pl.pallas_call(
        kern,
        grid=(bsz, POOL_GROUPS, nt),
        in_specs=[pl.BlockSpec((1, LANES, cw), lambda b, g, t: (b, jnp.maximum(t * hb - 1, 0), g)),
                  pl.BlockSpec((1, ts, cw), lambda b, g, t: (b, t, g)),
                  pl.BlockSpec((1, LANES, cw), lambda b, g, t: (b, jnp.minimum((t + 1) * hb, n_hb - 1), g)),
                  pl.BlockSpec((1, ts, ts), per_group),
                  pl.BlockSpec((1, ts, LANES), per_group),
                  pl.BlockSpec((1, ts, LANES), per_group),
                  pl.BlockSpec((1, cw, cw), per_group),
                  pl.BlockSpec((1, cw), lambda b, g, t: (0, g))],
        out_specs=pl.BlockSpec((1, ts, cw), lambda b, g, t: (b, t, g)),
        out_shape=jax.ShapeDtypeStruct((bsz, seq, POOL_GROUPS * cw), BF16),
        compiler_params=_cparams(("parallel", "parallel", "parallel")),
        name="pool",
    )(proj3, proj3, proj3, band_m, band_p, band_n, pool_w, pool_scale)


def _dot_split(a, b, a_is_value):
    val = a if a_is_value else b
    hi = val.astype(BF16)
    lo = (val - hi.astype(F32)).astype(BF16)
    if a_is_value:
        return (jnp.dot(hi, b, preferred_element_type=F32) + jnp.dot(lo, b, preferred_element_type=F32))
    return (jnp.dot(a, hi, preferred_element_type=F32) + jnp.dot(a, lo, preferred_element_type=F32))


def _dot_split_t(mask, val_r):
    hi = val_r.astype(BF16)
    lo = (val_r - hi.astype(F32)).astype(BF16)
    nt_dims = (((1,), (1,)), ((), ()))
    return (lax.dot_general(mask, hi, nt_dims, preferred_element_type=F32)
            + lax.dot_general(mask, lo, nt_dims, preferred_element_type=F32))


def _mlstm_kernel(q_ref, kt_ref, v_ref, uo_ref, gr_ref, nw_ref, o_ref,
                  cf_ref, nf_ref, mf_ref, cb_ref, nb_ref, mb_ref, cbs_ref, nbs_ref, mbs_ref,
                  *, chunk, n_chunks, heads):
    L = chunk
    nb = q_ref.shape[0]
    dqk = kt_ref.shape[1] // heads
    dv = v_ref.shape[2] // heads
    p = pl.program_id(1)
    c = pl.program_id(2)
    row = lax.broadcasted_iota(I32, (L, L), 0)
    col = lax.broadcasted_iota(I32, (L, L), 1)
    tri_le = row <= col
    tri_ge = row >= col
    m_le = jnp.where(tri_le, 1.0, 0.0).astype(BF16)
    m_ge = jnp.where(tri_ge, 1.0, 0.0).astype(BF16)
    lane_r = lax.broadcasted_iota(I32, (1, L), 1)
    neg_inf = jnp.float32(-jnp.inf)

    chains = [(bb, hd) for bb in range(nb) for hd in range(heads)]
    hs = range(len(chains))
    kts = [kt_ref[bb, hd * dqk:(hd + 1) * dqk, :] for bb, hd in chains]
    vs = [v_ref[bb, :, hd * dv:(hd + 1) * dv] for bb, hd in chains]
    gates = [gr_ref[bb, hd * SUBLANES:(hd + 1) * SUBLANES, :] for bb, hd in chains]
    lf_r = [jax.nn.log_sigmoid(g) for g in gates]

    def update_state(c_ref, n_ref, m_ref, g_r, tot):
        m_prev = [m_ref[hh] for hh in hs]
        m_new = [jnp.maximum(tot[hh] + m_prev[hh], jnp.max(g_r[hh], axis=1, keepdims=True)) for hh in hs]
        decay = [jnp.exp(tot[hh] + m_prev[hh] - m_new[hh]) for hh in hs]
        kw = [kts[hh].astype(F32) * jnp.exp(g_r[hh] - m_new[hh]) for hh in hs]
        upd = [jnp.dot(kw[hh].astype(BF16), vs[hh], preferred_element_type=F32) for hh in hs]
        for hh in hs:
            c_ref[hh] = decay[hh] * c_ref[hh] + upd[hh]
            n_ref[hh] = decay[hh] * n_ref[hh] + jnp.sum(kw[hh], axis=1, keepdims=True)
            m_ref[hh] = m_new[hh]

    @pl.when(p == 0)
    def _backward_states():
        @pl.when(c == 0)
        def _():
            cb_ref[...] = jnp.zeros_like(cb_ref)
            nb_ref[...] = jnp.zeros_like(nb_ref)
            mb_ref[...] = jnp.zeros_like(mb_ref)

        cc = n_chunks - 1 - c
        cbs_ref[cc] = cb_ref[...].astype(BF16)
        nbs_ref[cc] = nb_ref[...]
        mbs_ref[cc] = mb_ref[...]
        a_r = [_dot_split(lf_r[hh], m_ge, True)[3:4] for hh in hs]
        a0 = [jnp.sum(jnp.where(lane_r == 0, a_r[hh], 0.0), axis=1, keepdims=True) for hh in hs]
        g_r = [a0[hh] - a_r[hh] + gates[hh][2:3] for hh in hs]
        update_state(cb_ref, nb_ref, mb_ref, g_r, a0)

    @pl.when(p == 1)
    def _outputs():
        @pl.when(c == 0)
        def _():
            cf_ref[...] = jnp.zeros_like(cf_ref)
            nf_ref[...] = jnp.zeros_like(nf_ref)
            mf_ref[...] = jnp.zeros_like(mf_ref)

        nb_in = nbs_ref[c]
        mb_in = mbs_ref[c]
        cb_in = cbs_ref[c]
        qs = [q_ref[bb, :, hd * dqk:(hd + 1) * dqk] for bb, hd in chains]
        b_r = [_dot_split(lf_r[hh], m_le, True)[1:2] for hh in hs]
        a_r = [_dot_split(lf_r[hh], m_ge, True)[3:4] for hh in hs]
        b_c = [_dot_split_t(m_ge, lf_r[hh])[:, 1:2] for hh in hs]
        a_c = [_dot_split_t(m_le, lf_r[hh])[:, 3:4] for hh in hs]
        li_f = [gates[hh][0:1] for hh in hs]
        li_b = [gates[hh][2:3] for hh in hs]

        s = [jnp.dot(qs[hh], kts[hh], preferred_element_type=F32) for hh in hs]
        nlane = lax.broadcasted_iota(I32, (dqk, LANES), 1)
        nmat = [jnp.where(nlane == 0, nf_ref[hh], jnp.where(nlane == 1, nb_in[hh], 0.0)).astype(BF16)
                for hh in hs]
        qn = [jnp.dot(qs[hh], nmat[hh], preferred_element_type=F32) for hh in hs]

        def direction(d, mask, cum_c, m_prev, qn_col):
            d = [jnp.where(mask, d[hh], neg_inf) for hh in hs]
            m_inter = [cum_c[hh] + m_prev[hh] for hh in hs]
            m_t = [jnp.maximum(m_inter[hh], jnp.max(d[hh], axis=1, keepdims=True)) for hh in hs]
            pmat = [jnp.exp(d[hh] - m_t[hh]) * s[hh] for hh in hs]
            w_inter = [jnp.exp(m_inter[hh] - m_t[hh]) for hh in hs]
            den = [jnp.sum(pmat[hh], axis=1, keepdims=True) + w_inter[hh] * qn_col[hh] for hh in hs]
            r = [1.0 / jnp.maximum(jnp.abs(den[hh]), jnp.exp(-m_t[hh])) for hh in hs]
            return [pmat[hh] * r[hh] for hh in hs], [w_inter[hh] * r[hh] for hh in hs]

        pf, sf = direction([b_c[hh] - (b_r[hh] - li_f[hh]) for hh in hs], tri_ge, b_c,
                           [mf_ref[hh] for hh in hs], [qn[hh][:, 0:1] for hh in hs])
        pb, sb = direction([a_c[hh] - (a_r[hh] - li_b[hh]) for hh in hs], tri_le, a_c,
                           [mb_in[hh] for hh in hs], [qn[hh][:, 1:2] for hh in hs])
        qf = [qs[hh].astype(F32) for hh in hs]
        h = [jnp.dot((pf[hh] + pb[hh]).astype(BF16), vs[hh], preferred_element_type=F32)
             + jnp.dot((qf[hh] * sf[hh]).astype(BF16), cf_ref[hh].astype(BF16), preferred_element_type=F32)
             + jnp.dot((qf[hh] * sb[hh]).astype(BF16), cb_in[hh], preferred_element_type=F32) for hh in hs]

        mu = [jnp.mean(h[hh], axis=1, keepdims=True) for hh in hs]
        hc = [h[hh] - mu[hh] for hh in hs]
        var = [jnp.mean(hc[hh] * hc[hh], axis=1, keepdims=True) for hh in hs]
        hn = [hc[hh] * lax.rsqrt(var[hh] + LN_EPS) * nw_ref[chains[hh][1]] for hh in hs]
        for hh, (bb, hd) in enumerate(chains):
            gate_o = _sigmoid(uo_ref[bb, :, hd * dv:(hd + 1) * dv].astype(F32))
            o_ref[bb, :, hd * dv:(hd + 1) * dv] = (gate_o * hn[hh]).astype(o_ref.dtype)

        b_last = [jnp.sum(jnp.where(lane_r == L - 1, b_r[hh], 0.0), axis=1, keepdims=True) for hh in hs]
        update_state(cf_ref, nf_ref, mf_ref, [b_last[hh] - b_r[hh] + li_f[hh] for hh in hs], b_last)


def _mlstm(q, kt, proj3, gates, norm_w, v_off, o_off):
    bsz, seq, qw = q.shape
    dqk = qw // N_HEADS
    dv = norm_w.shape[-1]
    L = MLSTM_CHUNK
    nc = seq // L
    hg = MLSTM_HEADS_PER_STEP
    bg = MLSTM_BATCH_PER_STEP
    groups = N_HEADS // hg
    assert N_HEADS % hg == 0 and bsz % bg == 0 and v_off % (hg * dv) == 0 and o_off % (hg * dv) == 0
    v_blk0 = v_off // (hg * dv)
    o_blk0 = o_off // (hg * dv)
    kern = functools.partial(_mlstm_kernel, chunk=L, n_chunks=nc, heads=hg)

    def chunk_of(p, c):
        return jnp.where(p == 0, nc - 1 - c, c)

    def out_chunk(p, c):
        return jnp.where(p == 0, 0, c)

    return pl.pallas_call(
        kern,
        grid=((bsz // bg) * groups, 2, nc),
        in_specs=[
            pl.BlockSpec((bg, L, hg * dqk), lambda g, p, c: (g // groups, out_chunk(p, c), g % groups)),
            pl.BlockSpec((bg, hg * dqk, L), lambda g, p, c: (g // groups, g % groups, chunk_of(p, c))),
            pl.BlockSpec((bg, L, hg * dv), lambda g, p, c: (g // groups, chunk_of(p, c), v_blk0 + g % groups)),
            pl.BlockSpec((bg, L, hg * dv), lambda g, p, c: (g // groups, out_chunk(p, c), o_blk0 + g % groups)),
            pl.BlockSpec((bg, hg * SUBLANES, L), lambda g, p, c: (g // groups, g % groups, chunk_of(p, c))),
            pl.BlockSpec((hg, 1, dv), lambda g, p, c: (g % groups, 0, 0)),
        ],
        out_specs=pl.BlockSpec((bg, L, hg * dv), lambda g, p, c: (g // groups, out_chunk(p, c), g % groups)),
        out_shape=jax.ShapeDtypeStruct((bsz, seq, N_HEADS * dv), BF16),
        scratch_shapes=[
            pltpu.VMEM((bg * hg, dqk, dv), F32), pltpu.VMEM((bg * hg, dqk, 1), F32),
            pltpu.VMEM((bg * hg, 1, 1), F32),
            pltpu.VMEM((bg * hg, dqk, dv), F32), pltpu.VMEM((bg * hg, dqk, 1), F32),
            pltpu.VMEM((bg * hg, 1, 1), F32),
            pltpu.VMEM((nc, bg * hg, dqk, dv), BF16), pltpu.VMEM((nc, bg * hg, dqk, 1), F32),
            pltpu.VMEM((nc, bg * hg, 1, 1), F32),
        ],
        compiler_params=_cparams(("parallel", "arbitrary", "arbitrary")),
        name="mlstm",
    )(q, kt, proj3, proj3, gates, norm_w)


def _mixout_kernel(hg_ref, a_ref, uga_ref, ugb_ref, x_ref, wbd_ref, wo_ref, bga_ref, bgb_ref,
                   g_ref, b_ref, o_ref, op_ref, *, alpha):
    branch_b = jnp.dot(hg_ref[...], wbd_ref[...], preferred_element_type=F32)
    ga = _sigmoid(uga_ref[...].astype(F32) + bga_ref[...])
    gb = _sigmoid(ugb_ref[...].astype(F32) + bgb_ref[...])
    merged = ga * a_ref[...].astype(F32) + gb * branch_b
    mix = jnp.dot(merged.astype(BF16), wo_ref[...], preferred_element_type=F32)
    y = _layer_norm(alpha * x_ref[...] + mix, g_ref[...], b_ref[...])
    o_ref[...] = y
    op_ref[...] = _pack_bf16_pairs(y)


def _mixout(hg, branch_a, proj, x, w_b_down, w_out, b_gate, ln_g, ln_b, ga_col, alpha):
    t, d = x.shape
    tm = ROW_TILE
    inner = hg.shape[1]
    row = lambda i: (i, 0)
    const = lambda i: (0, 0)
    return pl.pallas_call(
        functools.partial(_mixout_kernel, alpha=alpha),
        grid=(t // tm,),
        in_specs=[pl.BlockSpec((tm, inner), row),
                  pl.BlockSpec((tm, d), row),
                  pl.BlockSpec((tm, d), lambda i: (i, ga_col)),
                  pl.BlockSpec((tm, d), lambda i: (i, ga_col + 1)),
                  pl.BlockSpec((tm, d), row),
                  pl.BlockSpec((inner, d), const),
                  pl.BlockSpec((d, d), const),
                  pl.BlockSpec((1, d), const),
                  pl.BlockSpec((1, d), lambda i: (0, 1)),
                  pl.BlockSpec((1, d), const),
                  pl.BlockSpec((1, d), const)],
        out_specs=[pl.BlockSpec((tm, d), row), pl.BlockSpec((tm, d // 2), row)],
        out_shape=[jax.ShapeDtypeStruct((t, d), F32), jax.ShapeDtypeStruct((t, d // 2), U32)],
        compiler_params=_cparams(("parallel",)),
        name="mixout",
    )(hg, branch_a, proj, proj, x, w_b_down, w_out, b_gate, b_gate, ln_g, ln_b)


def _router_kernel(x_ref, wh_ref, wl_ref, br_ref, eidx_ref, wts_ref, rank_ref, cnt_ref, run_ref,
                   *, n_experts):
    i = pl.program_id(0)
    tm = x_ref.shape[0]
    ne = n_experts
    per_group = ne // N_ROUTE_GROUPS

    @pl.when(i == 0)
    def _():
        run_ref[...] = jnp.zeros_like(run_ref)

    x = x_ref[...]
    xh = x.astype(BF16)
    xl = (x - xh.astype(F32)).astype(BF16)
    nt_dims = (((1,), (1,)), ((), ()))
    logits = (lax.dot_general(wh_ref[...], xh, nt_dims, preferred_element_type=F32)
              + lax.dot_general(wh_ref[...], xl, nt_dims, preferred_element_type=F32)
              + lax.dot_general(wl_ref[...], xh, nt_dims, preferred_element_type=F32))
    scores = _sigmoid(logits)
    sel = scores + br_ref[...]
    neg = jnp.float32(-jnp.inf)
    big = jnp.float32(1e9)
    rowi = lax.broadcasted_iota(I32, (ne, tm), 0).astype(F32)

    gi = lax.broadcasted_iota(I32, (N_ROUTE_GROUPS, tm), 0).astype(F32)
    work = jnp.zeros((N_ROUTE_GROUPS, tm), F32)
    for g in range(N_ROUTE_GROUPS):
        blk = sel[g * per_group:(g + 1) * per_group]
        ri = lax.broadcasted_iota(I32, (per_group, tm), 0).astype(F32) + float(g * per_group)
        m1 = jnp.max(blk, axis=0, keepdims=True)
        i1 = jnp.min(jnp.where(blk == m1, ri, big), axis=0, keepdims=True)
        m2 = jnp.max(jnp.where(ri == i1, neg, blk), axis=0, keepdims=True)
        work = jnp.where(gi == float(g), m1 + m2, work)
    row_group = jnp.floor(rowi * (1.0 / per_group))
    allowed = jnp.zeros((ne, tm), F32)
    for _ in range(TOPK_GROUPS):
        m = jnp.max(work, axis=0, keepdims=True)
        idx = jnp.min(jnp.where(work == m, gi, big), axis=0, keepdims=True)
        work = jnp.where(gi == idx, neg, work)
        allowed = jnp.where(row_group == idx, 1.0, allowed)
    selm = jnp.where(allowed > 0.5, sel, neg)

    member = jnp.zeros((ne, tm), F32)
    idxs, wks = [], []
    for _ in range(TOP_K):
        m = jnp.max(selm, axis=0, keepdims=True)
        idx = jnp.min(jnp.where(selm == m, rowi, big), axis=0, keepdims=True)
        hit = rowi == idx
        wks.append(jnp.sum(jnp.where(hit, scores, 0.0), axis=0, keepdims=True))
        idxs.append(idx)
        member = jnp.where(hit, 1.0, member)
        selm = jnp.where(hit, neg, selm)
    wsum = wks[0]
    for wk in wks[1:]:
        wsum = wsum + wk

    ti = lax.broadcasted_iota(I32, (tm, tm), 0)
    tj = lax.broadcasted_iota(I32, (tm, tm), 1)
    strict = jnp.where(ti < tj, 1.0, 0.0).astype(BF16)
    prefix = jnp.dot(member.astype(BF16), strict, preferred_element_type=F32) + run_ref[...]
    ranks = [jnp.sum(jnp.where(rowi == idx, prefix, 0.0), axis=0, keepdims=True) for idx in idxs]
    run_new = run_ref[...] + jnp.sum(member, axis=1, keepdims=True)
    run_ref[...] = run_new

    eidx_ref[...] = jnp.concatenate(idxs, axis=0).astype(I32)
    wts_ref[...] = jnp.concatenate([wk / wsum * ROUTED_SCALE for wk in wks], axis=0)
    rank_ref[...] = jnp.concatenate(ranks, axis=0).astype(I32)
    cnt_ref[...] = jnp.broadcast_to(run_new, cnt_ref.shape).astype(I32)


def _router(x1, wr_hi, wr_lo, b_router):
    t, d = x1.shape
    ne = wr_hi.shape[0]
    tm = ROUTER_TILE
    kern = functools.partial(_router_kernel, n_experts=ne)
    tok = lambda i: (0, i)
    const = lambda i: (0, 0)
    return pl.pallas_call(
        kern,
        grid=(t // tm,),
        in_specs=[pl.BlockSpec((tm, d), lambda i: (i, 0)),
                  pl.BlockSpec((ne, d), const),
                  pl.BlockSpec((ne, d), const),
                  pl.BlockSpec((ne, 1), const)],
        out_specs=[pl.BlockSpec((TOP_K, tm), tok), pl.BlockSpec((TOP_K, tm), tok),
                   pl.BlockSpec((TOP_K, tm), tok), pl.BlockSpec((ne, LANES), const)],
        out_shape=[jax.ShapeDtypeStruct((TOP_K, t), I32), jax.ShapeDtypeStruct((TOP_K, t), F32),
                   jax.ShapeDtypeStruct((TOP_K, t), I32), jax.ShapeDtypeStruct((ne, LANES), I32)],
        scratch_shapes=[pltpu.VMEM((ne, 1), F32)],
        compiler_params=_cparams(("arbitrary",)),
        name="router",
    )(x1, wr_hi, wr_lo, b_router)


def _slots_kernel(eidx_ref, rank_ref, ps_ref, pos_ref, *, n_experts):
    tm = eidx_ref.shape[1]
    rowi = lax.broadcasted_iota(I32, (n_experts, tm), 0)
    eidx = eidx_ref[...]
    starts = ps_ref[...]
    base = [jnp.sum(jnp.where(rowi == eidx[k:k + 1], starts, 0.0), axis=0, keepdims=True)
            for k in range(TOP_K)]
    pos_ref[...] = jnp.concatenate(base, axis=0).astype(I32) + rank_ref[...]


def _slots(eidx, rank, pstarts):
    k, t = eidx.shape
    ne = pstarts.shape[0]
    tm = SLOT_TILE
    tok = lambda i: (0, i)
    return pl.pallas_call(
        functools.partial(_slots_kernel, n_experts=ne),
        grid=(t // tm,),
        in_specs=[pl.BlockSpec((k, tm), tok), pl.BlockSpec((k, tm), tok),
                  pl.BlockSpec((ne, 1), lambda i: (0, 0))],
        out_specs=pl.BlockSpec((k, tm), tok),
        out_shape=jax.ShapeDtypeStruct((k, t), I32),
        compiler_params=_cparams(("parallel",)),
        name="slots",
    )(eidx, rank, pstarts.astype(F32).reshape(ne, 1))


def _sc_worker_id():
    return lax.axis_index("subcore") * SC_CORES + lax.axis_index("core")


def _sc_mesh():
    return plsc.VectorSubcoreMesh(core_axis_name="core", subcore_axis_name="subcore")


def _sc_vector_params():
    cp = pltpu.CompilerParams()
    if "needs_layout_passes" in pltpu.CompilerParams.__dataclass_fields__:
        cp = dataclasses.replace(cp, needs_layout_passes=False)
    return cp


def _sc_gather(table, idx):
    n = idx.shape[0]
    w = table.shape[1]
    ch = SC_GATHER_ROWS
    n_ch = n // (SC_WORKERS * ch)
    assert n % (SC_WORKERS * ch * 2) == 0

    @functools.partial(
        pl.kernel, mesh=_sc_mesh(),
        out_type=jax.ShapeDtypeStruct((n, w), table.dtype),
        scratch_types=[pltpu.VMEM((n_ch, ch), I32), pltpu.VMEM((2, ch, w), table.dtype),
                       pltpu.SemaphoreType.DMA((2,))],
    )
    def kern(table_hbm, idx_hbm, out_hbm, idx_v, rows_v, sem):
        first = _sc_worker_id() * n_ch
        pltpu.sync_copy(idx_hbm.at[pl.ds(first, n_ch)], idx_v)

        def gather(j, b):
            return pltpu.make_async_copy(table_hbm.at[idx_v.at[j]], rows_v.at[b], sem.at[b])

        gather(0, 0).start()

        @pl.loop(0, n_ch, step=2)
        def _(j):
            for b in range(2):
                jj = j + b

                @pl.when(jj + 1 < n_ch)
                def _():
                    gather(jj + 1, 1 - b).start()

                gather(jj, b).wait()
                row0 = pl.multiple_of((first + jj) * ch, ch)
                pltpu.sync_copy(rows_v.at[b], out_hbm.at[pl.ds(row0, ch)])

    return kern(table, idx.reshape(n // ch, ch))


def _sc_invert(pos_flat, n_slots, n_tokens):
    n = pos_flat.shape[0]
    per_w = n_slots // SC_WORKERS
    chunk = SC_SCAN_CHUNK
    assert n_slots % (SC_WORKERS * SC_LANES) == 0 and n % chunk == 0 and n_tokens % chunk == 0

    @functools.partial(
        pl.kernel, mesh=_sc_mesh(),
        out_type=jax.ShapeDtypeStruct((n_slots,), I32),
        scratch_types=[pltpu.VMEM((per_w,), I32), pltpu.VMEM((chunk,), I32)],
        compiler_params=_sc_vector_params(),
    )
    def kern(pos_hbm, out_hbm, table_v, pos_v):
        lo = _sc_worker_id() * per_w
        lane = lax.iota(I32, SC_LANES)

        @pl.loop(0, per_w // SC_LANES)
        def _(i):
            off = pl.multiple_of(i * SC_LANES, SC_LANES)
            table_v[pl.ds(off, SC_LANES)] = lax.rem(lo + off + lane, jnp.full((SC_LANES,), n_tokens, I32))

        @pl.loop(0, n // chunk)
        def _(c):
            pltpu.sync_copy(pos_hbm.at[pl.ds(pl.multiple_of(c * chunk, chunk), chunk)], pos_v)
            tok0 = lax.rem(c, n_tokens // chunk) * chunk

            @plsc.parallel_loop(0, chunk // SC_LANES, 1, unroll=SC_SCAN_UNROLL)
            def _(j):
                off = pl.multiple_of(j * SC_LANES, SC_LANES)
                local = pos_v[pl.ds(off, SC_LANES)] - lo
                mine = (local >= 0) & (local < per_w)
                plsc.store_scatter(table_v, [jnp.where(mine, local, 0)], tok0 + off + lane, mask=mine)

        pltpu.sync_copy(table_v, out_hbm.at[pl.ds(pl.multiple_of(lo, SC_LANES), per_w)])

    return kern(pos_flat)


def _experts_kernel(te_ref, nu_ref, ord_ref, nxt_ref, nxt2_ref, xs_hbm, wg_hbm, wu_hbm, wd_hbm, ys_ref,
                    xbuf_ref, wgf_ref, wuf_ref, wdf_ref, wgb_ref, wub_ref, wdb_ref, sem_ref, xsem_ref):
    i = pl.program_id(0)
    n_used = nu_ref[0]
    tile = xbuf_ref.shape[1]

    def row_copy(step):
        slot = lax.rem(step, ROW_SLOTS)
        src = xs_hbm.at[pl.ds(pl.multiple_of(step * tile, tile), tile)]
        return pltpu.make_async_copy(src, xbuf_ref.at[slot], xsem_ref.at[slot])

    def weight_copies(e, slot):
        return (pltpu.make_async_copy(wg_hbm.at[e], wgf_ref.at[slot], sem_ref.at[slot]),
                pltpu.make_async_copy(wu_hbm.at[e], wuf_ref.at[slot], sem_ref.at[slot]),
                pltpu.make_async_copy(wd_hbm.at[e], wdf_ref.at[slot], sem_ref.at[slot]))

    @pl.when(i < n_used)
    def _():
        @pl.when(i == 0)
        def _():
            for ahead in range(ROW_SLOTS - 1):
                @pl.when(ahead < n_used)
                def _():
                    row_copy(ahead).start()

        @pl.when(i + (ROW_SLOTS - 1) < n_used)
        def _():
            row_copy(i + (ROW_SLOTS - 1)).start()

        e = te_ref[i]
        ordinal = ord_ref[i]
        slot = lax.rem(ordinal, WEIGHT_SLOTS)
        first_tile_of_expert = jnp.logical_or(i == 0, e != te_ref[jnp.maximum(i - 1, 0)])

        @pl.when(i == 0)
        def _():
            for cp in weight_copies(e, slot):
                cp.start(priority=WEIGHT_DMA_PRIORITY)

            @pl.when(nxt_ref[i] >= 0)
            def _():
                for cp in weight_copies(nxt_ref[i], lax.rem(ordinal + 1, WEIGHT_SLOTS)):
                    cp.start(priority=WEIGHT_DMA_PRIORITY)

        @pl.when(first_tile_of_expert)
        def _():
            ahead = nxt2_ref[i]

            @pl.when(ahead >= 0)
            def _():
                for cp in weight_copies(ahead, lax.rem(ordinal + 2, WEIGHT_SLOTS)):
                    cp.start(priority=WEIGHT_DMA_PRIORITY)

            for cp in weight_copies(e, slot):
                cp.wait()
            wgb_ref[...] = wgf_ref[slot].astype(BF16)
            wub_ref[...] = wuf_ref[slot].astype(BF16)
            wdb_ref[...] = wdf_ref[slot].astype(BF16)

        row_copy(i).wait()
        x = _unpack_bf16_pairs(xbuf_ref[lax.rem(i, ROW_SLOTS)]).astype(BF16)
        gate = jnp.dot(x, wgb_ref[...], preferred_element_type=F32)
        up = jnp.dot(x, wub_ref[...], preferred_element_type=F32)
        hid = (gate * _sigmoid(gate) * up).astype(BF16)
        y = jnp.dot(hid, wdb_ref[...], preferred_element_type=F32)
        ys_ref[...] = _pack_bf16_pairs(y)

    @pl.when(i >= n_used)
    def _():
        ys_ref[...] = jnp.zeros_like(ys_ref)


def _experts(tile_expert, n_used, expert_ord, expert_next, expert_next2, xs, w_gate_e, w_up_e, w_down_e):
    n_slots, w = xs.shape
    tile = EXPERT_TILE
    n_tiles = n_slots // tile
    _, d, de = w_gate_e.shape
    grid_spec = pltpu.PrefetchScalarGridSpec(
        num_scalar_prefetch=5,
        grid=(n_tiles,),
        in_specs=[pl.BlockSpec(memory_space=pl.ANY),
                  pl.BlockSpec(memory_space=pl.ANY),
                  pl.BlockSpec(memory_space=pl.ANY),
                  pl.BlockSpec(memory_space=pl.ANY)],
        out_specs=pl.BlockSpec((tile, w), lambda i, *_: (i, 0)),
        scratch_shapes=[pltpu.VMEM((ROW_SLOTS, tile, w), U32),
                        pltpu.VMEM((WEIGHT_SLOTS, d, de), F32), pltpu.VMEM((WEIGHT_SLOTS, d, de), F32),
                        pltpu.VMEM((WEIGHT_SLOTS, de, d), F32),
                        pltpu.VMEM((d, de), BF16), pltpu.VMEM((d, de), BF16), pltpu.VMEM((de, d), BF16),
                        pltpu.SemaphoreType.DMA((WEIGHT_SLOTS,)), pltpu.SemaphoreType.DMA((ROW_SLOTS,))],
    )
    return pl.pallas_call(
        _experts_kernel,
        grid_spec=grid_spec,
        out_shape=jax.ShapeDtypeStruct((n_slots, w), U32),
        compiler_params=_cparams(("arbitrary",)),
        name="experts",
    )(tile_expert, n_used, expert_ord, expert_next, expert_next2, xs, w_gate_e, w_up_e, w_down_e)


def _combine_kernel(wts_ref, x_ref, xp_ref, yg_ref, wgs_ref, wus_ref, wds_ref, g_ref, b_ref, o_ref, *, alpha):
    wts = wts_ref[...]
    routed = jnp.zeros(x_ref.shape, F32)
    for k in range(TOP_K):
        routed = routed + wts[:, k:k + 1] * _unpack_bf16_pairs(yg_ref[k])
    xb = _unpack_bf16_pairs(xp_ref[...]).astype(BF16)
    gate = jnp.dot(xb, wgs_ref[...], preferred_element_type=F32)
    up = jnp.dot(xb, wus_ref[...], preferred_element_type=F32)
    hid = (gate * _sigmoid(gate) * up).astype(BF16)
    shared = jnp.dot(hid, wds_ref[...], preferred_element_type=F32)
    o_ref[...] = _layer_norm(alpha * x_ref[...] + (routed + shared), g_ref[...], b_ref[...])


def _combine(wts_c, x1, x1p, yg, w_gate_s, w_up_s, w_down_s, ln_g, ln_b, alpha):
    t, d = x1.shape
    w = x1p.shape[1]
    tb = COMBINE_TILE
    ds = w_gate_s.shape[1]
    row = lambda i: (i, 0)
    const = lambda i: (0, 0)
    return pl.pallas_call(
        functools.partial(_combine_kernel, alpha=alpha),
        grid=(t // tb,),
        in_specs=[pl.BlockSpec((tb, TOP_K), row),
                  pl.BlockSpec((tb, d), row),
                  pl.BlockSpec((tb, w), row),
                  pl.BlockSpec((TOP_K, tb, w), lambda i: (0, i, 0)),
                  pl.BlockSpec((d, ds), const),
                  pl.BlockSpec((d, ds), const),
                  pl.BlockSpec((ds, d), const),
                  pl.BlockSpec((1, d), const),
                  pl.BlockSpec((1, d), const)],
        out_specs=pl.BlockSpec((tb, d), row),
        out_shape=jax.ShapeDtypeStruct((t, d), F32),
        compiler_params=_cparams(("parallel",)),
        name="combine",
    )(wts_c, x1, x1p, yg, w_gate_s, w_up_s, w_down_s, ln_g, ln_b)


def _layer(alpha, x, w_in, b_if, b_gate, conv_qk, pool_w, pool_scale, mh_norm_w, w_b_down, w_out,
           ln1_g, ln1_b, w_router, b_router, w_gate_e, w_up_e, w_down_e,
           w_gate_s, w_up_s, w_down_s, ln2_g, ln2_b):
    bsz, seq, d = x.shape
    t = bsz * seq
    heads = N_HEADS
    pool_width = pool_w.shape[0] * pool_w.shape[1]
    qk_cols = conv_qk.shape[1]
    v_cols = mh_norm_w.shape[0] * mh_norm_w.shape[1]
    o_cols = v_cols
    if_cols = b_if.shape[0]
    dv = mh_norm_w.shape[1]
    dqk = qk_cols // (2 * heads)
    off_if = pool_width + qk_cols + v_cols + o_cols
    off_gate = off_if + if_cols

    xf = x.reshape(t, d)
    w_main = jnp.concatenate([w_in[:, :off_if], w_in[:, off_gate:]], axis=1).astype(BF16)
    n_gate = if_cols // heads
    w_if = w_in[:, off_if:off_gate].reshape(d, n_gate, heads).transpose(2, 1, 0)
    w_if = jnp.pad(w_if, ((0, 0), (0, SUBLANES - n_gate), (0, 0))).reshape(heads * SUBLANES, d).astype(BF16)
    bias_if = jnp.pad(b_if.reshape(n_gate, heads).T, ((0, 0), (0, SUBLANES - n_gate))).reshape(heads * SUBLANES, 1)
    n_main = w_main.shape[1]

    proj, gates = _inproj(xf, w_main, w_if, bias_if, seq)
    proj3 = proj.reshape(bsz, seq, n_main)

    branch_a = _pool(proj3, pool_w.astype(BF16), pool_scale.reshape(1, pool_width))

    cw = 2 * LANES
    q_col0 = pool_width // cw
    half = qk_cols // 2
    q = _qkconv(proj3, conv_qk[:, :half], q_col0, float(dqk) ** -0.5, False)
    kt = _qkconv(proj3, conv_qk[:, half:], q_col0 + half // cw, 1.0, True)

    hg = _mlstm(q, kt, proj3, gates, mh_norm_w.reshape(heads, 1, dv),
                pool_width + qk_cols, pool_width + qk_cols + v_cols)

    ga_col = (pool_width + qk_cols + v_cols + o_cols) // d
    x1, x1p = _mixout(hg.reshape(t, heads * dv), branch_a.reshape(t, pool_width), proj, xf,
                      w_b_down.astype(BF16), w_out.astype(BF16), b_gate.reshape(1, 2 * d),
                      ln1_g.reshape(1, d), ln1_b.reshape(1, d), ga_col, alpha)

    ne = w_router.shape[1]
    wr_t = w_router.T
    wr_hi = wr_t.astype(BF16)
    wr_lo = (wr_t - wr_hi.astype(F32)).astype(BF16)
    eidx, wts, rank, cnt = _router(x1, wr_hi, wr_lo, b_router.reshape(ne, 1))

    tile = EXPERT_TILE
    n_tiles = (t * TOP_K) // tile + ne
    counts = cnt[:, 0]
    pcounts = ((counts + tile - 1) // tile) * tile
    pends = jnp.cumsum(pcounts)
    pstarts = pends - pcounts
    pos = _slots(eidx, rank, pstarts)

    n_used = (pends[-1] // tile).astype(I32)
    tile_ids = jnp.minimum(jnp.arange(n_tiles, dtype=I32), n_used - 1)
    tile_expert = jnp.sum((pends[None, :] <= (tile_ids * tile)[:, None]).astype(I32), axis=1)
    tile_expert = jnp.minimum(tile_expert, ne - 1)
    new_expert = jnp.concatenate([jnp.ones((1,), I32), (tile_expert[1:] != tile_expert[:-1]).astype(I32)])
    expert_ord = jnp.cumsum(new_expert) - 1
    candidates = jnp.where(counts > 0, jnp.arange(ne, dtype=I32), ne)
    later_min = lax.cummin(candidates, axis=0, reverse=True)
    next_used = jnp.concatenate([later_min[1:], jnp.full((1,), ne, I32)])
    next_used = jnp.where(next_used >= ne, -1, next_used)
    next_used2 = jnp.where(next_used >= 0, next_used[jnp.maximum(next_used, 0)], -1)
    tile_onehot = tile_expert[:, None] == jnp.arange(ne, dtype=I32)[None, :]
    expert_next = jnp.sum(jnp.where(tile_onehot, next_used[None, :], 0), axis=1).astype(I32)
    expert_next2 = jnp.sum(jnp.where(tile_onehot, next_used2[None, :], 0), axis=1).astype(I32)

    n_slots = n_tiles * tile
    slot_tok = _sc_invert(pos.reshape(-1), n_slots, t)
    xs = _sc_gather(x1p, slot_tok)
    ys = _experts(tile_expert, n_used.reshape(1), expert_ord.astype(I32), expert_next, expert_next2, xs,
                  w_gate_e, w_up_e, w_down_e)
    yg = _sc_gather(ys, pos.reshape(-1)).reshape(TOP_K, t, x1p.shape[1])
    out = _combine(wts.T, x1, x1p, yg, w_gate_s.astype(BF16), w_up_s.astype(BF16),
                   w_down_s.astype(BF16), ln2_g.reshape(1, d), ln2_b.reshape(1, d), alpha)
    return out.reshape(bsz, seq, d)


def kernel(x, w_in, b_if, b_gate, conv_qk, pool_w, pool_scale, mh_norm_w, w_b_down, w_out, ln1_g, ln1_b,
           w_router, b_router, w_gate_e, w_up_e, w_down_e, w_gate_s, w_up_s, w_down_s, ln2_g, ln2_b):
    depth = w_in.shape[0]
    alpha = (2.0 * depth) ** 0.25
    for l in range(depth):
        x = _layer(alpha, x, w_in[l], b_if[l], b_gate[l], conv_qk[l], pool_w[l], pool_scale[l], mh_norm_w[l],
                   w_b_down[l], w_out[l], ln1_g[l], ln1_b[l], w_router[l], b_router[l], w_gate_e[l],
                   w_up_e[l], w_down_e[l], w_gate_s[l], w_up_s[l], w_down_s[l], ln2_g[l], ln2_b[l])
    return x
```

```python
import dataclasses
import functools

import jax
import jax.numpy as jnp
import numpy as np
from jax import lax
from jax.experimental import pallas as pl
from jax.experimental.pallas import tpu as pltpu
from jax.experimental.pallas import tpu_sc as plsc

F32 = jnp.float32
BF16 = jnp.bfloat16
I32 = jnp.int32
U32 = jnp.uint32

N_HEADS = 4
POOL_GROUPS = 4
CONV_WIDTH = 5
LN_EPS = 1e-5
N_ROUTE_GROUPS = 8
TOPK_GROUPS = 4
TOP_K = 8
ROUTED_SCALE = 2.5

LANES = 128
SUBLANES = 8
BF16_ROWS = 16
VMEM_LIMIT = 56 * 1024 * 1024

INPROJ_TILE_M = 2048
INPROJ_TILE_N = 1536
MLSTM_CHUNK = 256
MLSTM_HEADS_PER_STEP = 2
MLSTM_BATCH_PER_STEP = 2
SEQ_TILE = 1024
POOL_SUB = 256
ROW_TILE = 1024
ROUTER_TILE = 1024
EXPERT_TILE = 256
COMBINE_TILE = 512
SLOT_TILE = 2048
WEIGHT_SLOTS = 3
WEIGHT_DMA_PRIORITY = 0
ROW_SLOTS = 3

SC_CORES = 2
SC_SUBCORES = 16
SC_LANES = 16
SC_WORKERS = SC_CORES * SC_SUBCORES
SC_GATHER_ROWS = 64
SC_SCAN_CHUNK = 2048
SC_SCAN_UNROLL = 8

HI_MASK = 0xFFFF0000


def _cparams(sem):
    return pltpu.CompilerParams(dimension_semantics=sem, vmem_limit_bytes=VMEM_LIMIT)


def _sigmoid(x):
    return 1.0 / (1.0 + jnp.exp(-x))


def _layer_norm(y, g, b):
    mu = jnp.mean(y, axis=-1, keepdims=True)
    yc = y - mu
    var = jnp.mean(yc * yc, axis=-1, keepdims=True)
    return yc * lax.rsqrt(var + LN_EPS) * g + b


def _pack_bf16_pairs(y):
    c = y.shape[1] // 2
    bits = lax.bitcast_convert_type(y.astype(BF16).astype(F32), U32)
    return (bits[:, :c] >> 16) | (bits[:, c:] & jnp.uint32(HI_MASK))


def _unpack_bf16_pairs(p):
    lo = lax.bitcast_convert_type(p << 16, F32)
    hi = lax.bitcast_convert_type(p & jnp.uint32(HI_MASK), F32)
    return jnp.concatenate([lo, hi], axis=1)


def _inproj_kernel(x_ref, w_ref, wg_ref, bg_ref, o_ref, g_ref, xb_ref):
    @pl.when(pl.program_id(1) == 0)
    def _():
        xb_ref[...] = x_ref[...].astype(BF16)
        nt_dims = (((1,), (1,)), ((), ()))
        g_ref[0] = lax.dot_general(wg_ref[...], xb_ref[...], nt_dims, preferred_element_type=F32) + bg_ref[...]

    o_ref[...] = jnp.dot(xb_ref[...], w_ref[...], preferred_element_type=F32).astype(o_ref.dtype)


def _inproj(x, w, w_gate_t, bias_gate, seq):
    t, d = x.shape
    n = w.shape[1]
    r = w_gate_t.shape[0]
    tm, tn = INPROJ_TILE_M, INPROJ_TILE_N
    per_seq = seq // tm
    return pl.pallas_call(
        _inproj_kernel,
        grid=(t // tm, n // tn),
        in_specs=[pl.BlockSpec((tm, d), lambda i, j: (i, 0)),
                  pl.BlockSpec((d, tn), lambda i, j: (0, j)),
                  pl.BlockSpec((r, d), lambda i, j: (0, 0)),
                  pl.BlockSpec((r, 1), lambda i, j: (0, 0))],
        out_specs=[pl.BlockSpec((tm, tn), lambda i, j: (i, j)),
                   pl.BlockSpec((1, r, tm), lambda i, j: (i // per_seq, 0, i % per_seq))],
        out_shape=[jax.ShapeDtypeStruct((t, n), BF16), jax.ShapeDtypeStruct((t // seq, r, seq), F32)],
        scratch_shapes=[pltpu.VMEM((tm, d), BF16)],
        compiler_params=_cparams(("parallel", "arbitrary")),
        name="inproj",
    )(x, w, w_gate_t, bias_gate)


def _qkconv_kernel(prev_ref, main_ref, next_ref, w_ref, o_ref, *, ts, scale, transpose):
    t = pl.program_id(1)
    nt = pl.num_programs(1)
    main = main_ref[0].astype(F32)
    prev = prev_ref[0].astype(F32)[BF16_ROWS - SUBLANES:]
    nxt = next_ref[0].astype(F32)[:SUBLANES]
    prev = jnp.where(t > 0, prev, 0.0)
    nxt = jnp.where(t < nt - 1, nxt, 0.0)
    ext = jnp.concatenate([prev, main, nxt], axis=0)
    w = w_ref[...]
    pad = CONV_WIDTH // 2
    acc = jnp.zeros_like(main)
    for j in range(CONV_WIDTH):
        off = SUBLANES - pad + j
        acc = acc + ext[off:off + ts] * w[j:j + 1]
    y = acc * _sigmoid(acc) * scale
    if transpose:
        o_ref[0] = y.T.astype(o_ref.dtype)
    else:
        o_ref[0] = y.astype(o_ref.dtype)


def _qkconv(proj3, conv_w, col0, scale, transpose):
    bsz, seq, _ = proj3.shape
    ts = SEQ_TILE
    nt = seq // ts
    cw = 2 * LANES
    ncol = conv_w.shape[1] // cw
    hb = ts // BF16_ROWS
    n_hb = seq // BF16_ROWS
    kern = functools.partial(_qkconv_kernel, ts=ts, scale=scale, transpose=transpose)
    if transpose:
        out_shape = jax.ShapeDtypeStruct((bsz, ncol * cw, seq), BF16)
        out_spec = pl.BlockSpec((1, cw, ts), lambda b, t, j: (b, j, t))
    else:
        out_shape = jax.ShapeDtypeStruct((bsz, seq, ncol * cw), BF16)
        out_spec = pl.BlockSpec((1, ts, cw), lambda b, t, j: (b, t, j))
    return pl.pallas_call(
        kern,
        grid=(bsz, nt, ncol),
        in_specs=[pl.BlockSpec((1, BF16_ROWS, cw), lambda b, t, j: (b, jnp.maximum(t * hb - 1, 0), col0 + j)),
                  pl.BlockSpec((1, ts, cw), lambda b, t, j: (b, t, col0 + j)),
                  pl.BlockSpec((1, BF16_ROWS, cw),
                               lambda b, t, j: (b, jnp.minimum((t + 1) * hb, n_hb - 1), col0 + j)),
                  pl.BlockSpec((CONV_WIDTH, cw), lambda b, t, j: (0, j))],
        out_specs=out_spec,
        out_shape=out_shape,
        compiler_params=_cparams(("parallel", "parallel", "parallel")),
        name="qkconv_t" if transpose else "qkconv",
    )(proj3, proj3, proj3, conv_w)


def _pool_kernel(prev_ref, main_ref, next_ref, band_ref, pw_ref, ps_ref, o_ref, *, ts, seq, cw):
    t = pl.program_id(1)
    nt = pl.num_programs(1)
    prev = jnp.where(t > 0, prev_ref[0], jnp.zeros_like(prev_ref[0]))
    nxt = jnp.where(t < nt - 1, next_ref[0], jnp.zeros_like(next_ref[0]))
    ext = jnp.concatenate([prev, main_ref[0], nxt], axis=0)
    sub = POOL_SUB
    for g in range(POOL_GROUPS):
        hw = 1 << g
        cols = slice(g * cw, (g + 1) * cw)
        for j in range(ts // sub):
            rows = ext[j * sub:j * sub + sub + 2 * LANES, cols]
            s = jnp.dot(band_ref[g], rows, preferred_element_type=F32)
            tabs = t * ts + j * sub + lax.broadcasted_iota(I32, (sub, 1), 0)
            cnt = jnp.minimum(tabs + hw, seq) - jnp.maximum(tabs - hw, 0)
            centre = rows[LANES:LANES + sub].astype(F32)
            pooled = s / cnt.astype(F32) - centre
            mixed = jnp.dot(pooled.astype(BF16), pw_ref[g], preferred_element_type=F32) * ps_ref[:, cols]
            o_ref[0, j * sub:(j + 1) * sub, cols] = mixed.astype(o_ref.dtype)


def _pool(proj3, pool_w, pool_scale):
    bsz, seq, _ = proj3.shape
    ts = SEQ_TILE
    nt = seq // ts
    cw = pool_w.shape[-1]
    width = POOL_GROUPS * cw
    hb = ts // LANES
    n_hb = seq // LANES
    i = np.arange(POOL_SUB)[:, None]
    c = np.arange(POOL_SUB + 2 * LANES)[None, :] - LANES
    band = np.stack([(c >= i - (1 << g)) & (c < i + (1 << g)) for g in range(POOL_GROUPS)])
    band = jnp.asarray(band.astype(np.float32), BF16)
    kern = functools.partial(_pool_kernel, ts=ts, seq=seq, cw=cw)
    return pl.pallas_call(
        kern,
        grid=(bsz, nt),
        in_specs=[pl.BlockSpec((1, LANES, width), lambda b, t: (b, jnp.maximum(t * hb - 1, 0), 0)),
                  pl.BlockSpec((1, ts, width), lambda b, t: (b, t, 0)),
                  pl.BlockSpec((1, LANES, width), lambda b, t: (b, jnp.minimum((t + 1) * hb, n_hb - 1), 0)),
                  pl.BlockSpec(band.shape, lambda b, t: (0, 0, 0)),
                  pl.BlockSpec(pool_w.shape, lambda b, t: (0, 0, 0)),
                  pl.BlockSpec((1, width), lambda b, t: (0, 0))],
        out_specs=pl.BlockSpec((1, ts, width), lambda b, t: (b, t, 0)),
        out_shape=jax.ShapeDtypeStruct((bsz, seq, width), BF16),
        compiler_params=_cparams(("parallel", "parallel")),
        name="pool",
    )(proj3, proj3, proj3, band, pool_w, pool_scale)


def _dot_split(val_r, mask):
    hi = val_r.astype(BF16)
    lo = (val_r - hi.astype(F32)).astype(BF16)
    return (jnp.dot(hi, mask, preferred_element_type=F32) + jnp.dot(lo, mask, preferred_element_type=F32))


def _dot_split_t(mask, val_r):
    hi = val_r.astype(BF16)
    lo = (val_r - hi.astype(F32)).astype(BF16)
    nt_dims = (((1,), (1,)), ((), ()))
    return (lax.dot_general(mask, hi, nt_dims, preferred_element_type=F32)
            + lax.dot_general(mask, lo, nt_dims, preferred_element_type=F32))


def _mlstm_kernel(q_ref, kt_ref, v_ref, uo_ref, gr_ref, nw_ref, o_ref,
                  cf_ref, nf_ref, mf_ref, cb_ref, nb_ref, mb_ref, cbs_ref, nbs_ref, mbs_ref,
                  *, chunk, n_chunks, heads):
    L = chunk
    nb = q_ref.shape[0]
    dqk = kt_ref.shape[1] // heads
    dv = v_ref.shape[2] // heads
    p = pl.program_id(1)
    c = pl.program_id(2)
    row = lax.broadcasted_iota(I32, (L, L), 0)
    col = lax.broadcasted_iota(I32, (L, L), 1)
    tri_le = row <= col
    tri_ge = row >= col
    m_le = jnp.where(tri_le, 1.0, 0.0).astype(BF16)
    m_ge = jnp.where(tri_ge, 1.0, 0.0).astype(BF16)
    lane_r = lax.broadcasted_iota(I32, (1, L), 1)
    neg_inf = jnp.float32(-jnp.inf)

    chains = [(bb, hd) for bb in range(nb) for hd in range(heads)]
    hs = range(len(chains))
    kts = [kt_ref[bb, hd * dqk:(hd + 1) * dqk, :] for bb, hd in chains]
    vs = [v_ref[bb, :, hd * dv:(hd + 1) * dv] for bb, hd in chains]
    gates = [gr_ref[bb, hd * SUBLANES:(hd + 1) * SUBLANES, :] for bb, hd in chains]
    lf_r = [jax.nn.log_sigmoid(g) for g in gates]

    def update_state(c_ref, n_ref, m_ref, g_r, tot):
        m_prev = [m_ref[hh] for hh in hs]
        m_new = [jnp.maximum(tot[hh] + m_prev[hh], jnp.max(g_r[hh], axis=1, keepdims=True)) for hh in hs]
        decay = [jnp.exp(tot[hh] + m_prev[hh] - m_new[hh]) for hh in hs]
        kw = [kts[hh].astype(F32) * jnp.exp(g_r[hh] - m_new[hh]) for hh in hs]
        upd = [jnp.dot(kw[hh].astype(BF16), vs[hh], preferred_element_type=F32) for hh in hs]
        for hh in hs:
            c_ref[hh] = decay[hh] * c_ref[hh] + upd[hh]
            n_ref[hh] = decay[hh] * n_ref[hh] + jnp.sum(kw[hh], axis=1, keepdims=True)
            m_ref[hh] = m_new[hh]

    @pl.when(p == 0)
    def _backward_states():
        @pl.when(c == 0)
        def _():
            cb_ref[...] = jnp.zeros_like(cb_ref)
            nb_ref[...] = jnp.zeros_like(nb_ref)
            mb_ref[...] = jnp.zeros_like(mb_ref)

        cc = n_chunks - 1 - c
        cbs_ref[cc] = cb_ref[...].astype(BF16)
        nbs_ref[cc] = nb_ref[...]
        mbs_ref[cc] = mb_ref[...]
        a_r = [_dot_split(lf_r[hh], m_ge)[3:4] for hh in hs]
        a0 = [jnp.sum(jnp.where(lane_r == 0, a_r[hh], 0.0), axis=1, keepdims=True) for hh in hs]
        g_r = [a0[hh] - a_r[hh] + gates[hh][2:3] for hh in hs]
        update_state(cb_ref, nb_ref, mb_ref, g_r, a0)

    @pl.when(p == 1)
    def _outputs():
        @pl.when(c == 0)
        def _():
            cf_ref[...] = jnp.zeros_like(cf_ref)
            nf_ref[...] = jnp.zeros_like(nf_ref)
            mf_ref[...] = jnp.zeros_like(mf_ref)

        nb_in = nbs_ref[c]
        mb_in = mbs_ref[c]
        cb_in = cbs_ref[c]
        qs = [q_ref[bb, :, hd * dqk:(hd + 1) * dqk] for bb, hd in chains]
        b_r = [_dot_split(lf_r[hh], m_le)[1:2] for hh in hs]
        a_r = [_dot_split(lf_r[hh], m_ge)[3:4] for hh in hs]
        b_c = [_dot_split_t(m_ge, lf_r[hh])[:, 1:2] for hh in hs]
        a_c = [_dot_split_t(m_le, lf_r[hh])[:, 3:4] for hh in hs]
        li_f = [gates[hh][0:1] for hh in hs]
        li_b = [gates[hh][2:3] for hh in hs]

        s = [jnp.dot(qs[hh], kts[hh], preferred_element_type=F32) for hh in hs]
        nlane = lax.broadcasted_iota(I32, (dqk, LANES), 1)
        nmat = [jnp.where(nlane == 0, nf_ref[hh], jnp.where(nlane == 1, nb_in[hh], 0.0)).astype(BF16)
                for hh in hs]
        qn = [jnp.dot(qs[hh], nmat[hh], preferred_element_type=F32) for hh in hs]

        def direction(d, mask, cum_c, m_prev, qn_col):
            d = [jnp.where(mask, d[hh], neg_inf) for hh in hs]
            m_inter = [cum_c[hh] + m_prev[hh] for hh in hs]
            m_t = [jnp.maximum(m_inter[hh], jnp.max(d[hh], axis=1, keepdims=True)) for hh in hs]
            pmat = [jnp.exp(d[hh] - m_t[hh]) * s[hh] for hh in hs]
            w_inter = [jnp.exp(m_inter[hh] - m_t[hh]) for hh in hs]
            den = [jnp.sum(pmat[hh], axis=1, keepdims=True) + w_inter[hh] * qn_col[hh] for hh in hs]
            r = [1.0 / jnp.maximum(jnp.abs(den[hh]), jnp.exp(-m_t[hh])) for hh in hs]
            return [pmat[hh] * r[hh] for hh in hs], [w_inter[hh] * r[hh] for hh in hs]

        pf, sf = direction([b_c[hh] - (b_r[hh] - li_f[hh]) for hh in hs], tri_ge, b_c,
                           [mf_ref[hh] for hh in hs], [qn[hh][:, 0:1] for hh in hs])
        pb, sb = direction([a_c[hh] - (a_r[hh] - li_b[hh]) for hh in hs], tri_le, a_c,
                           [mb_in[hh] for hh in hs], [qn[hh][:, 1:2] for hh in hs])
        qf = [qs[hh].astype(F32) for hh in hs]
        h = [jnp.dot((pf[hh] + pb[hh]).astype(BF16), vs[hh], preferred_element_type=F32)
             + jnp.dot((qf[hh] * sf[hh]).astype(BF16), cf_ref[hh].astype(BF16), preferred_element_type=F32)
             + jnp.dot((qf[hh] * sb[hh]).astype(BF16), cb_in[hh], preferred_element_type=F32) for hh in hs]

        mu = [jnp.mean(h[hh], axis=1, keepdims=True) for hh in hs]
        hc = [h[hh] - mu[hh] for hh in hs]
        var = [jnp.mean(hc[hh] * hc[hh], axis=1, keepdims=True) for hh in hs]
        hn = [hc[hh] * lax.rsqrt(var[hh] + LN_EPS) * nw_ref[chains[hh][1]] for hh in hs]
        for hh, (bb, hd) in enumerate(chains):
            gate_o = _sigmoid(uo_ref[bb, :, hd * dv:(hd + 1) * dv].astype(F32))
            o_ref[bb, :, hd * dv:(hd + 1) * dv] = (gate_o * hn[hh]).astype(o_ref.dtype)

        b_last = [jnp.sum(jnp.where(lane_r == L - 1, b_r[hh], 0.0), axis=1, keepdims=True) for hh in hs]
        update_state(cf_ref, nf_ref, mf_ref, [b_last[hh] - b_r[hh] + li_f[hh] for hh in hs], b_last)


def _mlstm(q, kt, proj3, gates, norm_w, v_off, o_off):
    bsz, seq, qw = q.shape
    dqk = qw // N_HEADS
    dv = norm_w.shape[-1]
    L = MLSTM_CHUNK
    nc = seq // L
    hg = MLSTM_HEADS_PER_STEP
    bg = MLSTM_BATCH_PER_STEP
    groups = N_HEADS // hg
    assert N_HEADS % hg == 0 and bsz % bg == 0 and v_off % (hg * dv) == 0 and o_off % (hg * dv) == 0
    v_blk0 = v_off // (hg * dv)
    o_blk0 = o_off // (hg * dv)
    kern = functools.partial(_mlstm_kernel, chunk=L, n_chunks=nc, heads=hg)

    def chunk_of(p, c):
        return jnp.where(p == 0, nc - 1 - c, c)

    def out_chunk(p, c):
        return jnp.where(p == 0, 0, c)

    return pl.pallas_call(
        kern,
        grid=((bsz // bg) * groups, 2, nc),
        in_specs=[
            pl.BlockSpec((bg, L, hg * dqk), lambda g, p, c: (g // groups, out_chunk(p, c), g % groups)),
            pl.BlockSpec((bg, hg * dqk, L), lambda g, p, c: (g // groups, g % groups, chunk_of(p, c))),
            pl.BlockSpec((bg, L, hg * dv), lambda g, p, c: (g // groups, chunk_of(p, c), v_blk0 + g % groups)),
            pl.BlockSpec((bg, L, hg * dv), lambda g, p, c: (g // groups, out_chunk(p, c), o_blk0 + g % groups)),
            pl.BlockSpec((bg, hg * SUBLANES, L), lambda g, p, c: (g // groups, g % groups, chunk_of(p, c))),
            pl.BlockSpec((hg, 1, dv), lambda g, p, c: (g % groups, 0, 0)),
        ],
        out_specs=pl.BlockSpec((bg, L, hg * dv), lambda g, p, c: (g // groups, out_chunk(p, c), g % groups)),
        out_shape=jax.ShapeDtypeStruct((bsz, seq, N_HEADS * dv), BF16),
        scratch_shapes=[
            pltpu.VMEM((bg * hg, dqk, dv), F32), pltpu.VMEM((bg * hg, dqk, 1), F32),
            pltpu.VMEM((bg * hg, 1, 1), F32),
            pltpu.VMEM((bg * hg, dqk, dv), F32), pltpu.VMEM((bg * hg, dqk, 1), F32),
            pltpu.VMEM((bg * hg, 1, 1), F32),
            pltpu.VMEM((nc, bg * hg, dqk, dv), BF16), pltpu.VMEM((nc, bg * hg, dqk, 1), F32),
            pltpu.VMEM((nc, bg * hg, 1, 1), F32),
        ],
        compiler_params=_cparams(("parallel", "arbitrary", "arbitrary")),
        name="mlstm",
    )(q, kt, proj3, proj3, gates, norm_w)


def _mixout_kernel(hg_ref, a_ref, uga_ref, ugb_ref, x_ref, wbd_ref, wo_ref, bga_ref, bgb_ref,
                   g_ref, b_ref, o_ref, op_ref, *, alpha):
    branch_b = jnp.dot(hg_ref[...], wbd_ref[...], preferred_element_type=F32)
    ga = _sigmoid(uga_ref[...].astype(F32) + bga_ref[...])
    gb = _sigmoid(ugb_ref[...].astype(F32) + bgb_ref[...])
    merged = ga * a_ref[...].astype(F32) + gb * branch_b
    mix = jnp.dot(merged.astype(BF16), wo_ref[...], preferred_element_type=F32)
    y = _layer_norm(alpha * x_ref[...] + mix, g_ref[...], b_ref[...])
    o_ref[...] = y
    op_ref[...] = _pack_bf16_pairs(y)


def _mixout(hg, branch_a, proj, x, w_b_down, w_out, b_gate, ln_g, ln_b, ga_col, alpha):
    t, d = x.shape
    tm = ROW_TILE
    inner = hg.shape[1]
    row = lambda i: (i, 0)
    const = lambda i: (0, 0)
    return pl.pallas_call(
        functools.partial(_mixout_kernel, alpha=alpha),
        grid=(t // tm,),
        in_specs=[pl.BlockSpec((tm, inner), row),
                  pl.BlockSpec((tm, d), row),
                  pl.BlockSpec((tm, d), lambda i: (i, ga_col)),
                  pl.BlockSpec((tm, d), lambda i: (i, ga_col + 1)),
                  pl.BlockSpec((tm, d), row),
                  pl.BlockSpec((inner, d), const),
                  pl.BlockSpec((d, d), const),
                  pl.BlockSpec((1, d), const),
                  pl.BlockSpec((1, d), lambda i: (0, 1)),
                  pl.BlockSpec((1, d), const),
                  pl.BlockSpec((1, d), const)],
        out_specs=[pl.BlockSpec((tm, d), row), pl.BlockSpec((tm, d // 2), row)],
        out_shape=[jax.ShapeDtypeStruct((t, d), F32), jax.ShapeDtypeStruct((t, d // 2), U32)],
        compiler_params=_cparams(("parallel",)),
        name="mixout",
    )(hg, branch_a, proj, proj, x, w_b_down, w_out, b_gate, b_gate, ln_g, ln_b)


def _router_kernel(x_ref, wh_ref, wl_ref, br_ref, eidx_ref, wts_ref, rank_ref, cnt_ref, run_ref,
                   *, n_experts):
    i = pl.program_id(0)
    tm = x_ref.shape[0]
    ne = n_experts
    per_group = ne // N_ROUTE_GROUPS

    @pl.when(i == 0)
    def _():
        run_ref[...] = jnp.zeros_like(run_ref)

    x = x_ref[...]
    xh = x.astype(BF16)
    xl = (x - xh.astype(F32)).astype(BF16)
    nt_dims = (((1,), (1,)), ((), ()))
    logits = (lax.dot_general(wh_ref[...], xh, nt_dims, preferred_element_type=F32)
              + lax.dot_general(wh_ref[...], xl, nt_dims, preferred_element_type=F32)
              + lax.dot_general(wl_ref[...], xh, nt_dims, preferred_element_type=F32))
    scores = _sigmoid(logits)
    sel = scores + br_ref[...]
    neg = jnp.float32(-jnp.inf)
    big = jnp.float32(1e9)
    rowi = lax.broadcasted_iota(I32, (ne, tm), 0).astype(F32)

    gi = lax.broadcasted_iota(I32, (N_ROUTE_GROUPS, tm), 0).astype(F32)
    work = jnp.zeros((N_ROUTE_GROUPS, tm), F32)
    for g in range(N_ROUTE_GROUPS):
        blk = sel[g * per_group:(g + 1) * per_group]
        ri = lax.broadcasted_iota(I32, (per_group, tm), 0).astype(F32) + float(g * per_group)
        m1 = jnp.max(blk, axis=0, keepdims=True)
        i1 = jnp.min(jnp.where(blk == m1, ri, big), axis=0, keepdims=True)
        m2 = jnp.max(jnp.where(ri == i1, neg, blk), axis=0, keepdims=True)
        work = jnp.where(gi == float(g), m1 + m2, work)
    row_group = jnp.floor(rowi * (1.0 / per_group))
    allowed = jnp.zeros((ne, tm), F32)
    for _ in range(TOPK_GROUPS):
        m = jnp.max(work, axis=0, keepdims=True)
        idx = jnp.min(jnp.where(work == m, gi, big), axis=0, keepdims=True)
        work = jnp.where(gi == idx, neg, work)
        allowed = jnp.where(row_group == idx, 1.0, allowed)
    selm = jnp.where(allowed > 0.5, sel, neg)

    member = jnp.zeros((ne, tm), F32)
    idxs, wks = [], []
    for _ in range(TOP_K):
        m = jnp.max(selm, axis=0, keepdims=True)
        idx = jnp.min(jnp.where(selm == m, rowi, big), axis=0, keepdims=True)
        hit = rowi == idx
        wks.append(jnp.sum(jnp.where(hit, scores, 0.0), axis=0, keepdims=True))
        idxs.append(idx)
        member = jnp.where(hit, 1.0, member)
        selm = jnp.where(hit, neg, selm)
    wsum = wks[0]
    for wk in wks[1:]:
        wsum = wsum + wk

    ti = lax.broadcasted_iota(I32, (tm, tm), 0)
    tj = lax.broadcasted_iota(I32, (tm, tm), 1)
    strict = jnp.where(ti < tj, 1.0, 0.0).astype(BF16)
    prefix = jnp.dot(member.astype(BF16), strict, preferred_element_type=F32) + run_ref[...]
    ranks = [jnp.sum(jnp.where(rowi == idx, prefix, 0.0), axis=0, keepdims=True) for idx in idxs]
    run_new = run_ref[...] + jnp.sum(member, axis=1, keepdims=True)
    run_ref[...] = run_new

    eidx_ref[...] = jnp.concatenate(idxs, axis=0).astype(I32)
    wts_ref[...] = jnp.concatenate([wk / wsum * ROUTED_SCALE for wk in wks], axis=0)
    rank_ref[...] = jnp.concatenate(ranks, axis=0).astype(I32)
    cnt_ref[...] = jnp.broadcast_to(run_new, cnt_ref.shape).astype(I32)


def _router(x1, wr_hi, wr_lo, b_router):
    t, d = x1.shape
    ne = wr_hi.shape[0]
    tm = ROUTER_TILE
    kern = functools.partial(_router_kernel, n_experts=ne)
    tok = lambda i: (0, i)
    const = lambda i: (0, 0)
    return pl.pallas_call(
        kern,
        grid=(t // tm,),
        in_specs=[pl.BlockSpec((tm, d), lambda i: (i, 0)),
                  pl.BlockSpec((ne, d), const),
                  pl.BlockSpec((ne, d), const),
                  pl.BlockSpec((ne, 1), const)],
        out_specs=[pl.BlockSpec((TOP_K, tm), tok), pl.BlockSpec((TOP_K, tm), tok),
                   pl.BlockSpec((TOP_K, tm), tok), pl.BlockSpec((ne, LANES), const)],
        out_shape=[jax.ShapeDtypeStruct((TOP_K, t), I32), jax.ShapeDtypeStruct((TOP_K, t), F32),
                   jax.ShapeDtypeStruct((TOP_K, t), I32), jax.ShapeDtypeStruct((ne, LANES), I32)],
        scratch_shapes=[pltpu.VMEM((ne, 1), F32)],
        compiler_params=_cparams(("arbitrary",)),
        name="router",
    )(x1, wr_hi, wr_lo, b_router)


def _slots_kernel(eidx_ref, rank_ref, ps_ref, pos_ref, *, n_experts):
    tm = eidx_ref.shape[1]
    rowi = lax.broadcasted_iota(I32, (n_experts, tm), 0)
    eidx = eidx_ref[...]
    starts = ps_ref[...]
    base = [jnp.sum(jnp.where(rowi == eidx[k:k + 1], starts, 0.0), axis=0, keepdims=True)
            for k in range(TOP_K)]
    pos_ref[...] = jnp.concatenate(base, axis=0).astype(I32) + rank_ref[...]


def _slots(eidx, rank, pstarts):
    k, t = eidx.shape
    ne = pstarts.shape[0]
    tm = SLOT_TILE
    tok = lambda i: (0, i)
    return pl.pallas_call(
        functools.partial(_slots_kernel, n_experts=ne),
        grid=(t // tm,),
        in_specs=[pl.BlockSpec((k, tm), tok), pl.BlockSpec((k, tm), tok),
                  pl.BlockSpec((ne, 1), lambda i: (0, 0))],
        out_specs=pl.BlockSpec((k, tm), tok),
        out_shape=jax.ShapeDtypeStruct((k, t), I32),
        compiler_params=_cparams(("parallel",)),
        name="slots",
    )(eidx, rank, pstarts.astype(F32).reshape(ne, 1))


def _sc_worker_id():
    return lax.axis_index("subcore") * SC_CORES + lax.axis_index("core")


def _sc_mesh():
    return plsc.VectorSubcoreMesh(core_axis_name="core", subcore_axis_name="subcore")


def _sc_vector_params():
    cp = pltpu.CompilerParams()
    if "needs_layout_passes" in pltpu.CompilerParams.__dataclass_fields__:
        cp = dataclasses.replace(cp, needs_layout_passes=False)
    return cp


def _sc_gather(table, idx):
    n = idx.shape[0]
    w = table.shape[1]
    ch = SC_GATHER_ROWS
    n_ch = n // (SC_WORKERS * ch)
    assert n % (SC_WORKERS * ch * 2) == 0

    @functools.partial(
        pl.kernel, mesh=_sc_mesh(),
        out_type=jax.ShapeDtypeStruct((n, w), table.dtype),
        scratch_types=[pltpu.VMEM((n_ch, ch), I32), pltpu.VMEM((2, ch, w), table.dtype),
                       pltpu.SemaphoreType.DMA((2,))],
    )
    def kern(table_hbm, idx_hbm, out_hbm, idx_v, rows_v, sem):
        first = _sc_worker_id() * n_ch
        pltpu.sync_copy(idx_hbm.at[pl.ds(first, n_ch)], idx_v)

        def gather(j, b):
            return pltpu.make_async_copy(table_hbm.at[idx_v.at[j]], rows_v.at[b], sem.at[b])

        gather(0, 0).start()

        @pl.loop(0, n_ch, step=2)
        def _(j):
            for b in range(2):
                jj = j + b

                @pl.when(jj + 1 < n_ch)
                def _():
                    gather(jj + 1, 1 - b).start()

                gather(jj, b).wait()
                row0 = pl.multiple_of((first + jj) * ch, ch)
                pltpu.sync_copy(rows_v.at[b], out_hbm.at[pl.ds(row0, ch)])

    return kern(table, idx.reshape(n // ch, ch))


def _sc_invert(pos_flat, n_slots, n_tokens):
    n = pos_flat.shape[0]
    per_w = n_slots // SC_WORKERS
    chunk = SC_SCAN_CHUNK
    assert n_slots % (SC_WORKERS * SC_LANES) == 0 and n % chunk == 0 and n_tokens % chunk == 0

    @functools.partial(
        pl.kernel, mesh=_sc_mesh(),
        out_type=jax.ShapeDtypeStruct((n_slots,), I32),
        scratch_types=[pltpu.VMEM((per_w,), I32), pltpu.VMEM((chunk,), I32)],
        compiler_params=_sc_vector_params(),
    )
    def kern(pos_hbm, out_hbm, table_v, pos_v):
        lo = _sc_worker_id() * per_w
        lane = lax.iota(I32, SC_LANES)

        @pl.loop(0, per_w // SC_LANES)
        def _(i):
            off = pl.multiple_of(i * SC_LANES, SC_LANES)
            table_v[pl.ds(off, SC_LANES)] = lax.rem(lo + off + lane, jnp.full((SC_LANES,), n_tokens, I32))

        @pl.loop(0, n // chunk)
        def _(c):
            pltpu.sync_copy(pos_hbm.at[pl.ds(pl.multiple_of(c * chunk, chunk), chunk)], pos_v)
            tok0 = lax.rem(c, n_tokens // chunk) * chunk

            @plsc.parallel_loop(0, chunk // SC_LANES, 1, unroll=SC_SCAN_UNROLL)
            def _(j):
                off = pl.multiple_of(j * SC_LANES, SC_LANES)
                local = pos_v[pl.ds(off, SC_LANES)] - lo
                mine = (local >= 0) & (local < per_w)
                plsc.store_scatter(table_v, [jnp.where(mine, local, 0)], tok0 + off + lane, mask=mine)

        pltpu.sync_copy(table_v, out_hbm.at[pl.ds(pl.multiple_of(lo, SC_LANES), per_w)])

    return kern(pos_flat)


def _experts_kernel(te_ref, nu_ref, ord_ref, nxt_ref, nxt2_ref, xs_hbm, wg_hbm, wu_hbm, wd_hbm, ys_ref,
                    xbuf_ref, wgf_ref, wuf_ref, wdf_ref, wgb_ref, wub_ref, wdb_ref, sem_ref, xsem_ref):
    i = pl.program_id(0)
    n_used = nu_ref[0]
    tile = xbuf_ref.shape[1]

    def row_copy(step):
        slot = lax.rem(step, ROW_SLOTS)
        src = xs_hbm.at[pl.ds(pl.multiple_of(step * tile, tile), tile)]
        return pltpu.make_async_copy(src, xbuf_ref.at[slot], xsem_ref.at[slot])

    def weight_copies(e, slot):
        return (pltpu.make_async_copy(wg_hbm.at[e], wgf_ref.at[slot], sem_ref.at[slot]),
                pltpu.make_async_copy(wu_hbm.at[e], wuf_ref.at[slot], sem_ref.at[slot]),
                pltpu.make_async_copy(wd_hbm.at[e], wdf_ref.at[slot], sem_ref.at[slot]))

    @pl.when(i < n_used)
    def _():
        @pl.when(i == 0)
        def _():
            for ahead in range(ROW_SLOTS - 1):
                @pl.when(ahead < n_used)
                def _():
                    row_copy(ahead).start()

        @pl.when(i + (ROW_SLOTS - 1) < n_used)
        def _():
            row_copy(i + (ROW_SLOTS - 1)).start()

        e = te_ref[i]
        ordinal = ord_ref[i]
        slot = lax.rem(ordinal, WEIGHT_SLOTS)
        first_tile_of_expert = jnp.logical_or(i == 0, e != te_ref[jnp.maximum(i - 1, 0)])

        @pl.when(i == 0)
        def _():
            for cp in weight_copies(e, slot):
                cp.start(priority=WEIGHT_DMA_PRIORITY)

            @pl.when(nxt_ref[i] >= 0)
            def _():
                for cp in weight_copies(nxt_ref[i], lax.rem(ordinal + 1, WEIGHT_SLOTS)):
                    cp.start(priority=WEIGHT_DMA_PRIORITY)

        @pl.when(first_tile_of_expert)
        def _():
            ahead = nxt2_ref[i]

            @pl.when(ahead >= 0)
            def _():
                for cp in weight_copies(ahead, lax.rem(ordinal + 2, WEIGHT_SLOTS)):
                    cp.start(priority=WEIGHT_DMA_PRIORITY)

            for cp in weight_copies(e, slot):
                cp.wait()
            wgb_ref[...] = wgf_ref[slot].astype(BF16)
            wub_ref[...] = wuf_ref[slot].astype(BF16)
            wdb_ref[...] = wdf_ref[slot].astype(BF16)

        row_copy(i).wait()
        x = _unpack_bf16_pairs(xbuf_ref[lax.rem(i, ROW_SLOTS)]).astype(BF16)
        gate = jnp.dot(x, wgb_ref[...], preferred_element_type=F32)
        up = jnp.dot(x, wub_ref[...], preferred_element_type=F32)
        hid = (gate * _sigmoid(gate) * up).astype(BF16)
        y = jnp.dot(hid, wdb_ref[...], preferred_element_type=F32)
        ys_ref[...] = _pack_bf16_pairs(y)

    @pl.when(i >= n_used)
    def _():
        ys_ref[...] = jnp.zeros_like(ys_ref)


def _experts(tile_expert, n_used, expert_ord, expert_next, expert_next2, xs, w_gate_e, w_up_e, w_down_e):
    n_slots, w = xs.shape
    tile = EXPERT_TILE
    n_tiles = n_slots // tile
    _, d, de = w_gate_e.shape
    grid_spec = pltpu.PrefetchScalarGridSpec(
        num_scalar_prefetch=5,
        grid=(n_tiles,),
        in_specs=[pl.BlockSpec(memory_space=pl.ANY),
                  pl.BlockSpec(memory_space=pl.ANY),
                  pl.BlockSpec(memory_space=pl.ANY),
                  pl.BlockSpec(memory_space=pl.ANY)],
        out_specs=pl.BlockSpec((tile, w), lambda i, *_: (i, 0)),
        scratch_shapes=[pltpu.VMEM((ROW_SLOTS, tile, w), U32),
                        pltpu.VMEM((WEIGHT_SLOTS, d, de), F32), pltpu.VMEM((WEIGHT_SLOTS, d, de), F32),
                        pltpu.VMEM((WEIGHT_SLOTS, de, d), F32),
                        pltpu.VMEM((d, de), BF16), pltpu.VMEM((d, de), BF16), pltpu.VMEM((de, d), BF16),
                        pltpu.SemaphoreType.DMA((WEIGHT_SLOTS,)), pltpu.SemaphoreType.DMA((ROW_SLOTS,))],
    )
    return pl.pallas_call(
        _experts_kernel,
        grid_spec=grid_spec,
        out_shape=jax.ShapeDtypeStruct((n_slots, w), U32),
        compiler_params=_cparams(("arbitrary",)),
        name="experts",
    )(tile_expert, n_used, expert_ord, expert_next, expert_next2, xs, w_gate_e, w_up_e, w_down_e)


def _combine_kernel(wts_ref, x_ref, xp_ref, yg_ref, wgs_ref, wus_ref, wds_ref, g_ref, b_ref, o_ref, *, alpha):
    wts = wts_ref[...]
    routed = jnp.zeros(x_ref.shape, F32)
    for k in range(TOP_K):
        routed = routed + wts[:, k:k + 1] * _unpack_bf16_pairs(yg_ref[k])
    xb = _unpack_bf16_pairs(xp_ref[...]).astype(BF16)
    gate = jnp.dot(xb, wgs_ref[...], preferred_element_type=F32)
    up = jnp.dot(xb, wus_ref[...], preferred_element_type=F32)
    hid = (gate * _sigmoid(gate) * up).astype(BF16)
    shared = jnp.dot(hid, wds_ref[...], preferred_element_type=F32)
    o_ref[...] = _layer_norm(alpha * x_ref[...] + (routed + shared), g_ref[...], b_ref[...])


def _combine(wts_c, x1, x1p, yg, w_gate_s, w_up_s, w_down_s, ln_g, ln_b, alpha):
    t, d = x1.shape
    w = x1p.shape[1]
    tb = COMBINE_TILE
    ds = w_gate_s.shape[1]
    row = lambda i: (i, 0)
    const = lambda i: (0, 0)
    return pl.pallas_call(
        functools.partial(_combine_kernel, alpha=alpha),
        grid=(t // tb,),
        in_specs=[pl.BlockSpec((tb, TOP_K), row),
                  pl.BlockSpec((tb, d), row),
                  pl.BlockSpec((tb, w), row),
                  pl.BlockSpec((TOP_K, tb, w), lambda i: (0, i, 0)),
                  pl.BlockSpec((d, ds), const),
                  pl.BlockSpec((d, ds), const),
                  pl.BlockSpec((ds, d), const),
                  pl.BlockSpec((1, d), const),
                  pl.BlockSpec((1, d), const)],
        out_specs=pl.BlockSpec((tb, d), row),
        out_shape=jax.ShapeDtypeStruct((t, d), F32),
        compiler_params=_cparams(("parallel",)),
        name="combine",
    )(wts_c, x1, x1p, yg, w_gate_s, w_up_s, w_down_s, ln_g, ln_b)


def _layer(alpha, x, w_in, b_if, b_gate, conv_qk, pool_w, pool_scale, mh_norm_w, w_b_down, w_out,
           ln1_g, ln1_b, w_router, b_router, w_gate_e, w_up_e, w_down_e,
           w_gate_s, w_up_s, w_down_s, ln2_g, ln2_b):
    bsz, seq, d = x.shape
    t = bsz * seq
    heads = N_HEADS
    pool_width = pool_w.shape[0] * pool_w.shape[1]
    qk_cols = conv_qk.shape[1]
    v_cols = mh_norm_w.shape[0] * mh_norm_w.shape[1]
    o_cols = v_cols
    if_cols = b_if.shape[0]
    dv = mh_norm_w.shape[1]
    dqk = qk_cols // (2 * heads)
    off_if = pool_width + qk_cols + v_cols + o_cols
    off_gate = off_if + if_cols

    xf = x.reshape(t, d)
    w_main = jnp.concatenate([w_in[:, :off_if], w_in[:, off_gate:]], axis=1).astype(BF16)
    n_gate = if_cols // heads
    w_if = w_in[:, off_if:off_gate].reshape(d, n_gate, heads).transpose(2, 1, 0)
    w_if = jnp.pad(w_if, ((0, 0), (0, SUBLANES - n_gate), (0, 0))).reshape(heads * SUBLANES, d).astype(BF16)
    bias_if = jnp.pad(b_if.reshape(n_gate, heads).T, ((0, 0), (0, SUBLANES - n_gate))).reshape(heads * SUBLANES, 1)
    n_main = w_main.shape[1]

    proj, gates = _inproj(xf, w_main, w_if, bias_if, seq)
    proj3 = proj.reshape(bsz, seq, n_main)

    branch_a = _pool(proj3, pool_w.astype(BF16), pool_scale.reshape(1, pool_width))

    cw = 2 * LANES
    q_col0 = pool_width // cw
    half = qk_cols // 2
    q = _qkconv(proj3, conv_qk[:, :half], q_col0, float(dqk) ** -0.5, False)
    kt = _qkconv(proj3, conv_qk[:, half:], q_col0 + half // cw, 1.0, True)

    hg = _mlstm(q, kt, proj3, gates, mh_norm_w.reshape(heads, 1, dv),
                pool_width + qk_cols, pool_width + qk_cols + v_cols)

    ga_col = (pool_width + qk_cols + v_cols + o_cols) // d
    x1, x1p = _mixout(hg.reshape(t, heads * dv), branch_a.reshape(t, pool_width), proj, xf,
                      w_b_down.astype(BF16), w_out.astype(BF16), b_gate.reshape(1, 2 * d),
                      ln1_g.reshape(1, d), ln1_b.reshape(1, d), ga_col, alpha)

    ne = w_router.shape[1]
    wr_t = w_router.T
    wr_hi = wr_t.astype(BF16)
    wr_lo = (wr_t - wr_hi.astype(F32)).astype(BF16)
    eidx, wts, rank, cnt = _router(x1, wr_hi, wr_lo, b_router.reshape(ne, 1))

    tile = EXPERT_TILE
    n_tiles = (t * TOP_K) // tile + ne
    counts = cnt[:, 0]
    pcounts = ((counts + tile - 1) // tile) * tile
    pends = jnp.cumsum(pcounts)
    pstarts = pends - pcounts
    pos = _slots(eidx, rank, pstarts)

    n_used = (pends[-1] // tile).astype(I32)
    tile_ids = jnp.minimum(jnp.arange(n_tiles, dtype=I32), n_used - 1)
    tile_expert = jnp.sum((pends[None, :] <= (tile_ids * tile)[:, None]).astype(I32), axis=1)
    tile_expert = jnp.minimum(tile_expert, ne - 1)
    new_expert = jnp.concatenate([jnp.ones((1,), I32), (tile_expert[1:] != tile_expert[:-1]).astype(I32)])
    expert_ord = jnp.cumsum(new_expert) - 1
    candidates = jnp.where(counts > 0, jnp.arange(ne, dtype=I32), ne)
    later_min = lax.cummin(candidates, axis=0, reverse=True)
    next_used = jnp.concatenate([later_min[1:], jnp.full((1,), ne, I32)])
    next_used = jnp.where(next_used >= ne, -1, next_used)
    next_used2 = jnp.where(next_used >= 0, next_used[jnp.maximum(next_used, 0)], -1)
    tile_onehot = tile_expert[:, None] == jnp.arange(ne, dtype=I32)[None, :]
    expert_next = jnp.sum(jnp.where(tile_onehot, next_used[None, :], 0), axis=1).astype(I32)
    expert_next2 = jnp.sum(jnp.where(tile_onehot, next_used2[None, :], 0), axis=1).astype(I32)

    n_slots = n_tiles * tile
    slot_tok = _sc_invert(pos.reshape(-1), n_slots, t)
    xs = _sc_gather(x1p, slot_tok)
    ys = _experts(tile_expert, n_used.reshape(1), expert_ord.astype(I32), expert_next, expert_next2, xs,
                  w_gate_e, w_up_e, w_down_e)
    yg = _sc_gather(ys, pos.reshape(-1)).reshape(TOP_K, t, x1p.shape[1])
    out = _combine(wts.T, x1, x1p, yg, w_gate_s.astype(BF16), w_up_s.astype(BF16),
                   w_down_s.astype(BF16), ln2_g.reshape(1, d), ln2_b.reshape(1, d), alpha)
    return out.reshape(bsz, seq, d)


def kernel(x, w_in, b_if, b_gate, conv_qk, pool_w, pool_scale, mh_norm_w, w_b_down, w_out, ln1_g, ln1_b,
           w_router, b_router, w_gate_e, w_up_e, w_down_e, w_gate_s, w_up_s, w_down_s, ln2_g, ln2_b):
    depth = w_in.shape[0]
    alpha = (2.0 * depth) ** 0.25
    for l in range(depth):
        x = _layer(alpha, x, w_in[l], b_if[l], b_gate[l], conv_qk[l], pool_w[l], pool_scale[l], mh_norm_w[l],
                   w_b_down[l], w_out[l], ln1_g[l], ln1_b[l], w_router[l], b_router[l], w_gate_e[l],
                   w_up_e[l], w_down_e[l], w_gate_s[l], w_up_s[l], w_down_s[l], ln2_g[l], ln2_b[l])
    return x
```

```python
import dataclasses
import functools

import jax
import jax.numpy as jnp
import numpy as np
from jax import lax
from jax.experimental import pallas as pl
from jax.experimental.pallas import tpu as pltpu
from jax.experimental.pallas import tpu_sc as plsc

F32 = jnp.float32
BF16 = jnp.bfloat16
I32 = jnp.int32
U32 = jnp.uint32

N_HEADS = 4
POOL_GROUPS = 4
CONV_WIDTH = 5
LN_EPS = 1e-5
N_ROUTE_GROUPS = 8
TOPK_GROUPS = 4
TOP_K = 8
ROUTED_SCALE = 2.5

LANES = 128
SUBLANES = 8
BF16_ROWS = 16
VMEM_LIMIT = 56 * 1024 * 1024

INPROJ_TILE_M = 2048
INPROJ_TILE_N = 1536
MLSTM_CHUNK = 256
MLSTM_HEADS_PER_STEP = 2
MLSTM_BATCH_PER_STEP = 2
SEQ_TILE = 1024
POOL_SUB = 256
ROW_TILE = 1024
ROUTER_TILE = 1024
EXPERT_TILE = 256
COMBINE_TILE = 512
SLOT_TILE = 2048
WEIGHT_SLOTS = 3
ROW_SLOTS = 3

SC_CORES = 2
SC_SUBCORES = 16
SC_LANES = 16
SC_WORKERS = SC_CORES * SC_SUBCORES
SC_GATHER_ROWS = 64
SC_SCAN_CHUNK = 2048
SC_SCAN_UNROLL = 8

HI_MASK = 0xFFFF0000


def _cparams(sem):
    return pltpu.CompilerParams(dimension_semantics=sem, vmem_limit_bytes=VMEM_LIMIT)


def _sigmoid(x):
    return 1.0 / (1.0 + jnp.exp(-x))


def _layer_norm(y, g, b):
    mu = jnp.mean(y, axis=-1, keepdims=True)
    yc = y - mu
    var = jnp.mean(yc * yc, axis=-1, keepdims=True)
    return yc * lax.rsqrt(var + LN_EPS) * g + b


def _pack_bf16_pairs(y):
    c = y.shape[1] // 2
    bits = lax.bitcast_convert_type(y.astype(BF16).astype(F32), U32)
    return (bits[:, :c] >> 16) | (bits[:, c:] & jnp.uint32(HI_MASK))


def _unpack_bf16_pairs(p):
    lo = lax.bitcast_convert_type(p << 16, F32)
    hi = lax.bitcast_convert_type(p & jnp.uint32(HI_MASK), F32)
    return jnp.concatenate([lo, hi], axis=1)


def _inproj_kernel(x_ref, w_ref, wg_ref, bg_ref, o_ref, g_ref, xb_ref):
    @pl.when(pl.program_id(1) == 0)
    def _():
        xb_ref[...] = x_ref[...].astype(BF16)
        nt_dims = (((1,), (1,)), ((), ()))
        g_ref[0] = lax.dot_general(wg_ref[...], xb_ref[...], nt_dims, preferred_element_type=F32) + bg_ref[...]

    o_ref[...] = jnp.dot(xb_ref[...], w_ref[...], preferred_element_type=F32).astype(o_ref.dtype)


def _inproj(x, w, w_gate_t, bias_gate, seq):
    t, d = x.shape
    n = w.shape[1]
    r = w_gate_t.shape[0]
    tm, tn = INPROJ_TILE_M, INPROJ_TILE_N
    per_seq = seq // tm
    return pl.pallas_call(
        _inproj_kernel,
        grid=(t // tm, n // tn),
        in_specs=[pl.BlockSpec((tm, d), lambda i, j: (i, 0)),
                  pl.BlockSpec((d, tn), lambda i, j: (0, j)),
                  pl.BlockSpec((r, d), lambda i, j: (0, 0)),
                  pl.BlockSpec((r, 1), lambda i, j: (0, 0))],
        out_specs=[pl.BlockSpec((tm, tn), lambda i, j: (i, j)),
                   pl.BlockSpec((1, r, tm), lambda i, j: (i // per_seq, 0, i % per_seq))],
        out_shape=[jax.ShapeDtypeStruct((t, n), BF16), jax.ShapeDtypeStruct((t // seq, r, seq), F32)],
        scratch_shapes=[pltpu.VMEM((tm, d), BF16)],
        compiler_params=_cparams(("parallel", "arbitrary")),
        name="inproj",
    )(x, w, w_gate_t, bias_gate)


def _qkconv_kernel(prev_ref, main_ref, next_ref, w_ref, o_ref, *, ts, scale, transpose):
    t = pl.program_id(1)
    nt = pl.num_programs(1)
    main = main_ref[0].astype(F32)
    prev = prev_ref[0].astype(F32)[BF16_ROWS - SUBLANES:]
    nxt = next_ref[0].astype(F32)[:SUBLANES]
    prev = jnp.where(t > 0, prev, 0.0)
    nxt = jnp.where(t < nt - 1, nxt, 0.0)
    ext = jnp.concatenate([prev, main, nxt], axis=0)
    w = w_ref[...]
    pad = CONV_WIDTH // 2
    acc = jnp.zeros_like(main)
    for j in range(CONV_WIDTH):
        off = SUBLANES - pad + j
        acc = acc + ext[off:off + ts] * w[j:j + 1]
    y = acc * _sigmoid(acc) * scale
    if transpose:
        o_ref[0] = y.T.astype(o_ref.dtype)
    else:
        o_ref[0] = y.astype(o_ref.dtype)


def _qkconv(proj3, conv_w, col0, scale, transpose):
    bsz, seq, _ = proj3.shape
    ts = SEQ_TILE
    nt = seq // ts
    cw = 2 * LANES
    ncol = conv_w.shape[1] // cw
    hb = ts // BF16_ROWS
    n_hb = seq // BF16_ROWS
    kern = functools.partial(_qkconv_kernel, ts=ts, scale=scale, transpose=transpose)
    if transpose:
        out_shape = jax.ShapeDtypeStruct((bsz, ncol * cw, seq), BF16)
        out_spec = pl.BlockSpec((1, cw, ts), lambda b, t, j: (b, j, t))
    else:
        out_shape = jax.ShapeDtypeStruct((bsz, seq, ncol * cw), BF16)
        out_spec = pl.BlockSpec((1, ts, cw), lambda b, t, j: (b, t, j))
    return pl.pallas_call(
        kern,
        grid=(bsz, nt, ncol),
        in_specs=[pl.BlockSpec((1, BF16_ROWS, cw), lambda b, t, j: (b, jnp.maximum(t * hb - 1, 0), col0 + j)),
                  pl.BlockSpec((1, ts, cw), lambda b, t, j: (b, t, col0 + j)),
                  pl.BlockSpec((1, BF16_ROWS, cw),
                               lambda b, t, j: (b, jnp.minimum((t + 1) * hb, n_hb - 1), col0 + j)),
                  pl.BlockSpec((CONV_WIDTH, cw), lambda b, t, j: (0, j))],
        out_specs=out_spec,
        out_shape=out_shape,
        compiler_params=_cparams(("parallel", "parallel", "parallel")),
        name="qkconv_t" if transpose else "qkconv",
    )(proj3, proj3, proj3, conv_w)


def _pool_kernel(prev_ref, main_ref, next_ref, band_ref, pw_ref, ps_ref, o_ref, *, ts, seq, cw):
    t = pl.program_id(1)
    nt = pl.num_programs(1)
    prev = jnp.where(t > 0, prev_ref[0], jnp.zeros_like(prev_ref[0]))
    nxt = jnp.where(t < nt - 1, next_ref[0], jnp.zeros_like(next_ref[0]))
    ext = jnp.concatenate([prev, main_ref[0], nxt], axis=0)
    sub = POOL_SUB
    for g in range(POOL_GROUPS):
        hw = 1 << g
        cols = slice(g * cw, (g + 1) * cw)
        for j in range(ts // sub):
            rows = ext[j * sub:j * sub + sub + 2 * LANES, cols]
            s = jnp.dot(band_ref[g], rows, preferred_element_type=F32)
            tabs = t * ts + j * sub + lax.broadcasted_iota(I32, (sub, 1), 0)
            cnt = jnp.minimum(tabs + hw, seq) - jnp.maximum(tabs - hw, 0)
            centre = rows[LANES:LANES + sub].astype(F32)
            pooled = s / cnt.astype(F32) - centre
            mixed = jnp.dot(pooled.astype(BF16), pw_ref[g], preferred_element_type=F32) * ps_ref[:, cols]
            o_ref[0, j * sub:(j + 1) * sub, cols] = mixed.astype(o_ref.dtype)


def _pool(proj3, pool_w, pool_scale):
    bsz, seq, _ = proj3.shape
    ts = SEQ_TILE
    nt = seq // ts
    cw = pool_w.shape[-1]
    width = POOL_GROUPS * cw
    hb = ts // LANES
    n_hb = seq // LANES
    i = np.arange(POOL_SUB)[:, None]
    c = np.arange(POOL_SUB + 2 * LANES)[None, :] - LANES
    band = np.stack([(c >= i - (1 << g)) & (c < i + (1 << g)) for g in range(POOL_GROUPS)])
    band = jnp.asarray(band.astype(np.float32), BF16)
    kern = functools.partial(_pool_kernel, ts=ts, seq=seq, cw=cw)
    return pl.pallas_call(
        kern,
        grid=(bsz, nt),
        in_specs=[pl.BlockSpec((1, LANES, width), lambda b, t: (b, jnp.maximum(t * hb - 1, 0), 0)),
                  pl.BlockSpec((1, ts, width), lambda b, t: (b, t, 0)),
                  pl.BlockSpec((1, LANES, width), lambda b, t: (b, jnp.minimum((t + 1) * hb, n_hb - 1), 0)),
                  pl.BlockSpec(band.shape, lambda b, t: (0, 0, 0)),
                  pl.BlockSpec(pool_w.shape, lambda b, t: (0, 0, 0)),
                  pl.BlockSpec((1, width), lambda b, t: (0, 0))],
        out_specs=pl.BlockSpec((1, ts, width), lambda b, t: (b, t, 0)),
        out_shape=jax.ShapeDtypeStruct((bsz, seq, width), BF16),
        compiler_params=_cparams(("parallel", "parallel")),
        name="pool",
    )(proj3, proj3, proj3, band, pool_w, pool_scale)


def _dot_split(val_r, mask):
    hi = val_r.astype(BF16)
    lo = (val_r - hi.astype(F32)).astype(BF16)
    return (jnp.dot(hi, mask, preferred_element_type=F32) + jnp.dot(lo, mask, preferred_element_type=F32))


def _dot_split_t(mask, val_r):
    hi = val_r.astype(BF16)
    lo = (val_r - hi.astype(F32)).astype(BF16)
    nt_dims = (((1,), (1,)), ((), ()))
    return (lax.dot_general(mask, hi, nt_dims, preferred_element_type=F32)
            + lax.dot_general(mask, lo, nt_dims, preferred_element_type=F32))


def _mlstm_kernel(q_ref, kt_ref, v_ref, uo_ref, gr_ref, nw_ref, o_ref,
                  cf_ref, nf_ref, mf_ref, cb_ref, nb_ref, mb_ref, cbs_ref, nbs_ref, mbs_ref,
                  *, chunk, n_chunks, heads):
    L = chunk
    nb = q_ref.shape[0]
    dqk = kt_ref.shape[1] // heads
    dv = v_ref.shape[2] // heads
    p = pl.program_id(1)
    c = pl.program_id(2)
    row = lax.broadcasted_iota(I32, (L, L), 0)
    col = lax.broadcasted_iota(I32, (L, L), 1)
    tri_le = row <= col
    tri_ge = row >= col
    m_le = jnp.where(tri_le, 1.0, 0.0).astype(BF16)
    m_ge = jnp.where(tri_ge, 1.0, 0.0).astype(BF16)
    lane_r = lax.broadcasted_iota(I32, (1, L), 1)
    neg_inf = jnp.float32(-jnp.inf)

    chains = [(bb, hd) for bb in range(nb) for hd in range(heads)]
    hs = range(len(chains))
    kts = [kt_ref[bb, hd * dqk:(hd + 1) * dqk, :] for bb, hd in chains]
    vs = [v_ref[bb, :, hd * dv:(hd + 1) * dv] for bb, hd in chains]
    gates = [gr_ref[bb, hd * SUBLANES:(hd + 1) * SUBLANES, :] for bb, hd in chains]
    lf_r = [jax.nn.log_sigmoid(g) for g in gates]

    def update_state(c_ref, n_ref, m_ref, g_r, tot):
        m_prev = [m_ref[hh] for hh in hs]
        m_new = [jnp.maximum(tot[hh] + m_prev[hh], jnp.max(g_r[hh], axis=1, keepdims=True)) for hh in hs]
        decay = [jnp.exp(tot[hh] + m_prev[hh] - m_new[hh]) for hh in hs]
        kw = [kts[hh].astype(F32) * jnp.exp(g_r[hh] - m_new[hh]) for hh in hs]
        upd = [jnp.dot(kw[hh].astype(BF16), vs[hh], preferred_element_type=F32) for hh in hs]
        for hh in hs:
            c_ref[hh] = decay[hh] * c_ref[hh] + upd[hh]
            n_ref[hh] = decay[hh] * n_ref[hh] + jnp.sum(kw[hh], axis=1, keepdims=True)
            m_ref[hh] = m_new[hh]

    @pl.when(p == 0)
    def _backward_states():
        @pl.when(c == 0)
        def _():
            cb_ref[...] = jnp.zeros_like(cb_ref)
            nb_ref[...] = jnp.zeros_like(nb_ref)
            mb_ref[...] = jnp.zeros_like(mb_ref)

        cc = n_chunks - 1 - c
        cbs_ref[cc] = cb_ref[...].astype(BF16)
        nbs_ref[cc] = nb_ref[...]
        mbs_ref[cc] = mb_ref[...]
        a_r = [_dot_split(lf_r[hh], m_ge)[3:4] for hh in hs]
        a0 = [jnp.sum(jnp.where(lane_r == 0, a_r[hh], 0.0), axis=1, keepdims=True) for hh in hs]
        g_r = [a0[hh] - a_r[hh] + gates[hh][2:3] for hh in hs]
        update_state(cb_ref, nb_ref, mb_ref, g_r, a0)

    @pl.when(p == 1)
    def _outputs():
        @pl.when(c == 0)
        def _():
            cf_ref[...] = jnp.zeros_like(cf_ref)
            nf_ref[...] = jnp.zeros_like(nf_ref)
            mf_ref[...] = jnp.zeros_like(mf_ref)

        nb_in = nbs_ref[c]
        mb_in = mbs_ref[c]
        cb_in = cbs_ref[c]
        qs = [q_ref[bb, :, hd * dqk:(hd + 1) * dqk] for bb, hd in chains]
        b_r = [_dot_split(lf_r[hh], m_le)[1:2] for hh in hs]
        a_r = [_dot_split(lf_r[hh], m_ge)[3:4] for hh in hs]
        b_c = [_dot_split_t(m_ge, lf_r[hh])[:, 1:2] for hh in hs]
        a_c = [_dot_split_t(m_le, lf_r[hh])[:, 3:4] for hh in hs]
        li_f = [gates[hh][0:1] for hh in hs]
        li_b = [gates[hh][2:3] for hh in hs]

        s = [jnp.dot(qs[hh], kts[hh], preferred_element_type=F32) for hh in hs]
        nlane = lax.broadcasted_iota(I32, (dqk, LANES), 1)
        nmat = [jnp.where(nlane == 0, nf_ref[hh], jnp.where(nlane == 1, nb_in[hh], 0.0)).astype(BF16)
                for hh in hs]
        qn = [jnp.dot(qs[hh], nmat[hh], preferred_element_type=F32) for hh in hs]

        def direction(d, mask, cum_c, m_prev, qn_col):
            d = [jnp.where(mask, d[hh], neg_inf) for hh in hs]
            m_inter = [cum_c[hh] + m_prev[hh] for hh in hs]
            m_t = [jnp.maximum(m_inter[hh], jnp.max(d[hh], axis=1, keepdims=True)) for hh in hs]
            pmat = [jnp.exp(d[hh] - m_t[hh]) * s[hh] for hh in hs]
            w_inter = [jnp.exp(m_inter[hh] - m_t[hh]) for hh in hs]
            den = [jnp.sum(pmat[hh], axis=1, keepdims=True) + w_inter[hh] * qn_col[hh] for hh in hs]
            r = [1.0 / jnp.maximum(jnp.abs(den[hh]), jnp.exp(-m_t[hh])) for hh in hs]
            return [pmat[hh] * r[hh] for hh in hs], [w_inter[hh] * r[hh] for hh in hs]

        pf, sf = direction([b_c[hh] - (b_r[hh] - li_f[hh]) for hh in hs], tri_ge, b_c,
                           [mf_ref[hh] for hh in hs], [qn[hh][:, 0:1] for hh in hs])
        pb, sb = direction([a_c[hh] - (a_r[hh] - li_b[hh]) for hh in hs], tri_le, a_c,
                           [mb_in[hh] for hh in hs], [qn[hh][:, 1:2] for hh in hs])
        qf = [qs[hh].astype(F32) for hh in hs]
        h = [jnp.dot((pf[hh] + pb[hh]).astype(BF16), vs[hh], preferred_element_type=F32)
             + jnp.dot((qf[hh] * sf[hh]).astype(BF16), cf_ref[hh].astype(BF16), preferred_element_type=F32)
             + jnp.dot((qf[hh] * sb[hh]).astype(BF16), cb_in[hh], preferred_element_type=F32) for hh in hs]

        mu = [jnp.mean(h[hh], axis=1, keepdims=True) for hh in hs]
        hc = [h[hh] - mu[hh] for hh in hs]
        var = [jnp.mean(hc[hh] * hc[hh], axis=1, keepdims=True) for hh in hs]
        hn = [hc[hh] * lax.rsqrt(var[hh] + LN_EPS) * nw_ref[chains[hh][1]] for hh in hs]
        for hh, (bb, hd) in enumerate(chains):
            gate_o = _sigmoid(uo_ref[bb, :, hd * dv:(hd + 1) * dv].astype(F32))
            o_ref[bb, :, hd * dv:(hd + 1) * dv] = (gate_o * hn[hh]).astype(o_ref.dtype)

        b_last = [jnp.sum(jnp.where(lane_r == L - 1, b_r[hh], 0.0), axis=1, keepdims=True) for hh in hs]
        update_state(cf_ref, nf_ref, mf_ref, [b_last[hh] - b_r[hh] + li_f[hh] for hh in hs], b_last)


def _mlstm(q, kt, proj3, gates, norm_w, v_off, o_off):
    bsz, seq, qw = q.shape
    dqk = qw // N_HEADS
    dv = norm_w.shape[-1]
    L = MLSTM_CHUNK
    nc = seq // L
    hg = MLSTM_HEADS_PER_STEP
    bg = MLSTM_BATCH_PER_STEP
    groups = N_HEADS // hg
    assert N_HEADS % hg == 0 and bsz % bg == 0 and v_off % (hg * dv) == 0 and o_off % (hg * dv) == 0
    v_blk0 = v_off // (hg * dv)
    o_blk0 = o_off // (hg * dv)
    kern = functools.partial(_mlstm_kernel, chunk=L, n_chunks=nc, heads=hg)

    def chunk_of(p, c):
        return jnp.where(p == 0, nc - 1 - c, c)

    def out_chunk(p, c):
        return jnp.where(p == 0, 0, c)

    return pl.pallas_call(
        kern,
        grid=((bsz // bg) * groups, 2, nc),
        in_specs=[
            pl.BlockSpec((bg, L, hg * dqk), lambda g, p, c: (g // groups, out_chunk(p, c), g % groups)),
            pl.BlockSpec((bg, hg * dqk, L), lambda g, p, c: (g // groups, g % groups, chunk_of(p, c))),
            pl.BlockSpec((bg, L, hg * dv), lambda g, p, c: (g // groups, chunk_of(p, c), v_blk0 + g % groups)),
            pl.BlockSpec((bg, L, hg * dv), lambda g, p, c: (g // groups, out_chunk(p, c), o_blk0 + g % groups)),
            pl.BlockSpec((bg, hg * SUBLANES, L), lambda g, p, c: (g // groups, g % groups, chunk_of(p, c))),
            pl.BlockSpec((hg, 1, dv), lambda g, p, c: (g % groups, 0, 0)),
        ],
        out_specs=pl.BlockSpec((bg, L, hg * dv), lambda g, p, c: (g // groups, out_chunk(p, c), g % groups)),
        out_shape=jax.ShapeDtypeStruct((bsz, seq, N_HEADS * dv), BF16),
        scratch_shapes=[
            pltpu.VMEM((bg * hg, dqk, dv), F32), pltpu.VMEM((bg * hg, dqk, 1), F32),
            pltpu.VMEM((bg * hg, 1, 1), F32),
            pltpu.VMEM((bg * hg, dqk, dv), F32), pltpu.VMEM((bg * hg, dqk, 1), F32),
            pltpu.VMEM((bg * hg, 1, 1), F32),
            pltpu.VMEM((nc, bg * hg, dqk, dv), BF16), pltpu.VMEM((nc, bg * hg, dqk, 1), F32),
            pltpu.VMEM((nc, bg * hg, 1, 1), F32),
        ],
        compiler_params=_cparams(("parallel", "arbitrary", "arbitrary")),
        name="mlstm",
    )(q, kt, proj3, proj3, gates, norm_w)


def _mixout_kernel(hg_ref, a_ref, uga_ref, ugb_ref, x_ref, wbd_ref, wo_ref, bga_ref, bgb_ref,
                   g_ref, b_ref, o_ref, op_ref, *, alpha):
    branch_b = jnp.dot(hg_ref[...], wbd_ref[...], preferred_element_type=F32)
    ga = _sigmoid(uga_ref[...].astype(F32) + bga_ref[...])
    gb = _sigmoid(ugb_ref[...].astype(F32) + bgb_ref[...])
    merged = ga * a_ref[...].astype(F32) + gb * branch_b
    mix = jnp.dot(merged.astype(BF16), wo_ref[...], preferred_element_type=F32)
    y = _layer_norm(alpha * x_ref[...] + mix, g_ref[...], b_ref[...])
    o_ref[...] = y
    op_ref[...] = _pack_bf16_pairs(y)


def _mixout(hg, branch_a, proj, x, w_b_down, w_out, b_gate, ln_g, ln_b, ga_col, alpha):
    t, d = x.shape
    tm = ROW_TILE
    inner = hg.shape[1]
    row = lambda i: (i, 0)
    const = lambda i: (0, 0)
    return pl.pallas_call(
        functools.partial(_mixout_kernel, alpha=alpha),
        grid=(t // tm,),
        in_specs=[pl.BlockSpec((tm, inner), row),
                  pl.BlockSpec((tm, d), row),
                  pl.BlockSpec((tm, d), lambda i: (i, ga_col)),
                  pl.BlockSpec((tm, d), lambda i: (i, ga_col + 1)),
                  pl.BlockSpec((tm, d), row),
                  pl.BlockSpec((inner, d), const),
                  pl.BlockSpec((d, d), const),
                  pl.BlockSpec((1, d), const),
                  pl.BlockSpec((1, d), lambda i: (0, 1)),
                  pl.BlockSpec((1, d), const),
                  pl.BlockSpec((1, d), const)],
        out_specs=[pl.BlockSpec((tm, d), row), pl.BlockSpec((tm, d // 2), row)],
        out_shape=[jax.ShapeDtypeStruct((t, d), F32), jax.ShapeDtypeStruct((t, d // 2), U32)],
        compiler_params=_cparams(("parallel",)),
        name="mixout",
    )(hg, branch_a, proj, proj, x, w_b_down, w_out, b_gate, b_gate, ln_g, ln_b)


def _router_kernel(x_ref, wh_ref, wl_ref, br_ref, eidx_ref, wts_ref, rank_ref, cnt_ref, run_ref,
                   *, n_experts):
    i = pl.program_id(0)
    tm = x_ref.shape[0]
    ne = n_experts
    per_group = ne // N_ROUTE_GROUPS

    @pl.when(i == 0)
    def _():
        run_ref[...] = jnp.zeros_like(run_ref)

    x = x_ref[...]
    xh = x.astype(BF16)
    xl = (x - xh.astype(F32)).astype(BF16)
    nt_dims = (((1,), (1,)), ((), ()))
    logits = (lax.dot_general(wh_ref[...], xh, nt_dims, preferred_element_type=F32)
              + lax.dot_general(wh_ref[...], xl, nt_dims, preferred_element_type=F32)
              + lax.dot_general(wl_ref[...], xh, nt_dims, preferred_element_type=F32))
    scores = _sigmoid(logits)
    sel = scores + br_ref[...]
    neg = jnp.float32(-jnp.inf)
    big = jnp.float32(1e9)
    rowi = lax.broadcasted_iota(I32, (ne, tm), 0).astype(F32)

    gi = lax.broadcasted_iota(I32, (N_ROUTE_GROUPS, tm), 0).astype(F32)
    work = jnp.zeros((N_ROUTE_GROUPS, tm), F32)
    for g in range(N_ROUTE_GROUPS):
        blk = sel[g * per_group:(g + 1) * per_group]
        ri = lax.broadcasted_iota(I32, (per_group, tm), 0).astype(F32) + float(g * per_group)
        m1 = jnp.max(blk, axis=0, keepdims=True)
        i1 = jnp.min(jnp.where(blk == m1, ri, big), axis=0, keepdims=True)
        m2 = jnp.max(jnp.where(ri == i1, neg, blk), axis=0, keepdims=True)
        work = jnp.where(gi == float(g), m1 + m2, work)
    row_group = jnp.floor(rowi * (1.0 / per_group))
    allowed = jnp.zeros((ne, tm), F32)
    for _ in range(TOPK_GROUPS):
        m = jnp.max(work, axis=0, keepdims=True)
        idx = jnp.min(jnp.where(work == m, gi, big), axis=0, keepdims=True)
        work = jnp.where(gi == idx, neg, work)
        allowed = jnp.where(row_group == idx, 1.0, allowed)
    selm = jnp.where(allowed > 0.5, sel, neg)

    member = jnp.zeros((ne, tm), F32)
    idxs, wks = [], []
    for _ in range(TOP_K):
        m = jnp.max(selm, axis=0, keepdims=True)
        idx = jnp.min(jnp.where(selm == m, rowi, big), axis=0, keepdims=True)
        hit = rowi == idx
        wks.append(jnp.sum(jnp.where(hit, scores, 0.0), axis=0, keepdims=True))
        idxs.append(idx)
        member = jnp.where(hit, 1.0, member)
        selm = jnp.where(hit, neg, selm)
    wsum = wks[0]
    for wk in wks[1:]:
        wsum = wsum + wk

    ti = lax.broadcasted_iota(I32, (tm, tm), 0)
    tj = lax.broadcasted_iota(I32, (tm, tm), 1)
    strict = jnp.where(ti < tj, 1.0, 0.0).astype(BF16)
    prefix = jnp.dot(member.astype(BF16), strict, preferred_element_type=F32) + run_ref[...]
    ranks = [jnp.sum(jnp.where(rowi == idx, prefix, 0.0), axis=0, keepdims=True) for idx in idxs]
    run_new = run_ref[...] + jnp.sum(member, axis=1, keepdims=True)
    run_ref[...] = run_new

    eidx_ref[...] = jnp.concatenate(idxs, axis=0).astype(I32)
    wts_ref[...] = jnp.concatenate([wk / wsum * ROUTED_SCALE for wk in wks], axis=0)
    rank_ref[...] = jnp.concatenate(ranks, axis=0).astype(I32)
    cnt_ref[...] = jnp.broadcast_to(run_new, cnt_ref.shape).astype(I32)


def _router(x1, wr_hi, wr_lo, b_router):
    t, d = x1.shape
    ne = wr_hi.shape[0]
    tm = ROUTER_TILE
    kern = functools.partial(_router_kernel, n_experts=ne)
    tok = lambda i: (0, i)
    const = lambda i: (0, 0)
    return pl.pallas_call(
        kern,
        grid=(t // tm,),
        in_specs=[pl.BlockSpec((tm, d), lambda i: (i, 0)),
                  pl.BlockSpec((ne, d), const),
                  pl.BlockSpec((ne, d), const),
                  pl.BlockSpec((ne, 1), const)],
        out_specs=[pl.BlockSpec((TOP_K, tm), tok), pl.BlockSpec((TOP_K, tm), tok),
                   pl.BlockSpec((TOP_K, tm), tok), pl.BlockSpec((ne, LANES), const)],
        out_shape=[jax.ShapeDtypeStruct((TOP_K, t), I32), jax.ShapeDtypeStruct((TOP_K, t), F32),
                   jax.ShapeDtypeStruct((TOP_K, t), I32), jax.ShapeDtypeStruct((ne, LANES), I32)],
        scratch_shapes=[pltpu.VMEM((ne, 1), F32)],
        compiler_params=_cparams(("arbitrary",)),
        name="router",
    )(x1, wr_hi, wr_lo, b_router)


def _slots_kernel(eidx_ref, rank_ref, ps_ref, pos_ref, *, n_experts):
    tm = eidx_ref.shape[1]
    rowi = lax.broadcasted_iota(I32, (n_experts, tm), 0)
    eidx = eidx_ref[...]
    starts = ps_ref[...]
    base = [jnp.sum(jnp.where(rowi == eidx[k:k + 1], starts, 0.0), axis=0, keepdims=True)
            for k in range(TOP_K)]
    pos_ref[...] = jnp.concatenate(base, axis=0).astype(I32) + rank_ref[...]


def _slots(eidx, rank, pstarts):
    k, t = eidx.shape
    ne = pstarts.shape[0]
    tm = SLOT_TILE
    tok = lambda i: (0, i)
    return pl.pallas_call(
        functools.partial(_slots_kernel, n_experts=ne),
        grid=(t // tm,),
        in_specs=[pl.BlockSpec((k, tm), tok), pl.BlockSpec((k, tm), tok),
                  pl.BlockSpec((ne, 1), lambda i: (0, 0))],
        out_specs=pl.BlockSpec((k, tm), tok),
        out_shape=jax.ShapeDtypeStruct((k, t), I32),
        compiler_params=_cparams(("parallel",)),
        name="slots",
    )(eidx, rank, pstarts.astype(F32).reshape(ne, 1))


def _sc_worker_id():
    return lax.axis_index("subcore") * SC_CORES + lax.axis_index("core")


def _sc_mesh():
    return plsc.VectorSubcoreMesh(core_axis_name="core", subcore_axis_name="subcore")


def _sc_vector_params():
    cp = pltpu.CompilerParams()
    if "needs_layout_passes" in pltpu.CompilerParams.__dataclass_fields__:
        cp = dataclasses.replace(cp, needs_layout_passes=False)
    return cp


def _sc_gather(table, idx):
    n = idx.shape[0]
    w = table.shape[1]
    ch = SC_GATHER_ROWS
    n_ch = n // (SC_WORKERS * ch)
    assert n % (SC_WORKERS * ch * 2) == 0

    @functools.partial(
        pl.kernel, mesh=_sc_mesh(),
        out_type=jax.ShapeDtypeStruct((n, w), table.dtype),
        scratch_types=[pltpu.VMEM((n_ch, ch), I32), pltpu.VMEM((2, ch, w), table.dtype),
                       pltpu.SemaphoreType.DMA((2,))],
    )
    def kern(table_hbm, idx_hbm, out_hbm, idx_v, rows_v, sem):
        first = _sc_worker_id() * n_ch
        pltpu.sync_copy(idx_hbm.at[pl.ds(first, n_ch)], idx_v)

        def gather(j, b):
            return pltpu.make_async_copy(table_hbm.at[idx_v.at[j]], rows_v.at[b], sem.at[b])

        gather(0, 0).start()

        @pl.loop(0, n_ch, step=2)
        def _(j):
            for b in range(2):
                jj = j + b

                @pl.when(jj + 1 < n_ch)
                def _():
                    gather(jj + 1, 1 - b).start()

                gather(jj, b).wait()
                row0 = pl.multiple_of((first + jj) * ch, ch)
                pltpu.sync_copy(rows_v.at[b], out_hbm.at[pl.ds(row0, ch)])

    return kern(table, idx.reshape(n // ch, ch))


def _sc_invert(pos_flat, n_slots, n_tokens):
    n = pos_flat.shape[0]
    per_w = n_slots // SC_WORKERS
    chunk = SC_SCAN_CHUNK
    assert n_slots % (SC_WORKERS * SC_LANES) == 0 and n % chunk == 0 and n_tokens % chunk == 0

    @functools.partial(
        pl.kernel, mesh=_sc_mesh(),
        out_type=jax.ShapeDtypeStruct((n_slots,), I32),
        scratch_types=[pltpu.VMEM((per_w,), I32), pltpu.VMEM((chunk,), I32)],
        compiler_params=_sc_vector_params(),
    )
    def kern(pos_hbm, out_hbm, table_v, pos_v):
        lo = _sc_worker_id() * per_w
        lane = lax.iota(I32, SC_LANES)

        @pl.loop(0, per_w // SC_LANES)
        def _(i):
            off = pl.multiple_of(i * SC_LANES, SC_LANES)
            table_v[pl.ds(off, SC_LANES)] = lax.rem(lo + off + lane, jnp.full((SC_LANES,), n_tokens, I32))

        @pl.loop(0, n // chunk)
        def _(c):
            pltpu.sync_copy(pos_hbm.at[pl.ds(pl.multiple_of(c * chunk, chunk), chunk)], pos_v)
            tok0 = lax.rem(c, n_tokens // chunk) * chunk

            @plsc.parallel_loop(0, chunk // SC_LANES, 1, unroll=SC_SCAN_UNROLL)
            def _(j):
                off = pl.multiple_of(j * SC_LANES, SC_LANES)
                local = pos_v[pl.ds(off, SC_LANES)] - lo
                mine = (local >= 0) & (local < per_w)
                plsc.store_scatter(table_v, [jnp.where(mine, local, 0)], tok0 + off + lane, mask=mine)

        pltpu.sync_copy(table_v, out_hbm.at[pl.ds(pl.multiple_of(lo, SC_LANES), per_w)])

    return kern(pos_flat)


def _experts_kernel(te_ref, nu_ref, ord_ref, nxt_ref, nxt2_ref, xs_hbm, wg_hbm, wu_hbm, wd_hbm, ys_ref,
                    xbuf_ref, wgf_ref, wuf_ref, wdf_ref, wgb_ref, wub_ref, wdb_ref, sem_ref, xsem_ref):
    i = pl.program_id(0)
    n_used = nu_ref[0]
    tile = xbuf_ref.shape[1]

    def row_copy(step):
        slot = lax.rem(step, ROW_SLOTS)
        src = xs_hbm.at[pl.ds(pl.multiple_of(step * tile, tile), tile)]
        return pltpu.make_async_copy(src, xbuf_ref.at[slot], xsem_ref.at[slot])

    def weight_copies(e, slot):
        return (pltpu.make_async_copy(wg_hbm.at[e], wgf_ref.at[slot], sem_ref.at[slot]),
                pltpu.make_async_copy(wu_hbm.at[e], wuf_ref.at[slot], sem_ref.at[slot]),
                pltpu.make_async_copy(wd_hbm.at[e], wdf_ref.at[slot], sem_ref.at[slot]))

    @pl.when(i < n_used)
    def _():
        @pl.when(i == 0)
        def _():
            for ahead in range(ROW_SLOTS - 1):
                @pl.when(ahead < n_used)
                def _():
                    row_copy(ahead).start()

        @pl.when(i + (ROW_SLOTS - 1) < n_used)
        def _():
            row_copy(i + (ROW_SLOTS - 1)).start()

        e = te_ref[i]
        ordinal = ord_ref[i]
        slot = lax.rem(ordinal, WEIGHT_SLOTS)
        first_tile_of_expert = jnp.logical_or(i == 0, e != te_ref[jnp.maximum(i - 1, 0)])

        @pl.when(i == 0)
        def _():
            for cp in weight_copies(e, slot):
                cp.start()

            @pl.when(nxt_ref[i] >= 0)
            def _():
                for cp in weight_copies(nxt_ref[i], lax.rem(ordinal + 1, WEIGHT_SLOTS)):
                    cp.start()

        @pl.when(first_tile_of_expert)
        def _():
            ahead = nxt2_ref[i]

            @pl.when(ahead >= 0)
            def _():
                for cp in weight_copies(ahead, lax.rem(ordinal + 2, WEIGHT_SLOTS)):
                    cp.start()

            for cp in weight_copies(e, slot):
                cp.wait()
            wgb_ref[...] = wgf_ref[slot].astype(BF16)
            wub_ref[...] = wuf_ref[slot].astype(BF16)
            wdb_ref[...] = wdf_ref[slot].astype(BF16)

        row_copy(i).wait()
        x = _unpack_bf16_pairs(xbuf_ref[lax.rem(i, ROW_SLOTS)]).astype(BF16)
        gate = jnp.dot(x, wgb_ref[...], preferred_element_type=F32)
        up = jnp.dot(x, wub_ref[...], preferred_element_type=F32)
        hid = (gate * _sigmoid(gate) * up).astype(BF16)
        y = jnp.dot(hid, wdb_ref[...], preferred_element_type=F32)
        ys_ref[...] = _pack_bf16_pairs(y)

    @pl.when(i >= n_used)
    def _():
        ys_ref[...] = jnp.zeros_like(ys_ref)


def _experts(tile_expert, n_used, expert_ord, expert_next, expert_next2, xs, w_gate_e, w_up_e, w_down_e):
    n_slots, w = xs.shape
    tile = EXPERT_TILE
    n_tiles = n_slots // tile
    _, d, de = w_gate_e.shape
    assert WEIGHT_SLOTS == 3 and ROW_SLOTS >= 2
    grid_spec = pltpu.PrefetchScalarGridSpec(
        num_scalar_prefetch=5,
        grid=(n_tiles,),
        in_specs=[pl.BlockSpec(memory_space=pl.ANY),
                  pl.BlockSpec(memory_space=pl.ANY),
                  pl.BlockSpec(memory_space=pl.ANY),
                  pl.BlockSpec(memory_space=pl.ANY)],
        out_specs=pl.BlockSpec((tile, w), lambda i, *_: (i, 0)),
        scratch_shapes=[pltpu.VMEM((ROW_SLOTS, tile, w), U32),
                        pltpu.VMEM((WEIGHT_SLOTS, d, de), F32), pltpu.VMEM((WEIGHT_SLOTS, d, de), F32),
                        pltpu.VMEM((WEIGHT_SLOTS, de, d), F32),
                        pltpu.VMEM((d, de), BF16), pltpu.VMEM((d, de), BF16), pltpu.VMEM((de, d), BF16),
                        pltpu.SemaphoreType.DMA((WEIGHT_SLOTS,)), pltpu.SemaphoreType.DMA((ROW_SLOTS,))],
    )
    return pl.pallas_call(
        _experts_kernel,
        grid_spec=grid_spec,
        out_shape=jax.ShapeDtypeStruct((n_slots, w), U32),
        compiler_params=_cparams(("arbitrary",)),
        name="experts",
    )(tile_expert, n_used, expert_ord, expert_next, expert_next2, xs, w_gate_e, w_up_e, w_down_e)


def _combine_kernel(wts_ref, x_ref, xp_ref, yg_ref, wgs_ref, wus_ref, wds_ref, g_ref, b_ref, o_ref, *, alpha):
    wts = wts_ref[...]
    routed = jnp.zeros(x_ref.shape, F32)
    for k in range(TOP_K):
        routed = routed + wts[:, k:k + 1] * _unpack_bf16_pairs(yg_ref[k])
    xb = _unpack_bf16_pairs(xp_ref[...]).astype(BF16)
    gate = jnp.dot(xb, wgs_ref[...], preferred_element_type=F32)
    up = jnp.dot(xb, wus_ref[...], preferred_element_type=F32)
    hid = (gate * _sigmoid(gate) * up).astype(BF16)
    shared = jnp.dot(hid, wds_ref[...], preferred_element_type=F32)
    o_ref[...] = _layer_norm(alpha * x_ref[...] + (routed + shared), g_ref[...], b_ref[...])


def _combine(wts_c, x1, x1p, yg, w_gate_s, w_up_s, w_down_s, ln_g, ln_b, alpha):
    t, d = x1.shape
    w = x1p.shape[1]
    tb = COMBINE_TILE
    ds = w_gate_s.shape[1]
    row = lambda i: (i, 0)
    const = lambda i: (0, 0)
    return pl.pallas_call(
        functools.partial(_combine_kernel, alpha=alpha),
        grid=(t // tb,),
        in_specs=[pl.BlockSpec((tb, TOP_K), row),
                  pl.BlockSpec((tb, d), row),
                  pl.BlockSpec((tb, w), row),
                  pl.BlockSpec((TOP_K, tb, w), lambda i: (0, i, 0)),
                  pl.BlockSpec((d, ds), const),
                  pl.BlockSpec((d, ds), const),
                  pl.BlockSpec((ds, d), const),
                  pl.BlockSpec((1, d), const),
                  pl.BlockSpec((1, d), const)],
        out_specs=pl.BlockSpec((tb, d), row),
        out_shape=jax.ShapeDtypeStruct((t, d), F32),
        compiler_params=_cparams(("parallel",)),
        name="combine",
    )(wts_c, x1, x1p, yg, w_gate_s, w_up_s, w_down_s, ln_g, ln_b)


def _layer(alpha, x, w_in, b_if, b_gate, conv_qk, pool_w, pool_scale, mh_norm_w, w_b_down, w_out,
           ln1_g, ln1_b, w_router, b_router, w_gate_e, w_up_e, w_down_e,
           w_gate_s, w_up_s, w_down_s, ln2_g, ln2_b):
    bsz, seq, d = x.shape
    t = bsz * seq
    heads = N_HEADS
    pool_width = pool_w.shape[0] * pool_w.shape[1]
    qk_cols = conv_qk.shape[1]
    v_cols = mh_norm_w.shape[0] * mh_norm_w.shape[1]
    o_cols = v_cols
    if_cols = b_if.shape[0]
    dv = mh_norm_w.shape[1]
    dqk = qk_cols // (2 * heads)
    off_if = pool_width + qk_cols + v_cols + o_cols
    off_gate = off_if + if_cols

    xf = x.reshape(t, d)
    w_main = jnp.concatenate([w_in[:, :off_if], w_in[:, off_gate:]], axis=1).astype(BF16)
    n_gate = if_cols // heads
    w_if = w_in[:, off_if:off_gate].reshape(d, n_gate, heads).transpose(2, 1, 0)
    w_if = jnp.pad(w_if, ((0, 0), (0, SUBLANES - n_gate), (0, 0))).reshape(heads * SUBLANES, d).astype(BF16)
    bias_if = jnp.pad(b_if.reshape(n_gate, heads).T, ((0, 0), (0, SUBLANES - n_gate))).reshape(heads * SUBLANES, 1)
    n_main = w_main.shape[1]

    proj, gates = _inproj(xf, w_main, w_if, bias_if, seq)
    proj3 = proj.reshape(bsz, seq, n_main)

    branch_a = _pool(proj3, pool_w.astype(BF16), pool_scale.reshape(1, pool_width))

    cw = 2 * LANES
    q_col0 = pool_width // cw
    half = qk_cols // 2
    q = _qkconv(proj3, conv_qk[:, :half], q_col0, float(dqk) ** -0.5, False)
    kt = _qkconv(proj3, conv_qk[:, half:], q_col0 + half // cw, 1.0, True)

    hg = _mlstm(q, kt, proj3, gates, mh_norm_w.reshape(heads, 1, dv),
                pool_width + qk_cols, pool_width + qk_cols + v_cols)

    ga_col = (pool_width + qk_cols + v_cols + o_cols) // d
    x1, x1p = _mixout(hg.reshape(t, heads * dv), branch_a.reshape(t, pool_width), proj, xf,
                      w_b_down.astype(BF16), w_out.astype(BF16), b_gate.reshape(1, 2 * d),
                      ln1_g.reshape(1, d), ln1_b.reshape(1, d), ga_col, alpha)

    ne = w_router.shape[1]
    wr_t = w_router.T
    wr_hi = wr_t.astype(BF16)
    wr_lo = (wr_t - wr_hi.astype(F32)).astype(BF16)
    eidx, wts, rank, cnt = _router(x1, wr_hi, wr_lo, b_router.reshape(ne, 1))

    tile = EXPERT_TILE
    n_tiles = (t * TOP_K) // tile + ne
    counts = cnt[:, 0]
    pcounts = ((counts + tile - 1) // tile) * tile
    pends = jnp.cumsum(pcounts)
    pstarts = pends - pcounts
    pos = _slots(eidx, rank, pstarts)

    n_used = (pends[-1] // tile).astype(I32)
    tile_ids = jnp.minimum(jnp.arange(n_tiles, dtype=I32), n_used - 1)
    tile_expert = jnp.sum((pends[None, :] <= (tile_ids * tile)[:, None]).astype(I32), axis=1)
    tile_expert = jnp.minimum(tile_expert, ne - 1)
    new_expert = jnp.concatenate([jnp.ones((1,), I32), (tile_expert[1:] != tile_expert[:-1]).astype(I32)])
    expert_ord = jnp.cumsum(new_expert) - 1
    candidates = jnp.where(counts > 0, jnp.arange(ne, dtype=I32), ne)
    later_min = lax.cummin(candidates, axis=0, reverse=True)
    next_used = jnp.concatenate([later_min[1:], jnp.full((1,), ne, I32)])
    next_used = jnp.where(next_used >= ne, -1, next_used)
    next_used2 = jnp.where(next_used >= 0, next_used[jnp.maximum(next_used, 0)], -1)
    tile_onehot = tile_expert[:, None] == jnp.arange(ne, dtype=I32)[None, :]
    expert_next = jnp.sum(jnp.where(tile_onehot, next_used[None, :], 0), axis=1).astype(I32)
    expert_next2 = jnp.sum(jnp.where(tile_onehot, next_used2[None, :], 0), axis=1).astype(I32)

    n_slots = n_tiles * tile
    slot_tok = _sc_invert(pos.reshape(-1), n_slots, t)
    xs = _sc_gather(x1p, slot_tok)
    ys = _experts(tile_expert, n_used.reshape(1), expert_ord.astype(I32), expert_next, expert_next2, xs,
                  w_gate_e, w_up_e, w_down_e)
    yg = _sc_gather(ys, pos.reshape(-1)).reshape(TOP_K, t, x1p.shape[1])
    out = _combine(wts.T, x1, x1p, yg, w_gate_s.astype(BF16), w_up_s.astype(BF16),
                   w_down_s.astype(BF16), ln2_g.reshape(1, d), ln2_b.reshape(1, d), alpha)
    return out.reshape(bsz, seq, d)


def kernel(x, w_in, b_if, b_gate, conv_qk, pool_w, pool_scale, mh_norm_w, w_b_down, w_out, ln1_g, ln1_b,
           w_router, b_router, w_gate_e, w_up_e, w_down_e, w_gate_s, w_up_s, w_down_s, ln2_g, ln2_b):
    depth = w_in.shape[0]
    alpha = (2.0 * depth) ** 0.25
    for l in range(depth):
        x = _layer(alpha, x, w_in[l], b_if[l], b_gate[l], conv_qk[l], pool_w[l], pool_scale[l], mh_norm_w[l],
                   w_b_down[l], w_out[l], ln1_g[l], ln1_b[l], w_router[l], b_router[l], w_gate_e[l],
                   w_up_e[l], w_down_e[l], w_gate_s[l], w_up_s[l], w_down_s[l], ln2_g[l], ln2_b[l])
    return x
```

```python
import dataclasses
import functools

import jax
import jax.numpy as jnp
import numpy as np
from jax import lax
from jax.experimental import pallas as pl
from jax.experimental.pallas import tpu as pltpu
from jax.experimental.pallas import tpu_sc as plsc

F32 = jnp.float32
BF16 = jnp.bfloat16
I32 = jnp.int32
U32 = jnp.uint32

N_HEADS = 4
POOL_GROUPS = 4
CONV_WIDTH = 5
LN_EPS = 1e-5
N_ROUTE_GROUPS = 8
TOPK_GROUPS = 4
TOP_K = 8
ROUTED_SCALE = 2.5

LANES = 128
SUBLANES = 8
BF16_ROWS = 16
VMEM_LIMIT = 56 * 1024 * 1024

INPROJ_TILE_M = 2048
INPROJ_TILE_N = 1536
MLSTM_CHUNK = 256
MLSTM_HEADS_PER_STEP = 2
MLSTM_BATCH_PER_STEP = 2
SEQ_TILE = 1024
POOL_SUB = 256
ROW_TILE = 1024
ROUTER_TILE = 1024
EXPERT_TILE = 256
COMBINE_TILE = 512
SLOT_TILE = 2048
WEIGHT_SLOTS = 3
WEIGHT_DMA_PRIORITY = 0
ROW_SLOTS = 3

SC_CORES = 2
SC_SUBCORES = 16
SC_LANES = 16
SC_WORKERS = SC_CORES * SC_SUBCORES
SC_GATHER_ROWS = 64
SC_SCAN_CHUNK = 8192
SC_SCAN_UNROLL = 8

HI_MASK = 0xFFFF0000


def _cparams(sem):
    return pltpu.CompilerParams(dimension_semantics=sem, vmem_limit_bytes=VMEM_LIMIT)


def _sigmoid(x):
    return 1.0 / (1.0 + jnp.exp(-x))


def _layer_norm(y, g, b):
    mu = jnp.mean(y, axis=-1, keepdims=True)
    yc = y - mu
    var = jnp.mean(yc * yc, axis=-1, keepdims=True)
    return yc * lax.rsqrt(var + LN_EPS) * g + b


def _pack_bf16_pairs(y):
    c = y.shape[1] // 2
    bits = lax.bitcast_convert_type(y.astype(BF16).astype(F32), U32)
    return (bits[:, :c] >> 16) | (bits[:, c:] & jnp.uint32(HI_MASK))


def _unpack_bf16_pairs(p):
    lo = lax.bitcast_convert_type(p << 16, F32)
    hi = lax.bitcast_convert_type(p & jnp.uint32(HI_MASK), F32)
    return jnp.concatenate([lo, hi], axis=1)


def _inproj_kernel(x_ref, w_ref, wg_ref, bg_ref, o_ref, g_ref, xb_ref):
    @pl.when(pl.program_id(1) == 0)
    def _():
        xb_ref[...] = x_ref[...].astype(BF16)
        nt_dims = (((1,), (1,)), ((), ()))
        g_ref[0] = lax.dot_general(wg_ref[...], xb_ref[...], nt_dims, preferred_element_type=F32) + bg_ref[...]

    o_ref[...] = jnp.dot(xb_ref[...], w_ref[...], preferred_element_type=F32).astype(o_ref.dtype)


def _inproj(x, w, w_gate_t, bias_gate, seq):
    t, d = x.shape
    n = w.shape[1]
    r = w_gate_t.shape[0]
    tm, tn = INPROJ_TILE_M, INPROJ_TILE_N
    per_seq = seq // tm
    return pl.pallas_call(
        _inproj_kernel,
        grid=(t // tm, n // tn),
        in_specs=[pl.BlockSpec((tm, d), lambda i, j: (i, 0)),
                  pl.BlockSpec((d, tn), lambda i, j: (0, j)),
                  pl.BlockSpec((r, d), lambda i, j: (0, 0)),
                  pl.BlockSpec((r, 1), lambda i, j: (0, 0))],
        out_specs=[pl.BlockSpec((tm, tn), lambda i, j: (i, j)),
                   pl.BlockSpec((1, r, tm), lambda i, j: (i // per_seq, 0, i % per_seq))],
        out_shape=[jax.ShapeDtypeStruct((t, n), BF16), jax.ShapeDtypeStruct((t // seq, r, seq), F32)],
        scratch_shapes=[pltpu.VMEM((tm, d), BF16)],
        compiler_params=_cparams(("parallel", "arbitrary")),
        name="inproj",
    )(x, w, w_gate_t, bias_gate)


def _qkconv_kernel(prev_ref, main_ref, next_ref, w_ref, o_ref, *, ts, scale, transpose):
    t = pl.program_id(1)
    nt = pl.num_programs(1)
    main = main_ref[0].astype(F32)
    prev = prev_ref[0].astype(F32)[BF16_ROWS - SUBLANES:]
    nxt = next_ref[0].astype(F32)[:SUBLANES]
    prev = jnp.where(t > 0, prev, 0.0)
    nxt = jnp.where(t < nt - 1, nxt, 0.0)
    ext = jnp.concatenate([prev, main, nxt], axis=0)
    w = w_ref[...]
    pad = CONV_WIDTH // 2
    acc = jnp.zeros_like(main)
    for j in range(CONV_WIDTH):
        off = SUBLANES - pad + j
        acc = acc + ext[off:off + ts] * w[j:j + 1]
    y = acc * _sigmoid(acc) * scale
    if transpose:
        o_ref[0] = y.T.astype(o_ref.dtype)
    else:
        o_ref[0] = y.astype(o_ref.dtype)


def _qkconv(proj3, conv_w, col0, scale, transpose):
    bsz, seq, _ = proj3.shape
    ts = SEQ_TILE
    nt = seq // ts
    cw = 2 * LANES
    ncol = conv_w.shape[1] // cw
    hb = ts // BF16_ROWS
    n_hb = seq // BF16_ROWS
    kern = functools.partial(_qkconv_kernel, ts=ts, scale=scale, transpose=transpose)
    if transpose:
        out_shape = jax.ShapeDtypeStruct((bsz, ncol * cw, seq), BF16)
        out_spec = pl.BlockSpec((1, cw, ts), lambda b, t, j: (b, j, t))
    else:
        out_shape = jax.ShapeDtypeStruct((bsz, seq, ncol * cw), BF16)
        out_spec = pl.BlockSpec((1, ts, cw), lambda b, t, j: (b, t, j))
    return pl.pallas_call(
        kern,
        grid=(bsz, nt, ncol),
        in_specs=[pl.BlockSpec((1, BF16_ROWS, cw), lambda b, t, j: (b, jnp.maximum(t * hb - 1, 0), col0 + j)),
                  pl.BlockSpec((1, ts, cw), lambda b, t, j: (b, t, col0 + j)),
                  pl.BlockSpec((1, BF16_ROWS, cw),
                               lambda b, t, j: (b, jnp.minimum((t + 1) * hb, n_hb - 1), col0 + j)),
                  pl.BlockSpec((CONV_WIDTH, cw), lambda b, t, j: (0, j))],
        out_specs=out_spec,
        out_shape=out_shape,
        compiler_params=_cparams(("parallel", "parallel", "parallel")),
        name="qkconv_t" if transpose else "qkconv",
    )(proj3, proj3, proj3, conv_w)


def _pool_kernel(prev_ref, main_ref, next_ref, band_ref, pw_ref, ps_ref, o_ref, *, ts, seq, cw):
    t = pl.program_id(1)
    nt = pl.num_programs(1)
    prev = jnp.where(t > 0, prev_ref[0], jnp.zeros_like(prev_ref[0]))
    nxt = jnp.where(t < nt - 1, next_ref[0], jnp.zeros_like(next_ref[0]))
    ext = jnp.concatenate([prev, main_ref[0], nxt], axis=0)
    sub = POOL_SUB
    for g in range(POOL_GROUPS):
        hw = 1 << g
        cols = slice(g * cw, (g + 1) * cw)
        for j in range(ts // sub):
            rows = ext[j * sub:j * sub + sub + 2 * LANES, cols]
            s = jnp.dot(band_ref[g], rows, preferred_element_type=F32)
            tabs = t * ts + j * sub + lax.broadcasted_iota(I32, (sub, 1), 0)
            cnt = jnp.minimum(tabs + hw, seq) - jnp.maximum(tabs - hw, 0)
            centre = rows[LANES:LANES + sub].astype(F32)
            pooled = s / cnt.astype(F32) - centre
            mixed = jnp.dot(pooled.astype(BF16), pw_ref[g], preferred_element_type=F32) * ps_ref[:, cols]
            o_ref[0, j * sub:(j + 1) * sub, cols] = mixed.astype(o_ref.dtype)


def _pool(proj3, pool_w, pool_scale):
    bsz, seq, _ = proj3.shape
    ts = SEQ_TILE
    nt = seq // ts
    cw = pool_w.shape[-1]
    width = POOL_GROUPS * cw
    hb = ts // LANES
    n_hb = seq // LANES
    i = np.arange(POOL_SUB)[:, None]
    c = np.arange(POOL_SUB + 2 * LANES)[None, :] - LANES
    band = np.stack([(c >= i - (1 << g)) & (c < i + (1 << g)) for g in range(POOL_GROUPS)])
    band = jnp.asarray(band.astype(np.float32), BF16)
    kern = functools.partial(_pool_kernel, ts=ts, seq=seq, cw=cw)
    return pl.pallas_call(
        kern,
        grid=(bsz, nt),
        in_specs=[pl.BlockSpec((1, LANES, width), lambda b, t: (b, jnp.maximum(t * hb - 1, 0), 0)),
                  pl.BlockSpec((1, ts, width), lambda b, t: (b, t, 0)),
                  pl.BlockSpec((1, LANES, width), lambda b, t: (b, jnp.minimum((t + 1) * hb, n_hb - 1), 0)),
                  pl.BlockSpec(band.shape, lambda b, t: (0, 0, 0)),
                  pl.BlockSpec(pool_w.shape, lambda b, t: (0, 0, 0)),
                  pl.BlockSpec((1, width), lambda b, t: (0, 0))],
        out_specs=pl.BlockSpec((1, ts, width), lambda b, t: (b, t, 0)),
        out_shape=jax.ShapeDtypeStruct((bsz, seq, width), BF16),
        compiler_params=_cparams(("parallel", "parallel")),
        name="pool",
    )(proj3, proj3, proj3, band, pool_w, pool_scale)


def _dot_split(val_r, mask):
    hi = val_r.astype(BF16)
    lo = (val_r - hi.astype(F32)).astype(BF16)
    return (jnp.dot(hi, mask, preferred_element_type=F32) + jnp.dot(lo, mask, preferred_element_type=F32))


def _dot_split_t(mask, val_r):
    hi = val_r.astype(BF16)
    lo = (val_r - hi.astype(F32)).astype(BF16)
    nt_dims = (((1,), (1,)), ((), ()))
    return (lax.dot_general(mask, hi, nt_dims, preferred_element_type=F32)
            + lax.dot_general(mask, lo, nt_dims, preferred_element_type=F32))


def _mlstm_kernel(q_ref, kt_ref, v_ref, uo_ref, gr_ref, nw_ref, o_ref,
                  cf_ref, nf_ref, mf_ref, cb_ref, nb_ref, mb_ref, cbs_ref, nbs_ref, mbs_ref,
                  *, chunk, n_chunks, heads):
    L = chunk
    nb = q_ref.shape[0]
    dqk = kt_ref.shape[1] // heads
    dv = v_ref.shape[2] // heads
    p = pl.program_id(1)
    c = pl.program_id(2)
    row = lax.broadcasted_iota(I32, (L, L), 0)
    col = lax.broadcasted_iota(I32, (L, L), 1)
    tri_le = row <= col
    tri_ge = row >= col
    m_le = jnp.where(tri_le, 1.0, 0.0).astype(BF16)
    m_ge = jnp.where(tri_ge, 1.0, 0.0).astype(BF16)
    lane_r = lax.broadcasted_iota(I32, (1, L), 1)
    neg_inf = jnp.float32(-jnp.inf)

    chains = [(bb, hd) for bb in range(nb) for hd in range(heads)]
    hs = range(len(chains))
    kts = [kt_ref[bb, hd * dqk:(hd + 1) * dqk, :] for bb, hd in chains]
    vs = [v_ref[bb, :, hd * dv:(hd + 1) * dv] for bb, hd in chains]
    gates = [gr_ref[bb, hd * SUBLANES:(hd + 1) * SUBLANES, :] for bb, hd in chains]
    lf_r = [jax.nn.log_sigmoid(g) for g in gates]

    def update_state(c_ref, n_ref, m_ref, g_r, tot):
        m_prev = [m_ref[hh] for hh in hs]
        m_new = [jnp.maximum(tot[hh] + m_prev[hh], jnp.max(g_r[hh], axis=1, keepdims=True)) for hh in hs]
        decay = [jnp.exp(tot[hh] + m_prev[hh] - m_new[hh]) for hh in hs]
        kw = [kts[hh].astype(F32) * jnp.exp(g_r[hh] - m_new[hh]) for hh in hs]
        upd = [jnp.dot(kw[hh].astype(BF16), vs[hh], preferred_element_type=F32) for hh in hs]
        for hh in hs:
            c_ref[hh] = decay[hh] * c_ref[hh] + upd[hh]
            n_ref[hh] = decay[hh] * n_ref[hh] + jnp.sum(kw[hh], axis=1, keepdims=True)
            m_ref[hh] = m_new[hh]

    @pl.when(p == 0)
    def _backward_states():
        @pl.when(c == 0)
        def _():
            cb_ref[...] = jnp.zeros_like(cb_ref)
            nb_ref[...] = jnp.zeros_like(nb_ref)
            mb_ref[...] = jnp.zeros_like(mb_ref)

        cc = n_chunks - 1 - c
        cbs_ref[cc] = cb_ref[...].astype(BF16)
        nbs_ref[cc] = nb_ref[...]
        mbs_ref[cc] = mb_ref[...]
        a_r = [_dot_split(lf_r[hh], m_ge)[3:4] for hh in hs]
        a0 = [jnp.sum(jnp.where(lane_r == 0, a_r[hh], 0.0), axis=1, keepdims=True) for hh in hs]
        g_r = [a0[hh] - a_r[hh] + gates[hh][2:3] for hh in hs]
        update_state(cb_ref, nb_ref, mb_ref, g_r, a0)

    @pl.when(p == 1)
    def _outputs():
        @pl.when(c == 0)
        def _():
            cf_ref[...] = jnp.zeros_like(cf_ref)
            nf_ref[...] = jnp.zeros_like(nf_ref)
            mf_ref[...] = jnp.zeros_like(mf_ref)

        nb_in = nbs_ref[c]
        mb_in = mbs_ref[c]
        cb_in = cbs_ref[c]
        qs = [q_ref[bb, :, hd * dqk:(hd + 1) * dqk] for bb, hd in chains]
        b_r = [_dot_split(lf_r[hh], m_le)[1:2] for hh in hs]
        a_r = [_dot_split(lf_r[hh], m_ge)[3:4] for hh in hs]
        b_c = [_dot_split_t(m_ge, lf_r[hh])[:, 1:2] for hh in hs]
        a_c = [_dot_split_t(m_le, lf_r[hh])[:, 3:4] for hh in hs]
        li_f = [gates[hh][0:1] for hh in hs]
        li_b = [gates[hh][2:3] for hh in hs]

        s = [jnp.dot(qs[hh], kts[hh], preferred_element_type=F32) for hh in hs]
        nlane = lax.broadcasted_iota(I32, (dqk, LANES), 1)
        nmat = [jnp.where(nlane == 0, nf_ref[hh], jnp.where(nlane == 1, nb_in[hh], 0.0)).astype(BF16)
                for hh in hs]
        qn = [jnp.dot(qs[hh], nmat[hh], preferred_element_type=F32) for hh in hs]

        def direction(d, mask, cum_c, m_prev, qn_col):
            d = [jnp.where(mask, d[hh], neg_inf) for hh in hs]
            m_inter = [cum_c[hh] + m_prev[hh] for hh in hs]
            m_t = [jnp.maximum(m_inter[hh], jnp.max(d[hh], axis=1, keepdims=True)) for hh in hs]
            pmat = [jnp.exp(d[hh] - m_t[hh]) * s[hh] for hh in hs]
            w_inter = [jnp.exp(m_inter[hh] - m_t[hh]) for hh in hs]
            den = [jnp.sum(pmat[hh], axis=1, keepdims=True) + w_inter[hh] * qn_col[hh] for hh in hs]
            r = [1.0 / jnp.maximum(jnp.abs(den[hh]), jnp.exp(-m_t[hh])) for hh in hs]
            return [pmat[hh] * r[hh] for hh in hs], [w_inter[hh] * r[hh] for hh in hs]

        pf, sf = direction([b_c[hh] - (b_r[hh] - li_f[hh]) for hh in hs], tri_ge, b_c,
                           [mf_ref[hh] for hh in hs], [qn[hh][:, 0:1] for hh in hs])
        pb, sb = direction([a_c[hh] - (a_r[hh] - li_b[hh]) for hh in hs], tri_le, a_c,
                           [mb_in[hh] for hh in hs], [qn[hh][:, 1:2] for hh in hs])
        qf = [qs[hh].astype(F32) for hh in hs]
        h = [jnp.dot((pf[hh] + pb[hh]).astype(BF16), vs[hh], preferred_element_type=F32)
             + jnp.dot((qf[hh] * sf[hh]).astype(BF16), cf_ref[hh].astype(BF16), preferred_element_type=F32)
             + jnp.dot((qf[hh] * sb[hh]).astype(BF16), cb_in[hh], preferred_element_type=F32) for hh in hs]

        mu = [jnp.mean(h[hh], axis=1, keepdims=True) for hh in hs]
        hc = [h[hh] - mu[hh] for hh in hs]
        var = [jnp.mean(hc[hh] * hc[hh], axis=1, keepdims=True) for hh in hs]
        hn = [hc[hh] * lax.rsqrt(var[hh] + LN_EPS) * nw_ref[chains[hh][1]] for hh in hs]
        for hh, (bb, hd) in enumerate(chains):
            gate_o = _sigmoid(uo_ref[bb, :, hd * dv:(hd + 1) * dv].astype(F32))
            o_ref[bb, :, hd * dv:(hd + 1) * dv] = (gate_o * hn[hh]).astype(o_ref.dtype)

        b_last = [jnp.sum(jnp.where(lane_r == L - 1, b_r[hh], 0.0), axis=1, keepdims=True) for hh in hs]
        update_state(cf_ref, nf_ref, mf_ref, [b_last[hh] - b_r[hh] + li_f[hh] for hh in hs], b_last)


def _mlstm(q, kt, proj3, gates, norm_w, v_off, o_off):
    bsz, seq, qw = q.shape
    dqk = qw // N_HEADS
    dv = norm_w.shape[-1]
    L = MLSTM_CHUNK
    nc = seq // L
    hg = MLSTM_HEADS_PER_STEP
    bg = MLSTM_BATCH_PER_STEP
    groups = N_HEADS // hg
    assert N_HEADS % hg == 0 and bsz % bg == 0 and v_off % (hg * dv) == 0 and o_off % (hg * dv) == 0
    v_blk0 = v_off // (hg * dv)
    o_blk0 = o_off // (hg * dv)
    kern = functools.partial(_mlstm_kernel, chunk=L, n_chunks=nc, heads=hg)

    def chunk_of(p, c):
        return jnp.where(p == 0, nc - 1 - c, c)

    def out_chunk(p, c):
        return jnp.where(p == 0, 0, c)

    return pl.pallas_call(
        kern,
        grid=((bsz // bg) * groups, 2, nc),
        in_specs=[
            pl.BlockSpec((bg, L, hg * dqk), lambda g, p, c: (g // groups, out_chunk(p, c), g % groups)),
            pl.BlockSpec((bg, hg * dqk, L), lambda g, p, c: (g // groups, g % groups, chunk_of(p, c))),
            pl.BlockSpec((bg, L, hg * dv), lambda g, p, c: (g // groups, chunk_of(p, c), v_blk0 + g % groups)),
            pl.BlockSpec((bg, L, hg * dv), lambda g, p, c: (g // groups, out_chunk(p, c), o_blk0 + g % groups)),
            pl.BlockSpec((bg, hg * SUBLANES, L), lambda g, p, c: (g // groups, g % groups, chunk_of(p, c))),
            pl.BlockSpec((hg, 1, dv), lambda g, p, c: (g % groups, 0, 0)),
        ],
        out_specs=pl.BlockSpec((bg, L, hg * dv), lambda g, p, c: (g // groups, out_chunk(p, c), g % groups)),
        out_shape=jax.ShapeDtypeStruct((bsz, seq, N_HEADS * dv), BF16),
        scratch_shapes=[
            pltpu.VMEM((bg * hg, dqk, dv), F32), pltpu.VMEM((bg * hg, dqk, 1), F32),
            pltpu.VMEM((bg * hg, 1, 1), F32),
            pltpu.VMEM((bg * hg, dqk, dv), F32), pltpu.VMEM((bg * hg, dqk, 1), F32),
            pltpu.VMEM((bg * hg, 1, 1), F32),
            pltpu.VMEM((nc, bg * hg, dqk, dv), BF16), pltpu.VMEM((nc, bg * hg, dqk, 1), F32),
            pltpu.VMEM((nc, bg * hg, 1, 1), F32),
        ],
        compiler_params=_cparams(("parallel", "arbitrary", "arbitrary")),
        name="mlstm",
    )(q, kt, proj3, proj3, gates, norm_w)


def _mixout_kernel(hg_ref, a_ref, uga_ref, ugb_ref, x_ref, wbd_ref, wo_ref, bga_ref, bgb_ref,
                   g_ref, b_ref, o_ref, op_ref, *, alpha):
    branch_b = jnp.dot(hg_ref[...], wbd_ref[...], preferred_element_type=F32)
    ga = _sigmoid(uga_ref[...].astype(F32) + bga_ref[...])
    gb = _sigmoid(ugb_ref[...].astype(F32) + bgb_ref[...])
    merged = ga * a_ref[...].astype(F32) + gb * branch_b
    mix = jnp.dot(merged.astype(BF16), wo_ref[...], preferred_element_type=F32)
    y = _layer_norm(alpha * x_ref[...] + mix, g_ref[...], b_ref[...])
    o_ref[...] = y
    op_ref[...] = _pack_bf16_pairs(y)


def _mixout(hg, branch_a, proj, x, w_b_down, w_out, b_gate, ln_g, ln_b, ga_col, alpha):
    t, d = x.shape
    tm = ROW_TILE
    inner = hg.shape[1]
    row = lambda i: (i, 0)
    const = lambda i: (0, 0)
    return pl.pallas_call(
        functools.partial(_mixout_kernel, alpha=alpha),
        grid=(t // tm,),
        in_specs=[pl.BlockSpec((tm, inner), row),
                  pl.BlockSpec((tm, d), row),
                  pl.BlockSpec((tm, d), lambda i: (i, ga_col)),
                  pl.BlockSpec((tm, d), lambda i: (i, ga_col + 1)),
                  pl.BlockSpec((tm, d), row),
                  pl.BlockSpec((inner, d), const),
                  pl.BlockSpec((d, d), const),
                  pl.BlockSpec((1, d), const),
                  pl.BlockSpec((1, d), lambda i: (0, 1)),
                  pl.BlockSpec((1, d), const),
                  pl.BlockSpec((1, d), const)],
        out_specs=[pl.BlockSpec((tm, d), row), pl.BlockSpec((tm, d // 2), row)],
        out_shape=[jax.ShapeDtypeStruct((t, d), F32), jax.ShapeDtypeStruct((t, d // 2), U32)],
        compiler_params=_cparams(("parallel",)),
        name="mixout",
    )(hg, branch_a, proj, proj, x, w_b_down, w_out, b_gate, b_gate, ln_g, ln_b)


def _router_kernel(x_ref, wh_ref, wl_ref, br_ref, eidx_ref, wts_ref, rank_ref, cnt_ref, run_ref,
                   *, n_experts):
    i = pl.program_id(0)
    tm = x_ref.shape[0]
    ne = n_experts
    per_group = ne // N_ROUTE_GROUPS

    @pl.when(i == 0)
    def _():
        run_ref[...] = jnp.zeros_like(run_ref)

    x = x_ref[...]
    xh = x.astype(BF16)
    xl = (x - xh.astype(F32)).astype(BF16)
    nt_dims = (((1,), (1,)), ((), ()))
    logits = (lax.dot_general(wh_ref[...], xh, nt_dims, preferred_element_type=F32)
              + lax.dot_general(wh_ref[...], xl, nt_dims, preferred_element_type=F32)
              + lax.dot_general(wl_ref[...], xh, nt_dims, preferred_element_type=F32))
    scores = _sigmoid(logits)
    sel = scores + br_ref[...]
    neg = jnp.float32(-jnp.inf)
    big = jnp.float32(1e9)
    rowi = lax.broadcasted_iota(I32, (ne, tm), 0).astype(F32)

    gi = lax.broadcasted_iota(I32, (N_ROUTE_GROUPS, tm), 0).astype(F32)
    work = jnp.zeros((N_ROUTE_GROUPS, tm), F32)
    for g in range(N_ROUTE_GROUPS):
        blk = sel[g * per_group:(g + 1) * per_group]
        ri = lax.broadcasted_iota(I32, (per_group, tm), 0).astype(F32) + float(g * per_group)
        m1 = jnp.max(blk, axis=0, keepdims=True)
        i1 = jnp.min(jnp.where(blk == m1, ri, big), axis=0, keepdims=True)
        m2 = jnp.max(jnp.where(ri == i1, neg, blk), axis=0, keepdims=True)
        work = jnp.where(gi == float(g), m1 + m2, work)
    row_group = jnp.floor(rowi * (1.0 / per_group))
    allowed = jnp.zeros((ne, tm), F32)
    for _ in range(TOPK_GROUPS):
        m = jnp.max(work, axis=0, keepdims=True)
        idx = jnp.min(jnp.where(work == m, gi, big), axis=0, keepdims=True)
        work = jnp.where(gi == idx, neg, work)
        allowed = jnp.where(row_group == idx, 1.0, allowed)
    selm = jnp.where(allowed > 0.5, sel, neg)

    member = jnp.zeros((ne, tm), F32)
    idxs, wks = [], []
    for _ in range(TOP_K):
        m = jnp.max(selm, axis=0, keepdims=True)
        idx = jnp.min(jnp.where(selm == m, rowi, big), axis=0, keepdims=True)
        hit = rowi == idx
        wks.append(jnp.sum(jnp.where(hit, scores, 0.0), axis=0, keepdims=True))
        idxs.append(idx)
        member = jnp.where(hit, 1.0, member)
        selm = jnp.where(hit, neg, selm)
    wsum = wks[0]
    for wk in wks[1:]:
        wsum = wsum + wk

    ti = lax.broadcasted_iota(I32, (tm, tm), 0)
    tj = lax.broadcasted_iota(I32, (tm, tm), 1)
    strict = jnp.where(ti < tj, 1.0, 0.0).astype(BF16)
    prefix = jnp.dot(member.astype(BF16), strict, preferred_element_type=F32) + run_ref[...]
    ranks = [jnp.sum(jnp.where(rowi == idx, prefix, 0.0), axis=0, keepdims=True) for idx in idxs]
    run_new = run_ref[...] + jnp.sum(member, axis=1, keepdims=True)
    run_ref[...] = run_new

    eidx_ref[...] = jnp.concatenate(idxs, axis=0).astype(I32)
    wts_ref[...] = jnp.concatenate([wk / wsum * ROUTED_SCALE for wk in wks], axis=0)
    rank_ref[...] = jnp.concatenate(ranks, axis=0).astype(I32)
    cnt_ref[...] = jnp.broadcast_to(run_new, cnt_ref.shape).astype(I32)


def _router(x1, wr_hi, wr_lo, b_router):
    t, d = x1.shape
    ne = wr_hi.shape[0]
    tm = ROUTER_TILE
    kern = functools.partial(_router_kernel, n_experts=ne)
    tok = lambda i: (0, i)
    const = lambda i: (0, 0)
    return pl.pallas_call(
        kern,
        grid=(t // tm,),
        in_specs=[pl.BlockSpec((tm, d), lambda i: (i, 0)),
                  pl.BlockSpec((ne, d), const),
                  pl.BlockSpec((ne, d), const),
                  pl.BlockSpec((ne, 1), const)],
        out_specs=[pl.BlockSpec((TOP_K, tm), tok), pl.BlockSpec((TOP_K, tm), tok),
                   pl.BlockSpec((TOP_K, tm), tok), pl.BlockSpec((ne, LANES), const)],
        out_shape=[jax.ShapeDtypeStruct((TOP_K, t), I32), jax.ShapeDtypeStruct((TOP_K, t), F32),
                   jax.ShapeDtypeStruct((TOP_K, t), I32), jax.ShapeDtypeStruct((ne, LANES), I32)],
        scratch_shapes=[pltpu.VMEM((ne, 1), F32)],
        compiler_params=_cparams(("arbitrary",)),
        name="router",
    )(x1, wr_hi, wr_lo, b_router)


def _slots_kernel(eidx_ref, rank_ref, ps_ref, pos_ref, *, n_experts):
    tm = eidx_ref.shape[1]
    rowi = lax.broadcasted_iota(I32, (n_experts, tm), 0)
    eidx = eidx_ref[...]
    starts = ps_ref[...]
    base = [jnp.sum(jnp.where(rowi == eidx[k:k + 1], starts, 0.0), axis=0, keepdims=True)
            for k in range(TOP_K)]
    pos_ref[...] = jnp.concatenate(base, axis=0).astype(I32) + rank_ref[...]


def _slots(eidx, rank, pstarts):
    k, t = eidx.shape
    ne = pstarts.shape[0]
    tm = SLOT_TILE
    tok = lambda i: (0, i)
    return pl.pallas_call(
        functools.partial(_slots_kernel, n_experts=ne),
        grid=(t // tm,),
        in_specs=[pl.BlockSpec((k, tm), tok), pl.BlockSpec((k, tm), tok),
                  pl.BlockSpec((ne, 1), lambda i: (0, 0))],
        out_specs=pl.BlockSpec((k, tm), tok),
        out_shape=jax.ShapeDtypeStruct((k, t), I32),
        compiler_params=_cparams(("parallel",)),
        name="slots",
    )(eidx, rank, pstarts.astype(F32).reshape(ne, 1))


def _sc_worker_id():
    return lax.axis_index("subcore") * SC_CORES + lax.axis_index("core")


def _sc_mesh():
    return plsc.VectorSubcoreMesh(core_axis_name="core", subcore_axis_name="subcore")


def _sc_vector_params():
    cp = pltpu.CompilerParams()
    if "needs_layout_passes" in pltpu.CompilerParams.__dataclass_fields__:
        cp = dataclasses.replace(cp, needs_layout_passes=False)
    return cp


def _sc_gather(table, idx):
    n = idx.shape[0]
    w = table.shape[1]
    ch = SC_GATHER_ROWS
    n_ch = n // (SC_WORKERS * ch)
    assert n % (SC_WORKERS * ch * 2) == 0

    @functools.partial(
        pl.kernel, mesh=_sc_mesh(),
        out_type=jax.ShapeDtypeStruct((n, w), table.dtype),
        scratch_types=[pltpu.VMEM((n_ch, ch), I32), pltpu.VMEM((2, ch, w), table.dtype),
                       pltpu.SemaphoreType.DMA((2,))],
    )
    def kern(table_hbm, idx_hbm, out_hbm, idx_v, rows_v, sem):
        first = _sc_worker_id() * n_ch
        pltpu.sync_copy(idx_hbm.at[pl.ds(first, n_ch)], idx_v)

        def gather(j, b):
            return pltpu.make_async_copy(table_hbm.at[idx_v.at[j]], rows_v.at[b], sem.at[b])

        gather(0, 0).start()

        @pl.loop(0, n_ch, step=2)
        def _(j):
            for b in range(2):
                jj = j + b

                @pl.when(jj + 1 < n_ch)
                def _():
                    gather(jj + 1, 1 - b).start()

                gather(jj, b).wait()
                row0 = pl.multiple_of((first + jj) * ch, ch)
                pltpu.sync_copy(rows_v.at[b], out_hbm.at[pl.ds(row0, ch)])

    return kern(table, idx.reshape(n // ch, ch))


def _sc_invert(pos_flat, n_slots, n_tokens):
    n = pos_flat.shape[0]
    per_w = n_slots // SC_WORKERS
    chunk = SC_SCAN_CHUNK
    assert n_slots % (SC_WORKERS * SC_LANES) == 0 and n % chunk == 0 and n_tokens % chunk == 0

    @functools.partial(
        pl.kernel, mesh=_sc_mesh(),
        out_type=jax.ShapeDtypeStruct((n_slots,), I32),
        scratch_types=[pltpu.VMEM((per_w,), I32), pltpu.VMEM((chunk,), I32)],
        compiler_params=_sc_vector_params(),
    )
    def kern(pos_hbm, out_hbm, table_v, pos_v):
        lo = _sc_worker_id() * per_w
        lane = lax.iota(I32, SC_LANES)

        @pl.loop(0, per_w // SC_LANES)
        def _(i):
            off = pl.multiple_of(i * SC_LANES, SC_LANES)
            table_v[pl.ds(off, SC_LANES)] = lax.rem(lo + off + lane, jnp.full((SC_LANES,), n_tokens, I32))

        @pl.loop(0, n // chunk)
        def _(c):
            pltpu.sync_copy(pos_hbm.at[pl.ds(pl.multiple_of(c * chunk, chunk), chunk)], pos_v)
            tok0 = lax.rem(c, n_tokens // chunk) * chunk

            @plsc.parallel_loop(0, chunk // SC_LANES, 1, unroll=SC_SCAN_UNROLL)
            def _(j):
                off = pl.multiple_of(j * SC_LANES, SC_LANES)
                local = pos_v[pl.ds(off, SC_LANES)] - lo
                mine = (local >= 0) & (local < per_w)
                plsc.store_scatter(table_v, [jnp.where(mine, local, 0)], tok0 + off + lane, mask=mine)

        pltpu.sync_copy(table_v, out_hbm.at[pl.ds(pl.multiple_of(lo, SC_LANES), per_w)])

    return kern(pos_flat)


def _experts_kernel(te_ref, nu_ref, ord_ref, nxt_ref, nxt2_ref, xs_hbm, wg_hbm, wu_hbm, wd_hbm, ys_ref,
                    xbuf_ref, wgf_ref, wuf_ref, wdf_ref, wgb_ref, wub_ref, wdb_ref, sem_ref, xsem_ref):
    i = pl.program_id(0)
    n_used = nu_ref[0]
    tile = xbuf_ref.shape[1]

    def row_copy(step):
        slot = lax.rem(step, ROW_SLOTS)
        src = xs_hbm.at[pl.ds(pl.multiple_of(step * tile, tile), tile)]
        return pltpu.make_async_copy(src, xbuf_ref.at[slot], xsem_ref.at[slot])

    def weight_copies(e, slot):
        return (pltpu.make_async_copy(wg_hbm.at[e], wgf_ref.at[slot], sem_ref.at[slot]),
                pltpu.make_async_copy(wu_hbm.at[e], wuf_ref.at[slot], sem_ref.at[slot]),
                pltpu.make_async_copy(wd_hbm.at[e], wdf_ref.at[slot], sem_ref.at[slot]))

    @pl.when(i < n_used)
    def _():
        @pl.when(i == 0)
        def _():
            for ahead in range(ROW_SLOTS - 1):
                @pl.when(ahead < n_used)
                def _():
                    row_copy(ahead).start()

        @pl.when(i + (ROW_SLOTS - 1) < n_used)
        def _():
            row_copy(i + (ROW_SLOTS - 1)).start()

        e = te_ref[i]
        ordinal = ord_ref[i]
        slot = lax.rem(ordinal, WEIGHT_SLOTS)
        first_tile_of_expert = jnp.logical_or(i == 0, e != te_ref[jnp.maximum(i - 1, 0)])

        @pl.when(i == 0)
        def _():
            for cp in weight_copies(e, slot):
                cp.start(priority=WEIGHT_DMA_PRIORITY)

            @pl.when(nxt_ref[i] >= 0)
            def _():
                for cp in weight_copies(nxt_ref[i], lax.rem(ordinal + 1, WEIGHT_SLOTS)):
                    cp.start(priority=WEIGHT_DMA_PRIORITY)

        @pl.when(first_tile_of_expert)
        def _():
            ahead = nxt2_ref[i]

            @pl.when(ahead >= 0)
            def _():
                for cp in weight_copies(ahead, lax.rem(ordinal + 2, WEIGHT_SLOTS)):
                    cp.start(priority=WEIGHT_DMA_PRIORITY)

            for cp in weight_copies(e, slot):
                cp.wait()
            wgb_ref[...] = wgf_ref[slot].astype(BF16)
            wub_ref[...] = wuf_ref[slot].astype(BF16)
            wdb_ref[...] = wdf_ref[slot].astype(BF16)

        row_copy(i).wait()
        x = _unpack_bf16_pairs(xbuf_ref[lax.rem(i, ROW_SLOTS)]).astype(BF16)
        gate = jnp.dot(x, wgb_ref[...], preferred_element_type=F32)
        up = jnp.dot(x, wub_ref[...], preferred_element_type=F32)
        hid = (gate * _sigmoid(gate) * up).astype(BF16)
        y = jnp.dot(hid, wdb_ref[...], preferred_element_type=F32)
        ys_ref[...] = _pack_bf16_pairs(y)

    @pl.when(i >= n_used)
    def _():
        ys_ref[...] = jnp.zeros_like(ys_ref)


def _experts(tile_expert, n_used, expert_ord, expert_next, expert_next2, xs, w_gate_e, w_up_e, w_down_e):
    n_slots, w = xs.shape
    tile = EXPERT_TILE
    n_tiles = n_slots // tile
    _, d, de = w_gate_e.shape
    grid_spec = pltpu.PrefetchScalarGridSpec(
        num_scalar_prefetch=5,
        grid=(n_tiles,),
        in_specs=[pl.BlockSpec(memory_space=pl.ANY),
                  pl.BlockSpec(memory_space=pl.ANY),
                  pl.BlockSpec(memory_space=pl.ANY),
                  pl.BlockSpec(memory_space=pl.ANY)],
        out_specs=pl.BlockSpec((tile, w), lambda i, *_: (i, 0)),
        scratch_shapes=[pltpu.VMEM((ROW_SLOTS, tile, w), U32),
                        pltpu.VMEM((WEIGHT_SLOTS, d, de), F32), pltpu.VMEM((WEIGHT_SLOTS, d, de), F32),
                        pltpu.VMEM((WEIGHT_SLOTS, de, d), F32),
                        pltpu.VMEM((d, de), BF16), pltpu.VMEM((d, de), BF16), pltpu.VMEM((de, d), BF16),
                        pltpu.SemaphoreType.DMA((WEIGHT_SLOTS,)), pltpu.SemaphoreType.DMA((ROW_SLOTS,))],
    )
    return pl.pallas_call(
        _experts_kernel,
        grid_spec=grid_spec,
        out_shape=jax.ShapeDtypeStruct((n_slots, w), U32),
        compiler_params=_cparams(("arbitrary",)),
        name="experts",
    )(tile_expert, n_used, expert_ord, expert_next, expert_next2, xs, w_gate_e, w_up_e, w_down_e)


def _combine_kernel(wts_ref, x_ref, xp_ref, yg_ref, wgs_ref, wus_ref, wds_ref, g_ref, b_ref, o_ref, *, alpha):
    wts = wts_ref[...]
    routed = jnp.zeros(x_ref.shape, F32)
    for k in range(TOP_K):
        routed = routed + wts[:, k:k + 1] * _unpack_bf16_pairs(yg_ref[k])
    xb = _unpack_bf16_pairs(xp_ref[...]).astype(BF16)
    gate = jnp.dot(xb, wgs_ref[...], preferred_element_type=F32)
    up = jnp.dot(xb, wus_ref[...], preferred_element_type=F32)
    hid = (gate * _sigmoid(gate) * up).astype(BF16)
    shared = jnp.dot(hid, wds_ref[...], preferred_element_type=F32)
    o_ref[...] = _layer_norm(alpha * x_ref[...] + (routed + shared), g_ref[...], b_ref[...])


def _combine(wts_c, x1, x1p, yg, w_gate_s, w_up_s, w_down_s, ln_g, ln_b, alpha):
    t, d = x1.shape
    w = x1p.shape[1]
    tb = COMBINE_TILE
    ds = w_gate_s.shape[1]
    row = lambda i: (i, 0)
    const = lambda i: (0, 0)
    return pl.pallas_call(
        functools.partial(_combine_kernel, alpha=alpha),
        grid=(t // tb,),
        in_specs=[pl.BlockSpec((tb, TOP_K), row),
                  pl.BlockSpec((tb, d), row),
                  pl.BlockSpec((tb, w), row),
                  pl.BlockSpec((TOP_K, tb, w), lambda i: (0, i, 0)),
                  pl.BlockSpec((d, ds), const),
                  pl.BlockSpec((d, ds), const),
                  pl.BlockSpec((ds, d), const),
                  pl.BlockSpec((1, d), const),
                  pl.BlockSpec((1, d), const)],
        out_specs=pl.BlockSpec((tb, d), row),
        out_shape=jax.ShapeDtypeStruct((t, d), F32),
        compiler_params=_cparams(("parallel",)),
        name="combine",
    )(wts_c, x1, x1p, yg, w_gate_s, w_up_s, w_down_s, ln_g, ln_b)


def _layer(alpha, x, w_in, b_if, b_gate, conv_qk, pool_w, pool_scale, mh_norm_w, w_b_down, w_out,
           ln1_g, ln1_b, w_router, b_router, w_gate_e, w_up_e, w_down_e,
           w_gate_s, w_up_s, w_down_s, ln2_g, ln2_b):
    bsz, seq, d = x.shape
    t = bsz * seq
    heads = N_HEADS
    pool_width = pool_w.shape[0] * pool_w.shape[1]
    qk_cols = conv_qk.shape[1]
    v_cols = mh_norm_w.shape[0] * mh_norm_w.shape[1]
    o_cols = v_cols
    if_cols = b_if.shape[0]
    dv = mh_norm_w.shape[1]
    dqk = qk_cols // (2 * heads)
    off_if = pool_width + qk_cols + v_cols + o_cols
    off_gate = off_if + if_cols

    xf = x.reshape(t, d)
    w_main = jnp.concatenate([w_in[:, :off_if], w_in[:, off_gate:]], axis=1).astype(BF16)
    n_gate = if_cols // heads
    w_if = w_in[:, off_if:off_gate].reshape(d, n_gate, heads).transpose(2, 1, 0)
    w_if = jnp.pad(w_if, ((0, 0), (0, SUBLANES - n_gate), (0, 0))).reshape(heads * SUBLANES, d).astype(BF16)
    bias_if = jnp.pad(b_if.reshape(n_gate, heads).T, ((0, 0), (0, SUBLANES - n_gate))).reshape(heads * SUBLANES, 1)
    n_main = w_main.shape[1]

    proj, gates = _inproj(xf, w_main, w_if, bias_if, seq)
    proj3 = proj.reshape(bsz, seq, n_main)

    branch_a = _pool(proj3, pool_w.astype(BF16), pool_scale.reshape(1, pool_width))

    cw = 2 * LANES
    q_col0 = pool_width // cw
    half = qk_cols // 2
    q = _qkconv(proj3, conv_qk[:, :half], q_col0, float(dqk) ** -0.5, False)
    kt = _qkconv(proj3, conv_qk[:, half:], q_col0 + half // cw, 1.0, True)

    hg = _mlstm(q, kt, proj3, gates, mh_norm_w.reshape(heads, 1, dv),
                pool_width + qk_cols, pool_width + qk_cols + v_cols)

    ga_col = (pool_width + qk_cols + v_cols + o_cols) // d
    x1, x1p = _mixout(hg.reshape(t, heads * dv), branch_a.reshape(t, pool_width), proj, xf,
                      w_b_down.astype(BF16), w_out.astype(BF16), b_gate.reshape(1, 2 * d),
                      ln1_g.reshape(1, d), ln1_b.reshape(1, d), ga_col, alpha)

    ne = w_router.shape[1]
    wr_t = w_router.T
    wr_hi = wr_t.astype(BF16)
    wr_lo = (wr_t - wr_hi.astype(F32)).astype(BF16)
    eidx, wts, rank, cnt = _router(x1, wr_hi, wr_lo, b_router.reshape(ne, 1))

    tile = EXPERT_TILE
    n_tiles = (t * TOP_K) // tile + ne
    counts = cnt[:, 0]
    pcounts = ((counts + tile - 1) // tile) * tile
    pends = jnp.cumsum(pcounts)
    pstarts = pends - pcounts
    pos = _slots(eidx, rank, pstarts)

    n_used = (pends[-1] // tile).astype(I32)
    tile_ids = jnp.minimum(jnp.arange(n_tiles, dtype=I32), n_used - 1)
    tile_expert = jnp.sum((pends[None, :] <= (tile_ids * tile)[:, None]).astype(I32), axis=1)
    tile_expert = jnp.minimum(tile_expert, ne - 1)
    new_expert = jnp.concatenate([jnp.ones((1,), I32), (tile_expert[1:] != tile_expert[:-1]).astype(I32)])
    expert_ord = jnp.cumsum(new_expert) - 1
    candidates = jnp.where(counts > 0, jnp.arange(ne, dtype=I32), ne)
    later_min = lax.cummin(candidates, axis=0, reverse=True)
    next_used = jnp.concatenate([later_min[1:], jnp.full((1,), ne, I32)])
    next_used = jnp.where(next_used >= ne, -1, next_used)
    next_used2 = jnp.where(next_used >= 0, next_used[jnp.maximum(next_used, 0)], -1)
    tile_onehot = tile_expert[:, None] == jnp.arange(ne, dtype=I32)[None, :]
    expert_next = jnp.sum(jnp.where(tile_onehot, next_used[None, :], 0), axis=1).astype(I32)
    expert_next2 = jnp.sum(jnp.where(tile_onehot, next_used2[None, :], 0), axis=1).astype(I32)

    n_slots = n_tiles * tile
    slot_tok = _sc_invert(pos.reshape(-1), n_slots, t)
    xs = _sc_gather(x1p, slot_tok)
    ys = _experts(tile_expert, n_used.reshape(1), expert_ord.astype(I32), expert_next, expert_next2, xs,
                  w_gate_e, w_up_e, w_down_e)
    yg = _sc_gather(ys, pos.reshape(-1)).reshape(TOP_K, t, x1p.shape[1])
    out = _combine(wts.T, x1, x1p, yg, w_gate_s.astype(BF16), w_up_s.astype(BF16),
                   w_down_s.astype(BF16), ln2_g.reshape(1, d), ln2_b.reshape(1, d), alpha)
    return out.reshape(bsz, seq, d)


def kernel(x, w_in, b_if, b_gate, conv_qk, pool_w, pool_scale, mh_norm_w, w_b_down, w_out, ln1_g, ln1_b,
           w_router, b_router, w_gate_e, w_up_e, w_down_e, w_gate_s, w_up_s, w_down_s, ln2_g, ln2_b):
    depth = w_in.shape[0]
    alpha = (2.0 * depth) ** 0.25
    for l in range(depth):
        x = _layer(alpha, x, w_in[l], b_if[l], b_gate[l], conv_qk[l], pool_w[l], pool_scale[l], mh_norm_w[l],
                   w_b_down[l], w_out[l], ln1_g[l], ln1_b[l], w_router[l], b_router[l], w_gate_e[l],
                   w_up_e[l], w_down_e[l], w_gate_s[l], w_up_s[l], w_down_s[l], ln2_g[l], ln2_b[l])
    return x
```

```python
import dataclasses
import functools

import jax
import jax.numpy as jnp
import numpy as np
from jax import lax
from jax.experimental import pallas as pl
from jax.experimental.pallas import tpu as pltpu
from jax.experimental.pallas import tpu_sc as plsc

F32 = jnp.float32
BF16 = jnp.bfloat16
I32 = jnp.int32
U32 = jnp.uint32

N_HEADS = 4
POOL_GROUPS = 4
CONV_WIDTH = 5
LN_EPS = 1e-5
N_ROUTE_GROUPS = 8
TOPK_GROUPS = 4
TOP_K = 8
ROUTED_SCALE = 2.5

LANES = 128
SUBLANES = 8
BF16_ROWS = 16
VMEM_LIMIT = 56 * 1024 * 1024

INPROJ_TILE_M = 2048
INPROJ_TILE_N = 1536
MLSTM_CHUNK = 256
MLSTM_HEADS_PER_STEP = 2
MLSTM_BATCH_PER_STEP = 2
SEQ_TILE = 1024
POOL_SUB = 256
ROW_TILE = 1024
ROUTER_TILE = 1024
EXPERT_TILE = 256
COMBINE_TILE = 512
SLOT_TILE = 2048
WEIGHT_SLOTS = 3
WEIGHT_DMA_PRIORITY = 0
ROW_SLOTS = 3

SC_CORES = 2
SC_SUBCORES = 16
SC_LANES = 16
SC_WORKERS = SC_CORES * SC_SUBCORES
SC_GATHER_ROWS = 64
SC_SCAN_CHUNK = 8192
SC_SCAN_UNROLL = 8

HI_MASK = 0xFFFF0000


def _cparams(sem):
    return pltpu.CompilerParams(dimension_semantics=sem, vmem_limit_bytes=VMEM_LIMIT)


def _sigmoid(x):
    return 1.0 / (1.0 + jnp.exp(-x))


def _layer_norm(y, g, b):
    mu = jnp.mean(y, axis=-1, keepdims=True)
    yc = y - mu
    var = jnp.mean(yc * yc, axis=-1, keepdims=True)
    return yc * lax.rsqrt(var + LN_EPS) * g + b


def _pack_bf16_pairs(y):
    c = y.shape[1] // 2
    bits = lax.bitcast_convert_type(y.astype(BF16).astype(F32), U32)
    return (bits[:, :c] >> 16) | (bits[:, c:] & jnp.uint32(HI_MASK))


def _unpack_bf16_pairs(p):
    lo = lax.bitcast_convert_type(p << 16, F32)
    hi = lax.bitcast_convert_type(p & jnp.uint32(HI_MASK), F32)
    return jnp.concatenate([lo, hi], axis=1)


def _inproj_kernel(x_ref, w_ref, wg_ref, bg_ref, o_ref, g_ref, xb_ref):
    @pl.when(pl.program_id(1) == 0)
    def _():
        xb_ref[...] = x_ref[...].astype(BF16)
        nt_dims = (((1,), (1,)), ((), ()))
        g_ref[0] = lax.dot_general(wg_ref[...], xb_ref[...], nt_dims, preferred_element_type=F32) + bg_ref[...]

    o_ref[...] = jnp.dot(xb_ref[...], w_ref[...], preferred_element_type=F32).astype(o_ref.dtype)


def _inproj(x, w, w_gate_t, bias_gate, seq):
    t, d = x.shape
    n = w.shape[1]
    r = w_gate_t.shape[0]
    tm, tn = INPROJ_TILE_M, INPROJ_TILE_N
    per_seq = seq // tm
    return pl.pallas_call(
        _inproj_kernel,
        grid=(t // tm, n // tn),
        in_specs=[pl.BlockSpec((tm, d), lambda i, j: (i, 0)),
                  pl.BlockSpec((d, tn), lambda i, j: (0, j)),
                  pl.BlockSpec((r, d), lambda i, j: (0, 0)),
                  pl.BlockSpec((r, 1), lambda i, j: (0, 0))],
        out_specs=[pl.BlockSpec((tm, tn), lambda i, j: (i, j)),
                   pl.BlockSpec((1, r, tm), lambda i, j: (i // per_seq, 0, i % per_seq))],
        out_shape=[jax.ShapeDtypeStruct((t, n), BF16), jax.ShapeDtypeStruct((t // seq, r, seq), F32)],
        scratch_shapes=[pltpu.VMEM((tm, d), BF16)],
        compiler_params=_cparams(("parallel", "arbitrary")),
        name="inproj",
    )(x, w, w_gate_t, bias_gate)


def _qkconv_kernel(prev_ref, main_ref, next_ref, w_ref, o_ref, *, ts, scale, transpose):
    t = pl.program_id(1)
    nt = pl.num_programs(1)
    main = main_ref[0].astype(F32)
    prev = prev_ref[0].astype(F32)[BF16_ROWS - SUBLANES:]
    nxt = next_ref[0].astype(F32)[:SUBLANES]
    prev = jnp.where(t > 0, prev, 0.0)
    nxt = jnp.where(t < nt - 1, nxt, 0.0)
    ext = jnp.concatenate([prev, main, nxt], axis=0)
    w = w_ref[...]
    pad = CONV_WIDTH // 2
    acc = jnp.zeros_like(main)
    for j in range(CONV_WIDTH):
        off = SUBLANES - pad + j
        acc = acc + ext[off:off + ts] * w[j:j + 1]
    y = acc * _sigmoid(acc) * scale
    if transpose:
        o_ref[0] = y.T.astype(o_ref.dtype)
    else:
        o_ref[0] = y.astype(o_ref.dtype)


def _qkconv_pair_kernel(qp_ref, qm_ref, qn_ref, kp_ref, km_ref, kn_ref, wq_ref, wk_ref, q_ref, kt_ref,
                        *, ts, q_scale):
    _qkconv_kernel(qp_ref, qm_ref, qn_ref, wq_ref, q_ref, ts=ts, scale=q_scale, transpose=False)
    _qkconv_kernel(kp_ref, km_ref, kn_ref, wk_ref, kt_ref, ts=ts, scale=1.0, transpose=True)


def _qkconv_pair(proj3, conv_q, conv_k, q_col0, k_col0, q_scale):
    bsz, seq, _ = proj3.shape
    ts = SEQ_TILE
    nt = seq // ts
    cw = 2 * LANES
    ncol = conv_q.shape[1] // cw
    hb = ts // BF16_ROWS
    n_hb = seq // BF16_ROWS

    def taps(col0):
        return [pl.BlockSpec((1, BF16_ROWS, cw), lambda b, t, j: (b, jnp.maximum(t * hb - 1, 0), col0 + j)),
                pl.BlockSpec((1, ts, cw), lambda b, t, j: (b, t, col0 + j)),
                pl.BlockSpec((1, BF16_ROWS, cw), lambda b, t, j: (b, jnp.minimum((t + 1) * hb, n_hb - 1), col0 + j))]

    wspec = pl.BlockSpec((CONV_WIDTH, cw), lambda b, t, j: (0, j))
    return pl.pallas_call(
        functools.partial(_qkconv_pair_kernel, ts=ts, q_scale=q_scale),
        grid=(bsz, nt, ncol),
        in_specs=taps(q_col0) + taps(k_col0) + [wspec, wspec],
        out_specs=[pl.BlockSpec((1, ts, cw), lambda b, t, j: (b, t, j)),
                   pl.BlockSpec((1, cw, ts), lambda b, t, j: (b, j, t))],
        out_shape=[jax.ShapeDtypeStruct((bsz, seq, ncol * cw), BF16),
                   jax.ShapeDtypeStruct((bsz, ncol * cw, seq), BF16)],
        compiler_params=_cparams(("parallel", "parallel", "parallel")),
        name="qkconv_pair",
    )(proj3, proj3, proj3, proj3, proj3, proj3, conv_q, conv_k)


def _qkconv(proj3, conv_w, col0, scale, transpose):
    bsz, seq, _ = proj3.shape
    ts = SEQ_TILE
    nt = seq // ts
    cw = 2 * LANES
    ncol = conv_w.shape[1] // cw
    hb = ts // BF16_ROWS
    n_hb = seq // BF16_ROWS
    kern = functools.partial(_qkconv_kernel, ts=ts, scale=scale, transpose=transpose)
    if transpose:
        out_shape = jax.ShapeDtypeStruct((bsz, ncol * cw, seq), BF16)
        out_spec = pl.BlockSpec((1, cw, ts), lambda b, t, j: (b, j, t))
    else:
        out_shape = jax.ShapeDtypeStruct((bsz, seq, ncol * cw), BF16)
        out_spec = pl.BlockSpec((1, ts, cw), lambda b, t, j: (b, t, j))
    return pl.pallas_call(
        kern,
        grid=(bsz, nt, ncol),
        in_specs=[pl.BlockSpec((1, BF16_ROWS, cw), lambda b, t, j: (b, jnp.maximum(t * hb - 1, 0), col0 + j)),
                  pl.BlockSpec((1, ts, cw), lambda b, t, j: (b, t, col0 + j)),
                  pl.BlockSpec((1, BF16_ROWS, cw),
                               lambda b, t, j: (b, jnp.minimum((t + 1) * hb, n_hb - 1), col0 + j)),
                  pl.BlockSpec((CONV_WIDTH, cw), lambda b, t, j: (0, j))],
        out_specs=out_spec,
        out_shape=out_shape,
        compiler_params=_cparams(("parallel", "parallel", "parallel")),
        name="qkconv_t" if transpose else "qkconv",
    )(proj3, proj3, proj3, conv_w)


def _pool_kernel(prev_ref, main_ref, next_ref, band_ref, pw_ref, ps_ref, o_ref, *, ts, seq, cw):
    t = pl.program_id(1)
    nt = pl.num_programs(1)
    prev = jnp.where(t > 0, prev_ref[0], jnp.zeros_like(prev_ref[0]))
    nxt = jnp.where(t < nt - 1, next_ref[0], jnp.zeros_like(next_ref[0]))
    ext = jnp.concatenate([prev, main_ref[0], nxt], axis=0)
    sub = POOL_SUB
    for g in range(POOL_GROUPS):
        hw = 1 << g
        cols = slice(g * cw, (g + 1) * cw)
        for j in range(ts // sub):
            rows = ext[j * sub:j * sub + sub + 2 * LANES, cols]
            s = jnp.dot(band_ref[g], rows, preferred_element_type=F32)
            tabs = t * ts + j * sub + lax.broadcasted_iota(I32, (sub, 1), 0)
            cnt = jnp.minimum(tabs + hw, seq) - jnp.maximum(tabs - hw, 0)
            centre = rows[LANES:LANES + sub].astype(F32)
            pooled = s / cnt.astype(F32) - centre
            mixed = jnp.dot(pooled.astype(BF16), pw_ref[g], preferred_element_type=F32) * ps_ref[:, cols]
            o_ref[0, j * sub:(j + 1) * sub, cols] = mixed.astype(o_ref.dtype)


def _pool(proj3, pool_w, pool_scale):
    bsz, seq, _ = proj3.shape
    ts = SEQ_TILE
    nt = seq // ts
    cw = pool_w.shape[-1]
    width = POOL_GROUPS * cw
    hb = ts // LANES
    n_hb = seq // LANES
    i = np.arange(POOL_SUB)[:, None]
    c = np.arange(POOL_SUB + 2 * LANES)[None, :] - LANES
    band = np.stack([(c >= i - (1 << g)) & (c < i + (1 << g)) for g in range(POOL_GROUPS)])
    band = jnp.asarray(band.astype(np.float32), BF16)
    kern = functools.partial(_pool_kernel, ts=ts, seq=seq, cw=cw)
    return pl.pallas_call(
        kern,
        grid=(bsz, nt),
        in_specs=[pl.BlockSpec((1, LANES, width), lambda b, t: (b, jnp.maximum(t * hb - 1, 0), 0)),
                  pl.BlockSpec((1, ts, width), lambda b, t: (b, t, 0)),
                  pl.BlockSpec((1, LANES, width), lambda b, t: (b, jnp.minimum((t + 1) * hb, n_hb - 1), 0)),
                  pl.BlockSpec(band.shape, lambda b, t: (0, 0, 0)),
                  pl.BlockSpec(pool_w.shape, lambda b, t: (0, 0, 0)),
                  pl.BlockSpec((1, width), lambda b, t: (0, 0))],
        out_specs=pl.BlockSpec((1, ts, width), lambda b, t: (b, t, 0)),
        out_shape=jax.ShapeDtypeStruct((bsz, seq, width), BF16),
        compiler_params=_cparams(("parallel", "parallel")),
        name="pool",
    )(proj3, proj3, proj3, band, pool_w, pool_scale)


def _dot_split(val_r, mask):
    hi = val_r.astype(BF16)
    lo = (val_r - hi.astype(F32)).astype(BF16)
    return (jnp.dot(hi, mask, preferred_element_type=F32) + jnp.dot(lo, mask, preferred_element_type=F32))


def _dot_split_t(mask, val_r):
    hi = val_r.astype(BF16)
    lo = (val_r - hi.astype(F32)).astype(BF16)
    nt_dims = (((1,), (1,)), ((), ()))
    return (lax.dot_general(mask, hi, nt_dims, preferred_element_type=F32)
            + lax.dot_general(mask, lo, nt_dims, preferred_element_type=F32))


def _mlstm_kernel(q_ref, kt_ref, v_ref, uo_ref, gr_ref, nw_ref, o_ref,
                  cf_ref, nf_ref, mf_ref, cb_ref, nb_ref, mb_ref, cbs_ref, nbs_ref, mbs_ref,
                  *, chunk, n_chunks, heads):
    L = chunk
    nb = q_ref.shape[0]
    dqk = kt_ref.shape[1] // heads
    dv = v_ref.shape[2] // heads
    p = pl.program_id(1)
    c = pl.program_id(2)
    row = lax.broadcasted_iota(I32, (L, L), 0)
    col = lax.broadcasted_iota(I32, (L, L), 1)
    tri_le = row <= col
    tri_ge = row >= col
    m_le = jnp.where(tri_le, 1.0, 0.0).astype(BF16)
    m_ge = jnp.where(tri_ge, 1.0, 0.0).astype(BF16)
    lane_r = lax.broadcasted_iota(I32, (1, L), 1)
    neg_inf = jnp.float32(-jnp.inf)

    chains = [(bb, hd) for bb in range(nb) for hd in range(heads)]
    hs = range(len(chains))
    kts = [kt_ref[bb, hd * dqk:(hd + 1) * dqk, :] for bb, hd in chains]
    vs = [v_ref[bb, :, hd * dv:(hd + 1) * dv] for bb, hd in chains]
    gates = [gr_ref[bb, hd * SUBLANES:(hd + 1) * SUBLANES, :] for bb, hd in chains]
    lf_r = [jax.nn.log_sigmoid(g) for g in gates]

    def update_state(c_ref, n_ref, m_ref, g_r, tot):
        m_prev = [m_ref[hh] for hh in hs]
        m_new = [jnp.maximum(tot[hh] + m_prev[hh], jnp.max(g_r[hh], axis=1, keepdims=True)) for hh in hs]
        decay = [jnp.exp(tot[hh] + m_prev[hh] - m_new[hh]) for hh in hs]
        kw = [kts[hh].astype(F32) * jnp.exp(g_r[hh] - m_new[hh]) for hh in hs]
        upd = [jnp.dot(kw[hh].astype(BF16), vs[hh], preferred_element_type=F32) for hh in hs]
        for hh in hs:
            c_ref[hh] = decay[hh] * c_ref[hh] + upd[hh]
            n_ref[hh] = decay[hh] * n_ref[hh] + jnp.sum(kw[hh], axis=1, keepdims=True)
            m_ref[hh] = m_new[hh]

    @pl.when(p == 0)
    def _backward_states():
        @pl.when(c == 0)
        def _():
            cb_ref[...] = jnp.zeros_like(cb_ref)
            nb_ref[...] = jnp.zeros_like(nb_ref)
            mb_ref[...] = jnp.zeros_like(mb_ref)

        cc = n_chunks - 1 - c
        cbs_ref[cc] = cb_ref[...].astype(BF16)
        nbs_ref[cc] = nb_ref[...]
        mbs_ref[cc] = mb_ref[...]
        a_r = [_dot_split(lf_r[hh], m_ge)[3:4] for hh in hs]
        a0 = [jnp.sum(jnp.where(lane_r == 0, a_r[hh], 0.0), axis=1, keepdims=True) for hh in hs]
        g_r = [a0[hh] - a_r[hh] + gates[hh][2:3] for hh in hs]
        update_state(cb_ref, nb_ref, mb_ref, g_r, a0)

    @pl.when(p == 1)
    def _outputs():
        @pl.when(c == 0)
        def _():
            cf_ref[...] = jnp.zeros_like(cf_ref)
            nf_ref[...] = jnp.zeros_like(nf_ref)
            mf_ref[...] = jnp.zeros_like(mf_ref)

        nb_in = nbs_ref[c]
        mb_in = mbs_ref[c]
        cb_in = cbs_ref[c]
        qs = [q_ref[bb, :, hd * dqk:(hd + 1) * dqk] for bb, hd in chains]
        b_r = [_dot_split(lf_r[hh], m_le)[1:2] for hh in hs]
        a_r = [_dot_split(lf_r[hh], m_ge)[3:4] for hh in hs]
        b_c = [_dot_split_t(m_ge, lf_r[hh])[:, 1:2] for hh in hs]
        a_c = [_dot_split_t(m_le, lf_r[hh])[:, 3:4] for hh in hs]
        li_f = [gates[hh][0:1] for hh in hs]
        li_b = [gates[hh][2:3] for hh in hs]

        s = [jnp.dot(qs[hh], kts[hh], preferred_element_type=F32) for hh in hs]
        nlane = lax.broadcasted_iota(I32, (dqk, LANES), 1)
        nmat = [jnp.where(nlane == 0, nf_ref[hh], jnp.where(nlane == 1, nb_in[hh], 0.0)).astype(BF16)
                for hh in hs]
        qn = [jnp.dot(qs[hh], nmat[hh], preferred_element_type=F32) for hh in hs]

        def direction(d, mask, cum_c, m_prev, qn_col):
            d = [jnp.where(mask, d[hh], neg_inf) for hh in hs]
            m_inter = [cum_c[hh] + m_prev[hh] for hh in hs]
            m_t = [jnp.maximum(m_inter[hh], jnp.max(d[hh], axis=1, keepdims=True)) for hh in hs]
            pmat = [jnp.exp(d[hh] - m_t[hh]) * s[hh] for hh in hs]
            w_inter = [jnp.exp(m_inter[hh] - m_t[hh]) for hh in hs]
            den = [jnp.sum(pmat[hh], axis=1, keepdims=True) + w_inter[hh] * qn_col[hh] for hh in hs]
            r = [1.0 / jnp.maximum(jnp.abs(den[hh]), jnp.exp(-m_t[hh])) for hh in hs]
            return [pmat[hh] * r[hh] for hh in hs], [w_inter[hh] * r[hh] for hh in hs]

        pf, sf = direction([b_c[hh] - (b_r[hh] - li_f[hh]) for hh in hs], tri_ge, b_c,
                           [mf_ref[hh] for hh in hs], [qn[hh][:, 0:1] for hh in hs])
        pb, sb = direction([a_c[hh] - (a_r[hh] - li_b[hh]) for hh in hs], tri_le, a_c,
                           [mb_in[hh] for hh in hs], [qn[hh][:, 1:2] for hh in hs])
        qf = [qs[hh].astype(F32) for hh in hs]
        h = [jnp.dot((pf[hh] + pb[hh]).astype(BF16), vs[hh], preferred_element_type=F32)
             + jnp.dot((qf[hh] * sf[hh]).astype(BF16), cf_ref[hh].astype(BF16), preferred_element_type=F32)
             + jnp.dot((qf[hh] * sb[hh]).astype(BF16), cb_in[hh], preferred_element_type=F32) for hh in hs]

        mu = [jnp.mean(h[hh], axis=1, keepdims=True) for hh in hs]
        hc = [h[hh] - mu[hh] for hh in hs]
        var = [jnp.mean(hc[hh] * hc[hh], axis=1, keepdims=True) for hh in hs]
        hn = [hc[hh] * lax.rsqrt(var[hh] + LN_EPS) * nw_ref[chains[hh][1]] for hh in hs]
        for hh, (bb, hd) in enumerate(chains):
            gate_o = _sigmoid(uo_ref[bb, :, hd * dv:(hd + 1) * dv].astype(F32))
            o_ref[bb, :, hd * dv:(hd + 1) * dv] = (gate_o * hn[hh]).astype(o_ref.dtype)

        b_last = [jnp.sum(jnp.where(lane_r == L - 1, b_r[hh], 0.0), axis=1, keepdims=True) for hh in hs]
        update_state(cf_ref, nf_ref, mf_ref, [b_last[hh] - b_r[hh] + li_f[hh] for hh in hs], b_last)


def _mlstm(q, kt, proj3, gates, norm_w, v_off, o_off):
    bsz, seq, qw = q.shape
    dqk = qw // N_HEADS
    dv = norm_w.shape[-1]
    L = MLSTM_CHUNK
    nc = seq // L
    hg = MLSTM_HEADS_PER_STEP
    bg = MLSTM_BATCH_PER_STEP
    groups = N_HEADS // hg
    assert N_HEADS % hg == 0 and bsz % bg == 0 and v_off % (hg * dv) == 0 and o_off % (hg * dv) == 0
    v_blk0 = v_off // (hg * dv)
    o_blk0 = o_off // (hg * dv)
    kern = functools.partial(_mlstm_kernel, chunk=L, n_chunks=nc, heads=hg)

    def chunk_of(p, c):
        return jnp.where(p == 0, nc - 1 - c, c)

    def out_chunk(p, c):
        return jnp.where(p == 0, 0, c)

    return pl.pallas_call(
        kern,
        grid=((bsz // bg) * groups, 2, nc),
        in_specs=[
            pl.BlockSpec((bg, L, hg * dqk), lambda g, p, c: (g // groups, out_chunk(p, c), g % groups)),
            pl.BlockSpec((bg, hg * dqk, L), lambda g, p, c: (g // groups, g % groups, chunk_of(p, c))),
            pl.BlockSpec((bg, L, hg * dv), lambda g, p, c: (g // groups, chunk_of(p, c), v_blk0 + g % groups)),
            pl.BlockSpec((bg, L, hg * dv), lambda g, p, c: (g // groups, out_chunk(p, c), o_blk0 + g % groups)),
            pl.BlockSpec((bg, hg * SUBLANES, L), lambda g, p, c: (g // groups, g % groups, chunk_of(p, c))),
            pl.BlockSpec((hg, 1, dv), lambda g, p, c: (g % groups, 0, 0)),
        ],
        out_specs=pl.BlockSpec((bg, L, hg * dv), lambda g, p, c: (g // groups, out_chunk(p, c), g % groups)),
        out_shape=jax.ShapeDtypeStruct((bsz, seq, N_HEADS * dv), BF16),
        scratch_shapes=[
            pltpu.VMEM((bg * hg, dqk, dv), F32), pltpu.VMEM((bg * hg, dqk, 1), F32),
            pltpu.VMEM((bg * hg, 1, 1), F32),
            pltpu.VMEM((bg * hg, dqk, dv), F32), pltpu.VMEM((bg * hg, dqk, 1), F32),
            pltpu.VMEM((bg * hg, 1, 1), F32),
            pltpu.VMEM((nc, bg * hg, dqk, dv), BF16), pltpu.VMEM((nc, bg * hg, dqk, 1), F32),
            pltpu.VMEM((nc, bg * hg, 1, 1), F32),
        ],
        compiler_params=_cparams(("parallel", "arbitrary", "arbitrary")),
        name="mlstm",
    )(q, kt, proj3, proj3, gates, norm_w)


def _mixout_kernel(hg_ref, a_ref, uga_ref, ugb_ref, x_ref, wbd_ref, wo_ref, bga_ref, bgb_ref,
                   g_ref, b_ref, o_ref, op_ref, *, alpha):
    branch_b = jnp.dot(hg_ref[...], wbd_ref[...], preferred_element_type=F32)
    ga = _sigmoid(uga_ref[...].astype(F32) + bga_ref[...])
    gb = _sigmoid(ugb_ref[...].astype(F32) + bgb_ref[...])
    merged = ga * a_ref[...].astype(F32) + gb * branch_b
    mix = jnp.dot(merged.astype(BF16), wo_ref[...], preferred_element_type=F32)
    y = _layer_norm(alpha * x_ref[...] + mix, g_ref[...], b_ref[...])
    o_ref[...] = y
    op_ref[...] = _pack_bf16_pairs(y)


def _mixout(hg, branch_a, proj, x, w_b_down, w_out, b_gate, ln_g, ln_b, ga_col, alpha):
    t, d = x.shape
    tm = ROW_TILE
    inner = hg.shape[1]
    row = lambda i: (i, 0)
    const = lambda i: (0, 0)
    return pl.pallas_call(
        functools.partial(_mixout_kernel, alpha=alpha),
        grid=(t // tm,),
        in_specs=[pl.BlockSpec((tm, inner), row),
                  pl.BlockSpec((tm, d), row),
                  pl.BlockSpec((tm, d), lambda i: (i, ga_col)),
                  pl.BlockSpec((tm, d), lambda i: (i, ga_col + 1)),
                  pl.BlockSpec((tm, d), row),
                  pl.BlockSpec((inner, d), const),
                  pl.BlockSpec((d, d), const),
                  pl.BlockSpec((1, d), const),
                  pl.BlockSpec((1, d), lambda i: (0, 1)),
                  pl.BlockSpec((1, d), const),
                  pl.BlockSpec((1, d), const)],
        out_specs=[pl.BlockSpec((tm, d), row), pl.BlockSpec((tm, d // 2), row)],
        out_shape=[jax.ShapeDtypeStruct((t, d), F32), jax.ShapeDtypeStruct((t, d // 2), U32)],
        compiler_params=_cparams(("parallel",)),
        name="mixout",
    )(hg, branch_a, proj, proj, x, w_b_down, w_out, b_gate, b_gate, ln_g, ln_b)


def _router_kernel(x_ref, wh_ref, wl_ref, br_ref, eidx_ref, wts_ref, rank_ref, cnt_ref, run_ref,
                   *, n_experts):
    i = pl.program_id(0)
    tm = x_ref.shape[0]
    ne = n_experts
    per_group = ne // N_ROUTE_GROUPS

    @pl.when(i == 0)
    def _():
        run_ref[...] = jnp.zeros_like(run_ref)

    x = x_ref[...]
    xh = x.astype(BF16)
    xl = (x - xh.astype(F32)).astype(BF16)
    nt_dims = (((1,), (1,)), ((), ()))
    logits = (lax.dot_general(wh_ref[...], xh, nt_dims, preferred_element_type=F32)
              + lax.dot_general(wh_ref[...], xl, nt_dims, preferred_element_type=F32)
              + lax.dot_general(wl_ref[...], xh, nt_dims, preferred_element_type=F32))
    scores = _sigmoid(logits)
    sel = scores + br_ref[...]
    neg = jnp.float32(-jnp.inf)
    big = jnp.float32(1e9)
    rowi = lax.broadcasted_iota(I32, (ne, tm), 0).astype(F32)

    gi = lax.broadcasted_iota(I32, (N_ROUTE_GROUPS, tm), 0).astype(F32)
    work = jnp.zeros((N_ROUTE_GROUPS, tm), F32)
    for g in range(N_ROUTE_GROUPS):
        blk = sel[g * per_group:(g + 1) * per_group]
        ri = lax.broadcasted_iota(I32, (per_group, tm), 0).astype(F32) + float(g * per_group)
        m1 = jnp.max(blk, axis=0, keepdims=True)
        i1 = jnp.min(jnp.where(blk == m1, ri, big), axis=0, keepdims=True)
        m2 = jnp.max(jnp.where(ri == i1, neg, blk), axis=0, keepdims=True)
        work = jnp.where(gi == float(g), m1 + m2, work)
    row_group = jnp.floor(rowi * (1.0 / per_group))
    allowed = jnp.zeros((ne, tm), F32)
    for _ in range(TOPK_GROUPS):
        m = jnp.max(work, axis=0, keepdims=True)
        idx = jnp.min(jnp.where(work == m, gi, big), axis=0, keepdims=True)
        work = jnp.where(gi == idx, neg, work)
        allowed = jnp.where(row_group == idx, 1.0, allowed)
    selm = jnp.where(allowed > 0.5, sel, neg)

    member = jnp.zeros((ne, tm), F32)
    idxs, wks = [], []
    for _ in range(TOP_K):
        m = jnp.max(selm, axis=0, keepdims=True)
        idx = jnp.min(jnp.where(selm == m, rowi, big), axis=0, keepdims=True)
        hit = rowi == idx
        wks.append(jnp.sum(jnp.where(hit, scores, 0.0), axis=0, keepdims=True))
        idxs.append(idx)
        member = jnp.where(hit, 1.0, member)
        selm = jnp.where(hit, neg, selm)
    wsum = wks[0]
    for wk in wks[1:]:
        wsum = wsum + wk

    ti = lax.broadcasted_iota(I32, (tm, tm), 0)
    tj = lax.broadcasted_iota(I32, (tm, tm), 1)
    strict = jnp.where(ti < tj, 1.0, 0.0).astype(BF16)
    prefix = jnp.dot(member.astype(BF16), strict, preferred_element_type=F32) + run_ref[...]
    ranks = [jnp.sum(jnp.where(rowi == idx, prefix, 0.0), axis=0, keepdims=True) for idx in idxs]
    run_new = run_ref[...] + jnp.sum(member, axis=1, keepdims=True)
    run_ref[...] = run_new

    eidx_ref[...] = jnp.concatenate(idxs, axis=0).astype(I32)
    wts_ref[...] = jnp.concatenate([wk / wsum * ROUTED_SCALE for wk in wks], axis=0)
    rank_ref[...] = jnp.concatenate(ranks, axis=0).astype(I32)
    cnt_ref[...] = jnp.broadcast_to(run_new, cnt_ref.shape).astype(I32)


def _router(x1, wr_hi, wr_lo, b_router):
    t, d = x1.shape
    ne = wr_hi.shape[0]
    tm = ROUTER_TILE
    kern = functools.partial(_router_kernel, n_experts=ne)
    tok = lambda i: (0, i)
    const = lambda i: (0, 0)
    return pl.pallas_call(
        kern,
        grid=(t // tm,),
        in_specs=[pl.BlockSpec((tm, d), lambda i: (i, 0)),
                  pl.BlockSpec((ne, d), const),
                  pl.BlockSpec((ne, d), const),
                  pl.BlockSpec((ne, 1), const)],
        out_specs=[pl.BlockSpec((TOP_K, tm), tok), pl.BlockSpec((TOP_K, tm), tok),
                   pl.BlockSpec((TOP_K, tm), tok), pl.BlockSpec((ne, LANES), const)],
        out_shape=[jax.ShapeDtypeStruct((TOP_K, t), I32), jax.ShapeDtypeStruct((TOP_K, t), F32),
                   jax.ShapeDtypeStruct((TOP_K, t), I32), jax.ShapeDtypeStruct((ne, LANES), I32)],
        scratch_shapes=[pltpu.VMEM((ne, 1), F32)],
        compiler_params=_cparams(("arbitrary",)),
        name="router",
    )(x1, wr_hi, wr_lo, b_router)


def _slots_kernel(eidx_ref, rank_ref, ps_ref, pos_ref, *, n_experts):
    tm = eidx_ref.shape[1]
    rowi = lax.broadcasted_iota(I32, (n_experts, tm), 0)
    eidx = eidx_ref[...]
    starts = ps_ref[...]
    base = [jnp.sum(jnp.where(rowi == eidx[k:k + 1], starts, 0.0), axis=0, keepdims=True)
            for k in range(TOP_K)]
    pos_ref[...] = jnp.concatenate(base, axis=0).astype(I32) + rank_ref[...]


def _slots(eidx, rank, pstarts):
    k, t = eidx.shape
    ne = pstarts.shape[0]
    tm = SLOT_TILE
    tok = lambda i: (0, i)
    return pl.pallas_call(
        functools.partial(_slots_kernel, n_experts=ne),
        grid=(t // tm,),
        in_specs=[pl.BlockSpec((k, tm), tok), pl.BlockSpec((k, tm), tok),
                  pl.BlockSpec((ne, 1), lambda i: (0, 0))],
        out_specs=pl.BlockSpec((k, tm), tok),
        out_shape=jax.ShapeDtypeStruct((k, t), I32),
        compiler_params=_cparams(("parallel",)),
        name="slots",
    )(eidx, rank, pstarts.astype(F32).reshape(ne, 1))


def _sc_worker_id():
    return lax.axis_index("subcore") * SC_CORES + lax.axis_index("core")


def _sc_mesh():
    return plsc.VectorSubcoreMesh(core_axis_name="core", subcore_axis_name="subcore")


def _sc_vector_params():
    cp = pltpu.CompilerParams()
    if "needs_layout_passes" in pltpu.CompilerParams.__dataclass_fields__:
        cp = dataclasses.replace(cp, needs_layout_passes=False)
    return cp


def _sc_gather(table, idx):
    n = idx.shape[0]
    w = table.shape[1]
    ch = SC_GATHER_ROWS
    n_ch = n // (SC_WORKERS * ch)
    assert n % (SC_WORKERS * ch * 2) == 0

    @functools.partial(
        pl.kernel, mesh=_sc_mesh(),
        out_type=jax.ShapeDtypeStruct((n, w), table.dtype),
        scratch_types=[pltpu.VMEM((n_ch, ch), I32), pltpu.VMEM((2, ch, w), table.dtype),
                       pltpu.SemaphoreType.DMA((2,))],
    )
    def kern(table_hbm, idx_hbm, out_hbm, idx_v, rows_v, sem):
        first = _sc_worker_id() * n_ch
        pltpu.sync_copy(idx_hbm.at[pl.ds(first, n_ch)], idx_v)

        def gather(j, b):
            return pltpu.make_async_copy(table_hbm.at[idx_v.at[j]], rows_v.at[b], sem.at[b])

        gather(0, 0).start()

        @pl.loop(0, n_ch, step=2)
        def _(j):
            for b in range(2):
                jj = j + b

                @pl.when(jj + 1 < n_ch)
                def _():
                    gather(jj + 1, 1 - b).start()

                gather(jj, b).wait()
                row0 = pl.multiple_of((first + jj) * ch, ch)
                pltpu.sync_copy(rows_v.at[b], out_hbm.at[pl.ds(row0, ch)])

    return kern(table, idx.reshape(n // ch, ch))


def _sc_invert(pos_flat, n_slots, n_tokens):
    n = pos_flat.shape[0]
    per_w = n_slots // SC_WORKERS
    chunk = SC_SCAN_CHUNK
    assert n_slots % (SC_WORKERS * SC_LANES) == 0 and n % chunk == 0 and n_tokens % chunk == 0

    @functools.partial(
        pl.kernel, mesh=_sc_mesh(),
        out_type=jax.ShapeDtypeStruct((n_slots,), I32),
        scratch_types=[pltpu.VMEM((per_w,), I32), pltpu.VMEM((chunk,), I32)],
        compiler_params=_sc_vector_params(),
    )
    def kern(pos_hbm, out_hbm, table_v, pos_v):
        lo = _sc_worker_id() * per_w
        lane = lax.iota(I32, SC_LANES)

        @pl.loop(0, per_w // SC_LANES)
        def _(i):
            off = pl.multiple_of(i * SC_LANES, SC_LANES)
            table_v[pl.ds(off, SC_LANES)] = lax.rem(lo + off + lane, jnp.full((SC_LANES,), n_tokens, I32))

        @pl.loop(0, n // chunk)
        def _(c):
            pltpu.sync_copy(pos_hbm.at[pl.ds(pl.multiple_of(c * chunk, chunk), chunk)], pos_v)
            tok0 = lax.rem(c, n_tokens // chunk) * chunk

            @plsc.parallel_loop(0, chunk // SC_LANES, 1, unroll=SC_SCAN_UNROLL)
            def _(j):
                off = pl.multiple_of(j * SC_LANES, SC_LANES)
                local = pos_v[pl.ds(off, SC_LANES)] - lo
                mine = (local >= 0) & (local < per_w)
                plsc.store_scatter(table_v, [jnp.where(mine, local, 0)], tok0 + off + lane, mask=mine)

        pltpu.sync_copy(table_v, out_hbm.at[pl.ds(pl.multiple_of(lo, SC_LANES), per_w)])

    return kern(pos_flat)


def _experts_kernel(te_ref, nu_ref, ord_ref, nxt_ref, nxt2_ref, xs_hbm, wg_hbm, wu_hbm, wd_hbm, ys_ref,
                    xbuf_ref, wgf_ref, wuf_ref, wdf_ref, wgb_ref, wub_ref, wdb_ref, sem_ref, xsem_ref):
    i = pl.program_id(0)
    n_used = nu_ref[0]
    tile = xbuf_ref.shape[1]

    def row_copy(step):
        slot = lax.rem(step, ROW_SLOTS)
        src = xs_hbm.at[pl.ds(pl.multiple_of(step * tile, tile), tile)]
        return pltpu.make_async_copy(src, xbuf_ref.at[slot], xsem_ref.at[slot])

    def weight_copies(e, slot):
        return (pltpu.make_async_copy(wg_hbm.at[e], wgf_ref.at[slot], sem_ref.at[slot]),
                pltpu.make_async_copy(wu_hbm.at[e], wuf_ref.at[slot], sem_ref.at[slot]),
                pltpu.make_async_copy(wd_hbm.at[e], wdf_ref.at[slot], sem_ref.at[slot]))

    @pl.when(i < n_used)
    def _():
        @pl.when(i == 0)
        def _():
            for ahead in range(ROW_SLOTS - 1):
                @pl.when(ahead < n_used)
                def _():
                    row_copy(ahead).start()

        @pl.when(i + (ROW_SLOTS - 1) < n_used)
        def _():
            row_copy(i + (ROW_SLOTS - 1)).start()

        e = te_ref[i]
        ordinal = ord_ref[i]
        slot = lax.rem(ordinal, WEIGHT_SLOTS)
        first_tile_of_expert = jnp.logical_or(i == 0, e != te_ref[jnp.maximum(i - 1, 0)])

        @pl.when(i == 0)
        def _():
            for cp in weight_copies(e, slot):
                cp.start(priority=WEIGHT_DMA_PRIORITY)

            @pl.when(nxt_ref[i] >= 0)
            def _():
                for cp in weight_copies(nxt_ref[i], lax.rem(ordinal + 1, WEIGHT_SLOTS)):
                    cp.start(priority=WEIGHT_DMA_PRIORITY)

        @pl.when(first_tile_of_expert)
        def _():
            ahead = nxt2_ref[i]

            @pl.when(ahead >= 0)
            def _():
                for cp in weight_copies(ahead, lax.rem(ordinal + 2, WEIGHT_SLOTS)):
                    cp.start(priority=WEIGHT_DMA_PRIORITY)

            for cp in weight_copies(e, slot):
                cp.wait()
            wgb_ref[...] = wgf_ref[slot].astype(BF16)
            wub_ref[...] = wuf_ref[slot].astype(BF16)
            wdb_ref[...] = wdf_ref[slot].astype(BF16)

        row_copy(i).wait()
        x = _unpack_bf16_pairs(xbuf_ref[lax.rem(i, ROW_SLOTS)]).astype(BF16)
        gate = jnp.dot(x, wgb_ref[...], preferred_element_type=F32)
        up = jnp.dot(x, wub_ref[...], preferred_element_type=F32)
        hid = (gate * _sigmoid(gate) * up).astype(BF16)
        y = jnp.dot(hid, wdb_ref[...], preferred_element_type=F32)
        ys_ref[...] = _pack_bf16_pairs(y)

    @pl.when(i >= n_used)
    def _():
        ys_ref[...] = jnp.zeros_like(ys_ref)


def _experts(tile_expert, n_used, expert_ord, expert_next, expert_next2, xs, w_gate_e, w_up_e, w_down_e):
    n_slots, w = xs.shape
    tile = EXPERT_TILE
    n_tiles = n_slots // tile
    _, d, de = w_gate_e.shape
    grid_spec = pltpu.PrefetchScalarGridSpec(
        num_scalar_prefetch=5,
        grid=(n_tiles,),
        in_specs=[pl.BlockSpec(memory_space=pl.ANY),
                  pl.BlockSpec(memory_space=pl.ANY),
                  pl.BlockSpec(memory_space=pl.ANY),
                  pl.BlockSpec(memory_space=pl.ANY)],
        out_specs=pl.BlockSpec((tile, w), lambda i, *_: (i, 0)),
        scratch_shapes=[pltpu.VMEM((ROW_SLOTS, tile, w), U32),
                        pltpu.VMEM((WEIGHT_SLOTS, d, de), F32), pltpu.VMEM((WEIGHT_SLOTS, d, de), F32),
                        pltpu.VMEM((WEIGHT_SLOTS, de, d), F32),
                        pltpu.VMEM((d, de), BF16), pltpu.VMEM((d, de), BF16), pltpu.VMEM((de, d), BF16),
                        pltpu.SemaphoreType.DMA((WEIGHT_SLOTS,)), pltpu.SemaphoreType.DMA((ROW_SLOTS,))],
    )
    return pl.pallas_call(
        _experts_kernel,
        grid_spec=grid_spec,
        out_shape=jax.ShapeDtypeStruct((n_slots, w), U32),
        compiler_params=_cparams(("arbitrary",)),
        name="experts",
    )(tile_expert, n_used, expert_ord, expert_next, expert_next2, xs, w_gate_e, w_up_e, w_down_e)


def _combine_kernel(wts_ref, x_ref, xp_ref, yg_ref, wgs_ref, wus_ref, wds_ref, g_ref, b_ref, o_ref, *, alpha):
    wts = wts_ref[...]
    routed = jnp.zeros(x_ref.shape, F32)
    for k in range(TOP_K):
        routed = routed + wts[:, k:k + 1] * _unpack_bf16_pairs(yg_ref[k])
    xb = _unpack_bf16_pairs(xp_ref[...]).astype(BF16)
    gate = jnp.dot(xb, wgs_ref[...], preferred_element_type=F32)
    up = jnp.dot(xb, wus_ref[...], preferred_element_type=F32)
    hid = (gate * _sigmoid(gate) * up).astype(BF16)
    shared = jnp.dot(hid, wds_ref[...], preferred_element_type=F32)
    o_ref[...] = _layer_norm(alpha * x_ref[...] + (routed + shared), g_ref[...], b_ref[...])


def _combine(wts_c, x1, x1p, yg, w_gate_s, w_up_s, w_down_s, ln_g, ln_b, alpha):
    t, d = x1.shape
    w = x1p.shape[1]
    tb = COMBINE_TILE
    ds = w_gate_s.shape[1]
    row = lambda i: (i, 0)
    const = lambda i: (0, 0)
    return pl.pallas_call(
        functools.partial(_combine_kernel, alpha=alpha),
        grid=(t // tb,),
        in_specs=[pl.BlockSpec((tb, TOP_K), row),
                  pl.BlockSpec((tb, d), row),
                  pl.BlockSpec((tb, w), row),
                  pl.BlockSpec((TOP_K, tb, w), lambda i: (0, i, 0)),
                  pl.BlockSpec((d, ds), const),
                  pl.BlockSpec((d, ds), const),
                  pl.BlockSpec((ds, d), const),
                  pl.BlockSpec((1, d), const),
                  pl.BlockSpec((1, d), const)],
        out_specs=pl.BlockSpec((tb, d), row),
        out_shape=jax.ShapeDtypeStruct((t, d), F32),
        compiler_params=_cparams(("parallel",)),
        name="combine",
    )(wts_c, x1, x1p, yg, w_gate_s, w_up_s, w_down_s, ln_g, ln_b)


def _layer(alpha, x, w_in, b_if, b_gate, conv_qk, pool_w, pool_scale, mh_norm_w, w_b_down, w_out,
           ln1_g, ln1_b, w_router, b_router, w_gate_e, w_up_e, w_down_e,
           w_gate_s, w_up_s, w_down_s, ln2_g, ln2_b):
    bsz, seq, d = x.shape
    t = bsz * seq
    heads = N_HEADS
    pool_width = pool_w.shape[0] * pool_w.shape[1]
    qk_cols = conv_qk.shape[1]
    v_cols = mh_norm_w.shape[0] * mh_norm_w.shape[1]
    o_cols = v_cols
    if_cols = b_if.shape[0]
    dv = mh_norm_w.shape[1]
    dqk = qk_cols // (2 * heads)
    off_if = pool_width + qk_cols + v_cols + o_cols
    off_gate = off_if + if_cols

    xf = x.reshape(t, d)
    w_main = jnp.concatenate([w_in[:, :off_if], w_in[:, off_gate:]], axis=1).astype(BF16)
    n_gate = if_cols // heads
    w_if = w_in[:, off_if:off_gate].reshape(d, n_gate, heads).transpose(2, 1, 0)
    w_if = jnp.pad(w_if, ((0, 0), (0, SUBLANES - n_gate), (0, 0))).reshape(heads * SUBLANES, d).astype(BF16)
    bias_if = jnp.pad(b_if.reshape(n_gate, heads).T, ((0, 0), (0, SUBLANES - n_gate))).reshape(heads * SUBLANES, 1)
    n_main = w_main.shape[1]

    proj, gates = _inproj(xf, w_main, w_if, bias_if, seq)
    proj3 = proj.reshape(bsz, seq, n_main)

    branch_a = _pool(proj3, pool_w.astype(BF16), pool_scale.reshape(1, pool_width))

    cw = 2 * LANES
    q_col0 = pool_width // cw
    half = qk_cols // 2
    q, kt = _qkconv_pair(proj3, conv_qk[:, :half], conv_qk[:, half:], q_col0, q_col0 + half // cw,
                         float(dqk) ** -0.5)

    hg = _mlstm(q, kt, proj3, gates, mh_norm_w.reshape(heads, 1, dv),
                pool_width + qk_cols, pool_width + qk_cols + v_cols)

    ga_col = (pool_width + qk_cols + v_cols + o_cols) // d
    x1, x1p = _mixout(hg.reshape(t, heads * dv), branch_a.reshape(t, pool_width), proj, xf,
                      w_b_down.astype(BF16), w_out.astype(BF16), b_gate.reshape(1, 2 * d),
                      ln1_g.reshape(1, d), ln1_b.reshape(1, d), ga_col, alpha)

    ne = w_router.shape[1]
    wr_t = w_router.T
    wr_hi = wr_t.astype(BF16)
    wr_lo = (wr_t - wr_hi.astype(F32)).astype(BF16)
    eidx, wts, rank, cnt = _router(x1, wr_hi, wr_lo, b_router.reshape(ne, 1))

    tile = EXPERT_TILE
    n_tiles = (t * TOP_K) // tile + ne
    counts = cnt[:, 0]
    pcounts = ((counts + tile - 1) // tile) * tile
    pends = jnp.cumsum(pcounts)
    pstarts = pends - pcounts
    pos = _slots(eidx, rank, pstarts)

    n_used = (pends[-1] // tile).astype(I32)
    tile_ids = jnp.minimum(jnp.arange(n_tiles, dtype=I32), n_used - 1)
    tile_expert = jnp.sum((pends[None, :] <= (tile_ids * tile)[:, None]).astype(I32), axis=1)
    tile_expert = jnp.minimum(tile_expert, ne - 1)
    new_expert = jnp.concatenate([jnp.ones((1,), I32), (tile_expert[1:] != tile_expert[:-1]).astype(I32)])
    expert_ord = jnp.cumsum(new_expert) - 1
    candidates = jnp.where(counts > 0, jnp.arange(ne, dtype=I32), ne)
    later_min = lax.cummin(candidates, axis=0, reverse=True)
    next_used = jnp.concatenate([later_min[1:], jnp.full((1,), ne, I32)])
    next_used = jnp.where(next_used >= ne, -1, next_used)
    next_used2 = jnp.where(next_used >= 0, next_used[jnp.maximum(next_used, 0)], -1)
    tile_onehot = tile_expert[:, None] == jnp.arange(ne, dtype=I32)[None, :]
    expert_next = jnp.sum(jnp.where(tile_onehot, next_used[None, :], 0), axis=1).astype(I32)
    expert_next2 = jnp.sum(jnp.where(tile_onehot, next_used2[None, :], 0), axis=1).astype(I32)

    n_slots = n_tiles * tile
    slot_tok = _sc_invert(pos.reshape(-1), n_slots, t)
    xs = _sc_gather(x1p, slot_tok)
    ys = _experts(tile_expert, n_used.reshape(1), expert_ord.astype(I32), expert_next, expert_next2, xs,
                  w_gate_e, w_up_e, w_down_e)
    yg = _sc_gather(ys, pos.reshape(-1)).reshape(TOP_K, t, x1p.shape[1])
    out = _combine(wts.T, x1, x1p, yg, w_gate_s.astype(BF16), w_up_s.astype(BF16),
                   w_down_s.astype(BF16), ln2_g.reshape(1, d), ln2_b.reshape(1, d), alpha)
    return out.reshape(bsz, seq, d)


def kernel(x, w_in, b_if, b_gate, conv_qk, pool_w, pool_scale, mh_norm_w, w_b_down, w_out, ln1_g, ln1_b,
           w_router, b_router, w_gate_e, w_up_e, w_down_e, w_gate_s, w_up_s, w_down_s, ln2_g, ln2_b):
    depth = w_in.shape[0]
    alpha = (2.0 * depth) ** 0.25
    for l in range(depth):
        x = _layer(alpha, x, w_in[l], b_if[l], b_gate[l], conv_qk[l], pool_w[l], pool_scale[l], mh_norm_w[l],
                   w_b_down[l], w_out[l], ln1_g[l], ln1_b[l], w_router[l], b_router[l], w_gate_e[l],
                   w_up_e[l], w_down_e[l], w_gate_s[l], w_up_s[l], w_down_s[l], ln2_g[l], ln2_b[l])
    return x
```
